```python
import math
import jax, jax.numpy as jnp
from jax import lax
import numpy as np

D_MODEL = 1024
BATCH = 1
SEQ = 16384
DEPTH = 1
DEC_BATCH = 32
DEC_SEQ = 4
PAST_LEN = 16384
PAGE_SIZE = 128

HEAD_DIM = 64
H_A = 8
H_B = 8
W_A = H_A * HEAD_DIM
W_B = H_B * HEAD_DIM
ROT_FRACTION = 4
ROPE_THETA = 500000.0
H_IDX = 8
D_IDX = 32
TOPK_MAX = 256
Q_BLOCK = 128
PEER_HEADS = 8
PEER_NKEYS = 128
PEER_EXPERTS = PEER_NKEYS * PEER_NKEYS
PEER_DQ = 128
PEER_TOPK = 16
PEER_BLOCK = 128
PLE_DIM = 256
DN_ALPHA = (2.0 * DEPTH) ** 0.25
DN_BETA = (8.0 * DEPTH) ** -0.25
LN_EPS = 1e-5
IN_SIZES = (W_A, W_A, W_A, H_IDX * D_IDX, D_IDX, H_IDX, W_B, W_B, W_B, D_MODEL, D_MODEL)
N_IN = sum(IN_SIZES)
SPLIT_POINTS = tuple(int(s) for s in np.cumsum(IN_SIZES)[:-1])

kernel_name = "dsa_stickbreak_peer_deepnorm_step"


def _layer_norm(x, g, b):
    xf = x.astype(jnp.float32)
    mu = jnp.mean(xf, axis=-1, keepdims=True)
    var = jnp.mean(jnp.square(xf - mu), axis=-1, keepdims=True)
    return ((xf - mu) * lax.rsqrt(var + LN_EPS) * g + b).astype(x.dtype)


def _rope(x, pos):
    d = x.shape[-1]
    r = d // ROT_FRACTION
    half = r // 2
    inv = ROPE_THETA ** (-(jnp.arange(half, dtype=jnp.float32) * 2.0 / r))
    ang = pos.astype(jnp.float32)[:, None] * inv[None, :]
    cos = jnp.cos(ang)[:, None, :]
    sin = jnp.sin(ang)[:, None, :]
    xr = x[..., :r].astype(jnp.float32)
    x1, x2 = xr[..., :half], xr[..., half:]
    rot = jnp.concatenate([x1 * cos - x2 * sin, x2 * cos + x1 * sin], axis=-1).astype(x.dtype)
    return jnp.concatenate([rot, x[..., r:]], axis=-1)


def _project(x, w_in, pos):
    B, T, _ = x.shape
    z = jnp.einsum('btd,dn->btn', x, w_in)
    qa, ka, va, qi, ki, wi, qb, kb, vb, ga, gb = jnp.split(z, SPLIT_POINTS, axis=-1)
    qa = _rope(qa.reshape(B, T, H_A, HEAD_DIM), pos)
    ka = _rope(ka.reshape(B, T, H_A, HEAD_DIM), pos)
    va = va.reshape(B, T, H_A, HEAD_DIM)
    qi = _rope(qi.reshape(B, T, H_IDX, D_IDX), pos)
    ki = _rope(ki[:, :, None, :], pos)[:, :, 0, :]
    qb = qb.reshape(B, T, H_B, HEAD_DIM)
    kb = kb.reshape(B, T, H_B, HEAD_DIM)
    vb = vb.reshape(B, T, H_B, HEAD_DIM)
    return qa, ka, va, qi, ki, wi, qb, kb, vb, ga, gb


def _dsa_attend(qa, qi, wi, ki_all, q_pos, k_pos, gather_kv):
    L = ki_all.shape[1]
    n_top = min(TOPK_MAX, L // 4)
    s = jax.nn.relu(jnp.einsum('bthc,blc->bthl', qi, ki_all).astype(jnp.float32))
    score = jnp.einsum('bth,bthl->btl', wi.astype(jnp.float32), s)
    causal = k_pos[None, :] <= q_pos[:, None]
    score = jnp.where(causal[None], score, -jnp.inf)
    _, idx = lax.top_k(score, n_top)
    k_sel, v_sel = gather_kv(idx)
    valid = k_pos[idx] <= q_pos[None, :, None]
    logits = jnp.einsum('bthd,btkhd->bhtk', qa, k_sel).astype(jnp.float32) / math.sqrt(HEAD_DIM)
    logits = jnp.where(valid[:, None], logits, -jnp.inf)
    p = jax.nn.softmax(logits, axis=-1)
    return jnp.einsum('bhtk,btkhd->bthd', p.astype(v_sel.dtype), v_sel)


def _stick_breaking(q, k, v, q_pos, k_pos):
    z = jnp.einsum('bthd,blhd->bhtl', q, k).astype(jnp.float32) / math.sqrt(HEAD_DIM)
    mask = (k_pos[None, :] < q_pos[:, None])[None, None]
    log_1m = jnp.where(mask, jax.nn.log_sigmoid(-z), 0.0)
    between = lax.cumsum(log_1m, axis=3, reverse=True) - log_1m
    a = jnp.where(mask, jnp.exp(jax.nn.log_sigmoid(z) + between), 0.0)
    return jnp.einsum('bhtl,blhd->bthd', a.astype(v.dtype), v)


def _to_blocks(a, blk):
    B, T = a.shape[:2]
    return jnp.moveaxis(a.reshape((B, T // blk, blk) + a.shape[2:]), 1, 0)


def _from_blocks(a):
    nb, B, blk = a.shape[:3]
    return jnp.moveaxis(a, 0, 1).reshape((B, nb * blk) + a.shape[3:])


_take_rows = jax.vmap(lambda a, i: a[i])


def _peer_block(xb, w_pq, sub_keys, u, v):
    n = xb.shape[0]
    half = PEER_DQ // 2
    q = (xb @ w_pq).reshape(n, PEER_HEADS, PEER_DQ)
    s1 = jnp.einsum('nhc,kc->nhk', q[..., :half], sub_keys[0]).astype(jnp.float32)
    s2 = jnp.einsum('nhc,kc->nhk', q[..., half:], sub_keys[1]).astype(jnp.float32)
    t1, i1 = lax.top_k(s1, PEER_TOPK)
    t2, i2 = lax.top_k(s2, PEER_TOPK)
    cand = (t1[..., :, None] + t2[..., None, :]).reshape(n, PEER_HEADS, PEER_TOPK * PEER_TOPK)
    cand_idx = (i1[..., :, None] * PEER_NKEYS + i2[..., None, :]).reshape(n, PEER_HEADS, PEER_TOPK * PEER_TOPK)
    top, sel = lax.top_k(cand, PEER_TOPK)
    e = jnp.take_along_axis(cand_idx, sel, axis=-1)
    g = jax.nn.softmax(top, axis=-1)
    act = jax.nn.gelu(jnp.einsum('nd,nhkd->nhk', xb, u[e]))
    return jnp.einsum('nhk,nhkd->nd', (g * act).astype(xb.dtype), v[e])


def _peer(x2d, w_pq, sub_keys, u, v):
    n = x2d.shape[0]
    blk = min(PEER_BLOCK, n)
    pad = (-n) % blk
    xp = jnp.pad(x2d, ((0, pad), (0, 0))).reshape(-1, blk, x2d.shape[1])
    out = lax.map(lambda xb: _peer_block(xb, w_pq, sub_keys, u, v), xp)
    return out.reshape(-1, x2d.shape[1])[:n]


def _finish_layer(x, p, out_a, out_b, ga, gb, w_branch, w_out, ln1_g, ln1_b,
                  w_pq, sub_keys, u, v, ln2_g, ln2_b, w_ple_gate, w_ple_proj):
    B, T, _ = x.shape
    oa = jnp.einsum('btc,cd->btd', out_a.reshape(B, T, W_A), w_branch[:W_A])
    ob = jnp.einsum('btc,cd->btd', out_b.reshape(B, T, W_B), w_branch[W_A:])
    mix = jax.nn.sigmoid(ga) * oa + jax.nn.sigmoid(gb) * ob
    h = _layer_norm(DN_ALPHA * x + mix @ w_out, ln1_g, ln1_b)
    f = _peer(h.reshape(B * T, D_MODEL), w_pq, sub_keys, u, v).reshape(B, T, D_MODEL)
    e = jax.nn.sigmoid(h @ w_ple_gate) * (p @ w_ple_proj)
    return _layer_norm(DN_ALPHA * h + f + e, ln2_g, ln2_b)


def setup_inputs(seed: int = 0) -> dict:
    key = jax.random.key(seed)
    ks = jax.random.split(key, 24)
    f32 = jnp.float32
    n_pages = PAST_LEN // PAGE_SIZE
    n_used = DEC_BATCH * n_pages
    n_pool = n_used + (n_used + 3) // 4

    def nrm(k, shape, scale):
        return jax.random.normal(k, shape, f32) * scale

    page_table = jax.random.permutation(ks[0], n_pool)[:n_used].reshape(DEC_BATCH, n_pages).astype(jnp.int32)
    kv_shape = (DEPTH, n_pool, PAGE_SIZE, H_A, HEAD_DIM)
    return {
        "x_prompt": nrm(ks[1], (BATCH, SEQ, D_MODEL), 1.0),
        "x_sample": nrm(ks[2], (DEC_BATCH, DEC_SEQ, D_MODEL), 1.0),
        "p_prompt": nrm(ks[3], (DEPTH, BATCH, SEQ, PLE_DIM), 1.0),
        "p_sample": nrm(ks[4], (DEPTH, DEC_BATCH, DEC_SEQ, PLE_DIM), 1.0),
        "cache_a_k": nrm(ks[5], kv_shape, 1.0),
        "cache_a_v": nrm(ks[6], kv_shape, 1.0),
        "cache_idx_k": nrm(ks[7], (DEPTH, n_pool, PAGE_SIZE, D_IDX), 1.0),
        "cache_b_k": nrm(ks[8], (DEPTH, n_pool, PAGE_SIZE, H_B, HEAD_DIM), 1.0),
        "cache_b_v": nrm(ks[9], (DEPTH, n_pool, PAGE_SIZE, H_B, HEAD_DIM), 1.0),
        "page_table": page_table,
        "w_in": nrm(ks[10], (DEPTH, D_MODEL, N_IN), D_MODEL ** -0.5),
        "w_branch": nrm(ks[11], (DEPTH, W_A + W_B, D_MODEL), (W_A) ** -0.5),
        "w_out": nrm(ks[12], (DEPTH, D_MODEL, D_MODEL), DN_BETA * D_MODEL ** -0.5),
        "ln1_g": 1.0 + nrm(ks[13], (DEPTH, D_MODEL), 0.01),
        "ln1_b": nrm(ks[14], (DEPTH, D_MODEL), 0.01),
        "w_pq": nrm(ks[15], (DEPTH, D_MODEL, PEER_HEADS * PEER_DQ), D_MODEL ** -0.5),
        "peer_sub_keys": nrm(ks[16], (DEPTH, 2, PEER_NKEYS, PEER_DQ // 2), (PEER_DQ // 2) ** -0.5),
        "peer_u": nrm(ks[17], (DEPTH, PEER_EXPERTS, D_MODEL), D_MODEL ** -0.5),
        "peer_v": nrm(ks[18], (DEPTH, PEER_EXPERTS, D_MODEL), DN_BETA),
        "ln2_g": 1.0 + nrm(ks[19], (DEPTH, D_MODEL), 0.01),
        "ln2_b": nrm(ks[20], (DEPTH, D_MODEL), 0.01),
        "w_ple_gate": nrm(ks[21], (DEPTH, D_MODEL, D_MODEL), D_MODEL ** -0.5),
        "w_ple_proj": nrm(ks[22], (DEPTH, PLE_DIM, D_MODEL), PLE_DIM ** -0.5),
    }


def reference(x_prompt, x_sample, p_prompt, p_sample, cache_a_k, cache_a_v, cache_idx_k,
              cache_b_k, cache_b_v, page_table, w_in, w_branch, w_out, ln1_g, ln1_b,
              w_pq, peer_sub_keys, peer_u, peer_v, ln2_g, ln2_b, w_ple_gate, w_ple_proj):
    page = cache_a_k.shape[2]
    n_past = page_table.shape[1] * page
    dec_b, dec_t = x_sample.shape[0], x_sample.shape[1]
    seq = x_prompt.shape[1]
    pos_p = jnp.arange(seq, dtype=jnp.int32)
    pos_s = n_past + jnp.arange(dec_t, dtype=jnp.int32)
    pos_all_s = jnp.arange(n_past + dec_t, dtype=jnp.int32)
    blk = min(Q_BLOCK, seq)
    pos_blocks = pos_p.reshape(-1, blk)

    def paged_all(pool):
        g = pool[page_table]
        return g.reshape((dec_b, n_past) + pool.shape[2:])

    def gather_paged(pool, new_rows, idx):
        pidx = jnp.minimum(idx, n_past - 1)
        phys = jnp.take_along_axis(page_table, (pidx // page).reshape(dec_b, -1), axis=1).reshape(idx.shape)
        past = pool[phys, pidx % page]
        new = _take_rows(new_rows, jnp.clip(idx - n_past, 0, dec_t - 1))
        return jnp.where((idx >= n_past)[..., None, None], new, past)

    xp, xs = x_prompt, x_sample
    pa_k, pa_v, pi_k, pb_k, pb_v = [], [], [], [], []
    sa_k, sa_v, si_k, sb_k, sb_v = [], [], [], [], []
    for l in range(DEPTH):
        lw = (w_branch[l], w_out[l], ln1_g[l], ln1_b[l], w_pq[l], peer_sub_keys[l],
              peer_u[l], peer_v[l], ln2_g[l], ln2_b[l], w_ple_gate[l], w_ple_proj[l])
        qa, ka, va, qi, ki, wi, qb, kb, vb, ga, gb = _project(xp, w_in[l], pos_p)
        gather_p = lambda idx, ka=ka, va=va: (_take_rows(ka, idx), _take_rows(va, idx))
        out_a = _from_blocks(lax.map(
            lambda a, ki=ki, gather_p=gather_p: _dsa_attend(a[0], a[1], a[2], ki, a[3], pos_p, gather_p),
            (_to_blocks(qa, blk), _to_blocks(qi, blk), _to_blocks(wi, blk), pos_blocks)))
        out_b = _from_blocks(lax.map(
            lambda a, kb=kb, vb=vb: _stick_breaking(a[0], kb, vb, a[1], pos_p),
            (_to_blocks(qb, blk), pos_blocks)))
        pa_k.append(ka); pa_v.append(va); pi_k.append(ki); pb_k.append(kb); pb_v.append(vb)
        xp = _finish_layer(xp, p_prompt[l], out_a, out_b, ga, gb, *lw)

        qa, ka, va, qi, ki, wi, qb, kb, vb, ga, gb = _project(xs, w_in[l], pos_s)
        ki_all = jnp.concatenate([paged_all(cache_idx_k[l]), ki], axis=1)
        gather_s = lambda idx, l=l, ka=ka, va=va: (gather_paged(cache_a_k[l], ka, idx),
                                                 gather_paged(cache_a_v[l], va, idx))
        out_a = _dsa_attend(qa, qi, wi, ki_all, pos_s, pos_all_s, gather_s)
        kb_all = jnp.concatenate([paged_all(cache_b_k[l]), kb], axis=1)
        vb_all = jnp.concatenate([paged_all(cache_b_v[l]), vb], axis=1)
        out_b = _stick_breaking(qb, kb_all, vb_all, pos_s, pos_all_s)
        sa_k.append(ka); sa_v.append(va); si_k.append(ki); sb_k.append(kb); sb_v.append(vb)
        xs = _finish_layer(xs, p_sample[l], out_a, out_b, ga, gb, *lw)

    return (xp, xs,
            jnp.stack(pa_k), jnp.stack(pa_v), jnp.stack(pi_k), jnp.stack(pb_k), jnp.stack(pb_v),
            jnp.stack(sa_k), jnp.stack(sa_v), jnp.stack(si_k), jnp.stack(sb_k), jnp.stack(sb_v))
```

```python
import functools
import math

import numpy as np
import jax
import jax.numpy as jnp
from jax import lax
from jax.experimental import pallas as pl
from jax.experimental.pallas import tpu as pltpu

F32 = jnp.float32
BF16 = jnp.bfloat16
I32 = jnp.int32

D_MODEL = 1024
HEAD_DIM = 64
N_HEADS = 8
W_ATT = N_HEADS * HEAD_DIM
ROPE_THETA = 500000.0
ROT_64 = 16
ROT_32 = 8
H_IDX = 8
D_IDX = 32
TOPK_MAX = 256
PEER_HEADS = 8
PEER_NKEYS = 128
PEER_DQ = 128
PEER_TOPK = 16
PLE_DIM = 256
LN_EPS = 1e-5
IN_SIZES = (W_ATT, W_ATT, W_ATT, H_IDX * D_IDX, D_IDX, H_IDX, W_ATT, W_ATT, W_ATT, D_MODEL, D_MODEL)

LANES = 128
VMEM_LIMIT_BYTES = 56 * 1024 * 1024
INT_MIN = -(2 ** 31)
NEG_BIG = -1e30
M_INIT = -1e20
SB_DEAD = -104.0

C_QA, C_KA, C_VA, C_QI, C_KW = 0, 512, 1024, 1536, 1792
C_QB, C_KB, C_VB, C_GA, C_GB, N_COLS = 1920, 2432, 2944, 3456, 4480, 5504


def _cparams(sem):
    return pltpu.CompilerParams(dimension_semantics=sem, vmem_limit_bytes=VMEM_LIMIT_BYTES)


def _resident(shape):
    zeros = (0,) * len(shape)
    return pl.BlockSpec(shape, lambda *_: zeros, pipeline_mode=pl.Buffered(1))


def _rope_tables(pos):
    posf = pos.astype(F32)

    def cos_sin(r):
        half = r // 2
        inv = ROPE_THETA ** (-(jnp.arange(half, dtype=F32) * 2.0 / r))
        ang = posf[:, None] * inv[None, :]
        return jnp.cos(ang), jnp.sin(ang)

    def lanes(cos, sin, head_dim, half, n_rot_lanes):
        lane = np.arange(LANES)
        m = lane % head_dim
        first = (m < half) & (lane < n_rot_lanes)
        second = (m >= half) & (m < 2 * half) & (lane < n_rot_lanes)
        idx = np.where(first, m, np.where(second, m - half, 0))
        c = jnp.where(first | second, cos[:, idx], 1.0)
        sa = jnp.where(first, -sin[:, idx], 0.0)
        sb = jnp.where(second, sin[:, idx], 0.0)
        return [c, sa, sb]

    c64, s64 = cos_sin(ROT_64)
    c32, s32 = cos_sin(ROT_32)
    tabs = (lanes(c64, s64, HEAD_DIM, ROT_64 // 2, LANES) + lanes(c32, s32, D_IDX, ROT_32 // 2, LANES)
            + lanes(c32, s32, D_IDX, ROT_32 // 2, D_IDX))
    return jnp.stack(tabs).astype(F32)


def _proj_kernel(x_ref, w_ref, tab_ref, qa_ref, ka_ref, ka16_ref, va_ref, va16_ref, qi_ref, ki_ref, wi_ref,
                 qb_ref, kb_ref, kb16_ref, vb_ref, vb16_ref, sga_ref, sgb_ref):
    xb = x_ref[...].astype(BF16)

    def mm(c0, n):
        return jnp.dot(xb, w_ref[:, c0:c0 + n], preferred_element_type=F32)

    def rope(zc, k, sh):
        return (zc * tab_ref[3 * k] + pltpu.roll(zc, LANES - sh, 1) * tab_ref[3 * k + 1]
                + pltpu.roll(zc, sh, 1) * tab_ref[3 * k + 2])

    z = mm(C_QA, W_ATT)
    for c in range(W_ATT // LANES):
        sl = slice(c * LANES, (c + 1) * LANES)
        qa_ref[:, sl] = rope(z[:, sl], 0, ROT_64 // 2).astype(BF16)
    z = mm(C_KA, W_ATT)
    for c in range(W_ATT // LANES):
        sl = slice(c * LANES, (c + 1) * LANES)
        r = rope(z[:, sl], 0, ROT_64 // 2)
        ka_ref[:, sl] = r
        ka16_ref[:, sl] = r.astype(BF16)
    z = mm(C_VA, W_ATT)
    va_ref[...] = z
    va16_ref[...] = z.astype(BF16)
    z = mm(C_QI, H_IDX * D_IDX)
    for c in range(H_IDX * D_IDX // LANES):
        sl = slice(c * LANES, (c + 1) * LANES)
        qi_ref[:, sl] = rope(z[:, sl], 1, ROT_32 // 2).astype(BF16)
    r = rope(mm(C_KW, LANES), 2, ROT_32 // 2)
    ki_ref[...] = r[:, :D_IDX]
    wi_ref[...] = r[:, D_IDX:D_IDX + H_IDX]
    qb_ref[...] = mm(C_QB, W_ATT).astype(BF16)
    z = mm(C_KB, W_ATT)
    kb_ref[...] = z
    kb16_ref[...] = z.astype(BF16)
    z = mm(C_VB, W_ATT)
    vb_ref[...] = z
    vb16_ref[...] = z.astype(BF16)
    sga_ref[...] = jax.nn.sigmoid(mm(C_GA, D_MODEL)).astype(BF16)
    sgb_ref[...] = jax.nn.sigmoid(mm(C_GB, D_MODEL)).astype(BF16)


def _permute_w_in(w_in):
    qa, ka, va, qi, ki, wi, qb, kb, vb, ga, gb = jnp.split(w_in, np.cumsum(IN_SIZES)[:-1].tolist(), axis=1)
    scale = 1.0 / math.sqrt(HEAD_DIM)
    pad = jnp.zeros((D_MODEL, LANES - D_IDX - H_IDX), w_in.dtype)
    w = jnp.concatenate([qa * scale, ka, va, qi, ki, wi, pad, qb * scale, kb, vb, ga, gb], axis=1)
    assert w.shape[1] == N_COLS
    return w.astype(BF16)


def _project(x2d, w_perm, pos, tm):
    t = x2d.shape[0]
    tabs = _rope_tables(pos)
    row = lambda n, dt: jax.ShapeDtypeStruct((t, n), dt)
    rspec = lambda n: pl.BlockSpec((tm, n), lambda i: (i, 0))
    outs = [(W_ATT, BF16), (W_ATT, F32), (W_ATT, BF16), (W_ATT, F32), (W_ATT, BF16), (H_IDX * D_IDX, BF16),
            (D_IDX, F32), (H_IDX, F32), (W_ATT, BF16), (W_ATT, F32), (W_ATT, BF16), (W_ATT, F32), (W_ATT, BF16),
            (D_MODEL, BF16), (D_MODEL, BF16)]
    return pl.pallas_call(
        _proj_kernel,
        grid=(t // tm,),
        in_specs=[rspec(D_MODEL), _resident((D_MODEL, N_COLS)), pl.BlockSpec((9, tm, LANES), lambda i: (0, i, 0))],
        out_specs=[rspec(n) for n, _ in outs],
        out_shape=[row(n, dt) for n, dt in outs],
        compiler_params=_cparams(("parallel",)),
        name="proj_rope",
    )(x2d, w_perm, tabs)


def _sortable_key(score):
    b = lax.bitcast_convert_type(score, I32)
    return jnp.where(b < 0, b ^ jnp.int32(0x7FFFFFFF), b)


def _kth_largest_key(count_ge, k, rows_shape):
    v = jnp.where(count_ge(jnp.zeros(rows_shape, I32)) >= k, jnp.int32(0), jnp.int32(INT_MIN))

    def bit_body(t, v):
        cand = v + lax.shift_left(jnp.int32(1), jnp.int32(30) - t)
        return jnp.where(count_ge(cand) >= k, cand, v)

    return lax.fori_loop(0, 31, bit_body, v)


def _tie_cut(count_eq_below, need, n_bits, rows_shape):
    def bit_body(t, x):
        cand = x + lax.shift_left(jnp.int32(1), jnp.int32(n_bits - 1) - t)
        return jnp.where(count_eq_below(cand) < need, cand, x)

    return lax.fori_loop(0, n_bits, bit_body, jnp.zeros(rows_shape, I32))


def _attend_block(sel_add, q_ref, k_blk, v_blk, m_ref, l_ref, acc_ref):
    for h in range(N_HEADS):
        hs = slice(h * HEAD_DIM, (h + 1) * HEAD_DIM)
        s = lax.dot_general(q_ref[:, hs], k_blk[:, hs], (((1,), (1,)), ((), ())), preferred_element_type=F32)
        s = s + sel_add
        m_prev = m_ref[h]
        m_new = jnp.maximum(m_prev, jnp.max(s, axis=1, keepdims=True))
        p = jnp.exp(s - m_new)
        alpha = jnp.exp(m_prev - m_new)
        l_ref[h] = alpha * l_ref[h] + jnp.sum(p, axis=1, keepdims=True)
        acc_ref[h] = alpha * acc_ref[h] + jnp.dot(p.astype(BF16), v_blk[:, hs], preferred_element_type=F32)
        m_ref[h] = m_new


def _attend_init(m_ref, l_ref, acc_ref):
    m_ref[...] = jnp.full(m_ref.shape, M_INIT, F32)
    l_ref[...] = jnp.zeros(l_ref.shape, F32)
    acc_ref[...] = jnp.zeros(acc_ref.shape, F32)


def _attend_finish(o_ref, l_ref, acc_ref):
    for h in range(N_HEADS):
        o_ref[:, h * HEAD_DIM:(h + 1) * HEAD_DIM] = (acc_ref[h] / l_ref[h]).astype(o_ref.dtype)


def _tri_pairs(t, tq, tk, reverse):
    qi, kj, first, last = [], [], [], []
    for i in range(t // tq):
        js = list(range(((i + 1) * tq - 1) // tk + 1))
        if reverse:
            js = js[::-1]
        for n, j in enumerate(js):
            qi.append(i), kj.append(j), first.append(int(n == 0)), last.append(int(n == len(js) - 1))
    return tuple(jnp.asarray(np.asarray(a, np.int32)) for a in (qi, kj, first, last))


def _dsa_prompt_kernel(qi_s, kj_s, first_s, last_s, qis_ref, wi_ref, ki_ref, qa_ref, ka_ref, va_ref, o_ref,
                       sc_ref, vthr_ref, jthr_ref, m_ref, l_ref, acc_ref, *, tq, tk, n_top, n_idx_bits):
    p = pl.program_id(0)
    i = qi_s[p]
    j = kj_s[p]
    row = i * tq + lax.broadcasted_iota(I32, (tq, tk), 0)
    lane = lax.broadcasted_iota(I32, (tq, tk), 1)

    @pl.when(first_s[p] == 1)
    def _():
        _attend_init(m_ref, l_ref, acc_ref)
        n_chunks = ((i + 1) * tq + tk - 1) // tk
        w = wi_ref[...]

        def score_chunk(c, carry):
            k_blk = ki_ref[pl.ds(pl.multiple_of(c * tk, tk), tk), :]
            score = jnp.zeros((tq, tk), F32)
            for h in range(H_IDX):
                s = lax.dot_general(qis_ref[h], k_blk, (((1,), (1,)), ((), ())), preferred_element_type=F32)
                score = score + w[:, h:h + 1] * jnp.maximum(s, 0.0)
            score = jnp.where(c * tk + lane <= row, score, -jnp.inf)
            sc_ref[c] = _sortable_key(score)
            return carry

        lax.fori_loop(0, n_chunks, score_chunk, 0)

        lane1 = lax.broadcasted_iota(I32, (tq, LANES), 1)

        def lane_count(ones_fn):
            def body(c, acc):
                for t in range(tk // LANES):
                    acc = acc + ones_fn(sc_ref[c, :, t * LANES:(t + 1) * LANES], c * tk + t * LANES)
                return acc
            acc = lax.fori_loop(0, n_chunks, body, jnp.zeros((tq, LANES), F32))
            return jnp.sum(acc, axis=1, keepdims=True).astype(I32)

        def count_ge(cand):
            cand_b = jnp.broadcast_to(cand, (tq, LANES))
            return lane_count(lambda slab, k0: jnp.where(slab >= cand_b, 1.0, 0.0))

        v = _kth_largest_key(count_ge, n_top, (tq, 1))
        c_gt = count_ge(v + 1)
        need = n_top - c_gt
        excess = count_ge(v) - c_gt - need
        v_b = jnp.broadcast_to(v, (tq, LANES))
        vthr_ref[...] = v_b
        jthr_ref[...] = jnp.full((tq, LANES), 2 ** 30, I32)

        @pl.when(jnp.max(excess.astype(F32)) > 0.0)
        def _():
            def count_eq_below(x):
                x_b = jnp.broadcast_to(x, (tq, LANES))
                return lane_count(lambda slab, k0: jnp.where(slab == v_b, jnp.where(k0 + lane1 < x_b, 1.0, 0.0), 0.0))
            jthr_ref[...] = jnp.broadcast_to(_tie_cut(count_eq_below, need, n_idx_bits, (tq, 1)), (tq, LANES))

    key = sc_ref[j]
    reps = tk // LANES
    vt = jnp.concatenate([vthr_ref[...]] * reps, axis=1)
    jt = jnp.concatenate([jthr_ref[...]] * reps, axis=1)
    col = j * tk + lane
    sel = jnp.where(key > vt, 0.0, jnp.where(key == vt, jnp.where(col <= jt, 0.0, NEG_BIG), NEG_BIG))
    sel = jnp.where(col <= row, sel, NEG_BIG)
    _attend_block(sel, qa_ref, ka_ref[...], va_ref[...], m_ref, l_ref, acc_ref)

    @pl.when(last_s[p] == 1)
    def _():
        _attend_finish(o_ref, l_ref, acc_ref)


def _dsa_prompt(qis, wi, ki16, qa, ka16, va16, *, tq, tk):
    t = qa.shape[0]
    n_top = min(TOPK_MAX, t // 4)
    pairs = _tri_pairs(t, tq, tk, reverse=False)
    kern = functools.partial(_dsa_prompt_kernel, tq=tq, tk=tk, n_top=n_top,
                             n_idx_bits=max(1, int(math.ceil(math.log2(t)))))
    grid_spec = pltpu.PrefetchScalarGridSpec(
        num_scalar_prefetch=4,
        grid=(int(pairs[0].shape[0]),),
        in_specs=[
            pl.BlockSpec((H_IDX, tq, D_IDX), lambda p, qi, kj, fi, la: (0, qi[p], 0)),
            pl.BlockSpec((tq, H_IDX), lambda p, qi, kj, fi, la: (qi[p], 0)),
            pl.BlockSpec((t, D_IDX), lambda p, qi, kj, fi, la: (0, 0)),
            pl.BlockSpec((tq, W_ATT), lambda p, qi, kj, fi, la: (qi[p], 0)),
            pl.BlockSpec((tk, W_ATT), lambda p, qi, kj, fi, la: (kj[p], 0)),
            pl.BlockSpec((tk, W_ATT), lambda p, qi, kj, fi, la: (kj[p], 0)),
        ],
        out_specs=pl.BlockSpec((tq, W_ATT), lambda p, qi, kj, fi, la: (qi[p], 0)),
        scratch_shapes=[
            pltpu.VMEM((t // tk, tq, tk), I32),
            pltpu.VMEM((tq, LANES), I32),
            pltpu.VMEM((tq, LANES), I32),
            pltpu.VMEM((N_HEADS, tq, 1), F32),
            pltpu.VMEM((N_HEADS, tq, 1), F32),
            pltpu.VMEM((N_HEADS, tq, HEAD_DIM), F32),
        ],
    )
    return pl.pallas_call(
        kern, grid_spec=grid_spec, out_shape=jax.ShapeDtypeStruct((t, W_ATT), BF16),
        compiler_params=_cparams(("arbitrary",)), name="dsa_prompt",
    )(*pairs, qis, wi, ki16, qa, ka16, va16)


def _sb_block(q_ref, k_blk, v_blk, mask, tri_ref, carry_ref, acc_ref):
    worst = None
    for h in range(N_HEADS):
        hs = slice(h * HEAD_DIM, (h + 1) * HEAD_DIM)
        z = lax.dot_general(q_ref[:, hs], k_blk[:, hs], (((1,), (1,)), ((), ())), preferred_element_type=F32)
        lsm = -(jnp.maximum(z, 0.0) + jnp.log1p(jnp.exp(-jnp.abs(z))))
        lg = jnp.where(mask, lsm, 0.0)
        hi = lg.astype(BF16)
        lo = (lg - hi.astype(F32)).astype(BF16)
        tri = tri_ref[...]
        later = jnp.dot(hi, tri, preferred_element_type=F32) + jnp.dot(lo, tri, preferred_element_type=F32)
        carry = carry_ref[h]
        a = jnp.where(mask, jnp.exp(lsm + z + later + carry), 0.0)
        acc_ref[h] = acc_ref[h] + jnp.dot(a.astype(BF16), v_blk[:, hs], preferred_element_type=F32)
        carry = carry + later[:, 0:1] + lg[:, 0:1]
        carry_ref[h] = carry
        worst = carry if worst is None else jnp.maximum(worst, carry)
    return jnp.max(worst)


def _sb_finish(o_ref, acc_ref):
    for h in range(N_HEADS):
        o_ref[:, h * HEAD_DIM:(h + 1) * HEAD_DIM] = acc_ref[h].astype(o_ref.dtype)


def _later_matrix(tk):
    r = np.arange(tk)
    return jnp.asarray((r[:, None] > r[None, :]).astype(np.float32)).astype(BF16)


def _sb_prompt_kernel(qi_s, kj_s, first_s, last_s, q_ref, k_ref, v_ref, tri_ref, o_ref,
                      carry_ref, acc_ref, done_ref, *, tq, tk):
    p = pl.program_id(0)
    i = qi_s[p]
    j = kj_s[p]

    @pl.when(first_s[p] == 1)
    def _():
        carry_ref[...] = jnp.zeros(carry_ref.shape, F32)
        acc_ref[...] = jnp.zeros(acc_ref.shape, F32)
        done_ref[0] = 0

    @pl.when(done_ref[0] == 0)
    def _():
        row = i * tq + lax.broadcasted_iota(I32, (tq, tk), 0)
        col = j * tk + lax.broadcasted_iota(I32, (tq, tk), 1)
        worst = _sb_block(q_ref, k_ref[...], v_ref[...], col < row, tri_ref, carry_ref, acc_ref)
        done_ref[0] = (worst < SB_DEAD).astype(I32)

    @pl.when(last_s[p] == 1)
    def _():
        _sb_finish(o_ref, acc_ref)


def _sb_prompt(qb, kb16, vb16, *, tq, tk):
    t = qb.shape[0]
    pairs = _tri_pairs(t, tq, tk, reverse=True)
    kern = functools.partial(_sb_prompt_kernel, tq=tq, tk=tk)
    grid_spec = pltpu.PrefetchScalarGridSpec(
        num_scalar_prefetch=4,
        grid=(int(pairs[0].shape[0]),),
        in_specs=[
            pl.BlockSpec((tq, W_ATT), lambda p, qi, kj, fi, la: (qi[p], 0)),
            pl.BlockSpec((tk, W_ATT), lambda p, qi, kj, fi, la: (kj[p], 0)),
            pl.BlockSpec((tk, W_ATT), lambda p, qi, kj, fi, la: (kj[p], 0)),
            pl.BlockSpec((tk, tk), lambda p, qi, kj, fi, la: (0, 0)),
        ],
        out_specs=pl.BlockSpec((tq, W_ATT), lambda p, qi, kj, fi, la: (qi[p], 0)),
        scratch_shapes=[
            pltpu.VMEM((N_HEADS, tq, 1), F32),
            pltpu.VMEM((N_HEADS, tq, HEAD_DIM), F32),
            pltpu.SMEM((1,), I32),
        ],
    )
    return pl.pallas_call(
        kern, grid_spec=grid_spec, out_shape=jax.ShapeDtypeStruct((t, W_ATT), BF16),
        compiler_params=_cparams(("arbitrary",)), name="sb_prompt",
    )(*pairs, qb, kb16, vb16, _later_matrix(tk))


def _dsa_sample_kernel(pt_s, qis_ref, wi_ref, kip_ref, kin_ref, qa_ref, kap_ref, vap_ref, kan_ref, van_ref, o_ref,
                       sc_ref, vthr_ref, jthr_ref, m_ref, l_ref, acc_ref, *, rows, page, n_pages, n_new, n_top,
                       n_idx_bits):
    s = pl.program_id(1)
    n_blocks = n_pages + 1
    r_new = jnp.minimum(lax.broadcasted_iota(I32, (rows, page), 0), n_new - 1)
    lane = lax.broadcasted_iota(I32, (rows, page), 1)

    def score_block(k_blk, blk_idx, valid):
        w = wi_ref[0]
        score = jnp.zeros((rows, page), F32)
        for h in range(H_IDX):
            sh = lax.dot_general(qis_ref[0, h], k_blk, (((1,), (1,)), ((), ())), preferred_element_type=F32)
            score = score + w[:, h:h + 1] * jnp.maximum(sh, 0.0)
        key = _sortable_key(score)
        if valid is not None:
            causal, exists = valid
            key = jnp.where(exists, jnp.where(causal, key, _sortable_key(jnp.full((rows, page), -jnp.inf, F32))),
                            jnp.int32(INT_MIN))
        sc_ref[blk_idx] = key

    @pl.when(s < n_pages)
    def _():
        score_block(kip_ref[0].astype(BF16), s, None)

    @pl.when(s == n_pages)
    def _():
        _attend_init(m_ref, l_ref, acc_ref)
        score_block(kin_ref[0], n_pages, (lane <= r_new, lane < n_new))
        keys = sc_ref[...]
        idx = (lax.broadcasted_iota(I32, (n_blocks, rows, page), 0) * page
               + lax.broadcasted_iota(I32, (n_blocks, rows, page), 2))

        def count(hit):
            return jnp.sum(jnp.sum(jnp.where(hit, 1.0, 0.0), axis=0), axis=1, keepdims=True).astype(I32)

        def count_ge(cand):
            return count(keys >= cand[None])

        v = _kth_largest_key(count_ge, n_top, (rows, 1))
        c_gt = count_ge(v + 1)
        need = n_top - c_gt
        excess = count_ge(v) - c_gt - need
        vthr_ref[...] = jnp.broadcast_to(v, (rows, LANES))
        jthr_ref[...] = jnp.full((rows, LANES), 2 ** 30, I32)

        @pl.when(jnp.max(excess.astype(F32)) > 0.0)
        def _():
            def count_eq_below(x):
                return count(jnp.logical_and(keys == v[None], idx < x[None]))
            jthr_ref[...] = jnp.broadcast_to(_tie_cut(count_eq_below, need, n_idx_bits, (rows, 1)), (rows, LANES))

    def attend(blk_idx, k_blk, v_blk, valid):
        key = sc_ref[blk_idx]
        vt = vthr_ref[...]
        jt = jthr_ref[...]
        col = blk_idx * page + lane
        sel = jnp.where(key > vt, 0.0, jnp.where(key == vt, jnp.where(col <= jt, 0.0, NEG_BIG), NEG_BIG))
        if valid is not None:
            sel = jnp.where(valid, sel, NEG_BIG)
        _attend_block(sel, qa_ref.at[0], k_blk, v_blk, m_ref, l_ref, acc_ref)

    @pl.when(jnp.logical_and(s > n_pages, s <= 2 * n_pages))
    def _():
        attend(s - n_pages - 1, kap_ref[0].astype(BF16), vap_ref[0].astype(BF16), None)

    @pl.when(s == 2 * n_pages + 1)
    def _():
        attend(n_pages, kan_ref[0], van_ref[0], jnp.logical_and(lane <= r_new, lane < n_new))
        _attend_finish(o_ref.at[0], l_ref, acc_ref)


def _dsa_sample(page_table, qis, wi, cache_ik, ki_new, qa, cache_ak, cache_av, ka_new, va_new, *, n_new):
    b, n_pages = page_table.shape
    rows = qa.shape[1]
    page = cache_ak.shape[1]
    assert page == LANES
    n_keys = n_pages * page + n_new
    n_top = min(TOPK_MAX, n_keys // 4)
    kern = functools.partial(_dsa_sample_kernel, rows=rows, page=page, n_pages=n_pages, n_new=n_new, n_top=n_top,
                             n_idx_bits=max(1, int(math.ceil(math.log2((n_pages + 1) * page)))))
    last = n_pages - 1
    grid_spec = pltpu.PrefetchScalarGridSpec(
        num_scalar_prefetch=1,
        grid=(b, 2 * n_pages + 2),
        in_specs=[
            pl.BlockSpec((1, H_IDX, rows, D_IDX), lambda i, s, pt: (i, 0, 0, 0)),
            pl.BlockSpec((1, rows, H_IDX), lambda i, s, pt: (i, 0, 0)),
            pl.BlockSpec((1, page, D_IDX), lambda i, s, pt: (pt[i, jnp.minimum(s, last)], 0, 0)),
            pl.BlockSpec((1, page, D_IDX), lambda i, s, pt: (i, 0, 0)),
            pl.BlockSpec((1, rows, W_ATT), lambda i, s, pt: (i, 0, 0)),
            pl.BlockSpec((1, page, W_ATT), lambda i, s, pt: (pt[i, jnp.clip(s - n_pages - 1, 0, last)], 0, 0)),
            pl.BlockSpec((1, page, W_ATT), lambda i, s, pt: (pt[i, jnp.clip(s - n_pages - 1, 0, last)], 0, 0)),
            pl.BlockSpec((1, page, W_ATT), lambda i, s, pt: (i, 0, 0)),
            pl.BlockSpec((1, page, W_ATT), lambda i, s, pt: (i, 0, 0)),
        ],
        out_specs=pl.BlockSpec((1, rows, W_ATT), lambda i, s, pt: (i, 0, 0)),
        scratch_shapes=[
            pltpu.VMEM((n_pages + 1, rows, page), I32),
            pltpu.VMEM((rows, LANES), I32),
            pltpu.VMEM((rows, LANES), I32),
            pltpu.VMEM((N_HEADS, rows, 1), F32),
            pltpu.VMEM((N_HEADS, rows, 1), F32),
            pltpu.VMEM((N_HEADS, rows, HEAD_DIM), F32),
        ],
    )
    return pl.pallas_call(
        kern, grid_spec=grid_spec, out_shape=jax.ShapeDtypeStruct((b, rows, W_ATT), BF16),
        compiler_params=_cparams(("arbitrary", "arbitrary")), name="dsa_sample",
    )(page_table, qis, wi, cache_ik, ki_new, qa, cache_ak, cache_av, ka_new, va_new)


def _sb_sample_kernel(pt_s, q_ref, kn_ref, vn_ref, kp_ref, vp_ref, tri_ref, o_ref, carry_ref, acc_ref, done_ref,
                      *, rows, page, n_pages, n_new):
    s = pl.program_id(1)
    r_new = jnp.minimum(lax.broadcasted_iota(I32, (rows, page), 0), n_new - 1)
    lane = lax.broadcasted_iota(I32, (rows, page), 1)

    @pl.when(s == 0)
    def _():
        carry_ref[...] = jnp.zeros(carry_ref.shape, F32)
        acc_ref[...] = jnp.zeros(acc_ref.shape, F32)
        _sb_block(q_ref.at[0], kn_ref[0], vn_ref[0], lane < r_new, tri_ref, carry_ref, acc_ref)
        done_ref[0] = 0

    @pl.when(jnp.logical_and(s > 0, done_ref[0] == 0))
    def _():
        worst = _sb_block(q_ref.at[0], kp_ref[0].astype(BF16), vp_ref[0].astype(BF16), lane >= 0, tri_ref,
                          carry_ref, acc_ref)
        done_ref[0] = (worst < SB_DEAD).astype(I32)

    @pl.when(s == n_pages)
    def _():
        _sb_finish(o_ref.at[0], acc_ref)


def _sb_sample(page_table, qb, kb_new, vb_new, cache_bk, cache_bv, *, n_new):
    b, n_pages = page_table.shape
    rows = qb.shape[1]
    page = cache_bk.shape[1]
    kern = functools.partial(_sb_sample_kernel, rows=rows, page=page, n_pages=n_pages, n_new=n_new)
    pmap = lambda i, s, pt: (pt[i, n_pages - jnp.maximum(s, 1)], 0, 0)
    grid_spec = pltpu.PrefetchScalarGridSpec(
        num_scalar_prefetch=1,
        grid=(b, n_pages + 1),
        in_specs=[
            pl.BlockSpec((1, rows, W_ATT), lambda i, s, pt: (i, 0, 0)),
            pl.BlockSpec((1, page, W_ATT), lambda i, s, pt: (i, 0, 0)),
            pl.BlockSpec((1, page, W_ATT), lambda i, s, pt: (i, 0, 0)),
            pl.BlockSpec((1, page, W_ATT), pmap),
            pl.BlockSpec((1, page, W_ATT), pmap),
            pl.BlockSpec((page, page), lambda i, s, pt: (0, 0)),
        ],
        out_specs=pl.BlockSpec((1, rows, W_ATT), lambda i, s, pt: (i, 0, 0)),
        scratch_shapes=[
            pltpu.VMEM((N_HEADS, rows, 1), F32),
            pltpu.VMEM((N_HEADS, rows, HEAD_DIM), F32),
            pltpu.SMEM((1,), I32),
        ],
    )
    return pl.pallas_call(
        kern, grid_spec=grid_spec, out_shape=jax.ShapeDtypeStruct((b, rows, W_ATT), BF16),
        compiler_params=_cparams(("arbitrary", "arbitrary")), name="sb_sample",
    )(page_table, qb, kb_new, vb_new, cache_bk, cache_bv, _later_matrix(page))


def _layer_norm(y, g, b):
    mu = jnp.mean(y, axis=1, keepdims=True)
    d = y - mu
    var = jnp.mean(d * d, axis=1, keepdims=True)
    return d * lax.rsqrt(var + LN_EPS) * g + b


def _merge_kernel(oa_ref, ob_ref, sga_ref, sgb_ref, x_ref, wba_ref, wbb_ref, wo_ref, g_ref, b_ref, h_ref, h16_ref,
                  *, alpha):
    oa = jnp.dot(oa_ref[...], wba_ref[...], preferred_element_type=F32)
    ob = jnp.dot(ob_ref[...], wbb_ref[...], preferred_element_type=F32)
    mix = sga_ref[...].astype(F32) * oa + sgb_ref[...].astype(F32) * ob
    y = alpha * x_ref[...] + jnp.dot(mix.astype(BF16), wo_ref[...], preferred_element_type=F32)
    h = _layer_norm(y, g_ref[...], b_ref[...])
    h_ref[...] = h
    h16_ref[...] = h.astype(BF16)


def _merge(oa, ob, sga, sgb, x2d, wba, wbb, wo, g, b, *, alpha, tm):
    t = x2d.shape[0]
    rspec = lambda n: pl.BlockSpec((tm, n), lambda i: (i, 0))
    return pl.pallas_call(
        functools.partial(_merge_kernel, alpha=alpha),
        grid=(t // tm,),
        in_specs=[rspec(W_ATT), rspec(W_ATT), rspec(D_MODEL), rspec(D_MODEL), rspec(D_MODEL),
                  _resident((W_ATT, D_MODEL)), _resident((W_ATT, D_MODEL)), _resident((D_MODEL, D_MODEL)),
                  _resident((1, D_MODEL)), _resident((1, D_MODEL))],
        out_specs=[rspec(D_MODEL), rspec(D_MODEL)],
        out_shape=[jax.ShapeDtypeStruct((t, D_MODEL), F32), jax.ShapeDtypeStruct((t, D_MODEL), BF16)],
        compiler_params=_cparams(("parallel",)), name="merge_ln1",
    )(oa, ob, sga, sgb, x2d, wba, wbb, wo, g, b)


def _top_rows(x, n):
    out = []
    for _ in range(n):
        m = jnp.max(x, axis=0, keepdims=True)
        out.append(m)
        x = jnp.where(x == m, -jnp.inf, x)
    return out


def _peer_keys_kernel(ht_ref, wq_ref, wk_ref, s1_ref, s2_ref, e1_ref, e2_ref, thr_ref):
    qt = jnp.dot(wq_ref[...], ht_ref[...], preferred_element_type=F32)
    st = jnp.dot(wk_ref[...], qt.astype(BF16), preferred_element_type=F32)
    for h in range(PEER_HEADS):
        s1 = st[h * 2 * PEER_NKEYS:h * 2 * PEER_NKEYS + PEER_NKEYS]
        s2 = st[h * 2 * PEER_NKEYS + PEER_NKEYS:(h + 1) * 2 * PEER_NKEYS]
        t1 = _top_rows(s1, PEER_TOPK)
        t2 = jnp.concatenate(_top_rows(s2, PEER_TOPK), axis=0)
        cand = jnp.concatenate([t1[a] + t2 for a in range(PEER_TOPK)], axis=0)
        cmax = t1[0] + t2[0:1]
        cur = cand
        tot = jnp.zeros_like(cmax)
        thr = cmax
        for _ in range(PEER_TOPK):
            m = jnp.max(cur, axis=0, keepdims=True)
            hit = cur == m
            tot_new = tot + jnp.sum(jnp.where(hit, 1.0, 0.0), axis=0, keepdims=True)
            thr = jnp.where(jnp.logical_and(tot < PEER_TOPK, tot_new >= PEER_TOPK), m, thr)
            tot = tot_new
            cur = jnp.where(hit, -jnp.inf, cur)
        z = jnp.sum(jnp.where(cand >= thr, jnp.exp(cand - cmax), 0.0), axis=0, keepdims=True)
        s1_ref[h] = s1
        s2_ref[h] = s2
        e1_ref[h] = jnp.exp(s1 - t1[0]) / z
        e2_ref[h] = jnp.exp(s2 - t2[0:1])
        thr_ref[h:h + 1, :] = thr


def _peer_keys(ht16, wq_t, wk_t, *, tn):
    t = ht16.shape[1]
    big = jax.ShapeDtypeStruct((PEER_HEADS, PEER_NKEYS, t), F32)
    bspec = pl.BlockSpec((PEER_HEADS, PEER_NKEYS, tn), lambda i: (0, 0, i))
    return pl.pallas_call(
        _peer_keys_kernel,
        grid=(t // tn,),
        in_specs=[pl.BlockSpec((D_MODEL, tn), lambda i: (0, i)), _resident(wq_t.shape), _resident(wk_t.shape)],
        out_specs=[bspec, bspec, bspec, bspec, pl.BlockSpec((PEER_HEADS, tn), lambda i: (0, i))],
        out_shape=[big, big, big, big, jax.ShapeDtypeStruct((PEER_HEADS, t), F32)],
        compiler_params=_cparams(("parallel",)), name="peer_keys",
    )(ht16, wq_t, wk_t)


def _gelu_tanh(x):
    return 0.5 * x * (1.0 + jnp.tanh(math.sqrt(2.0 / math.pi) * (x + 0.044715 * (x * x * x))))


def _peer_mix_kernel(ht_ref, u_ref, vt_ref, s1_ref, s2_ref, e1_ref, e2_ref, thr_ref, ft_ref, g_ref, *, i_per_chunk):
    c = pl.program_id(1)

    @pl.when(c == 0)
    def _():
        ft_ref[...] = jnp.zeros(ft_ref.shape, F32)

    act = _gelu_tanh(jnp.dot(u_ref[...], ht_ref[...], preferred_element_type=F32))
    for ii in range(i_per_chunk):
        i = c * i_per_chunk + ii
        w = None
        for h in range(PEER_HEADS):
            s1_row = s1_ref[h, pl.ds(i, 1), :]
            e1_row = e1_ref[h, pl.ds(i, 1), :]
            gate = jnp.where(s1_row + s2_ref[h] >= thr_ref[h:h + 1, :], e1_row * e2_ref[h], 0.0)
            w = gate if w is None else w + gate
        rs = slice(ii * PEER_NKEYS, (ii + 1) * PEER_NKEYS)
        g_ref[rs, :] = (w * act[rs, :]).astype(BF16)
    ft_ref[...] += jnp.dot(vt_ref[...], g_ref[...], preferred_element_type=F32)


def _peer_mix(ht16, u16, vt16, s1, s2, e1, e2, thr, *, tn, i_per_chunk):
    t = ht16.shape[1]
    ce = i_per_chunk * PEER_NKEYS
    n_exp = u16.shape[0]
    bspec = pl.BlockSpec((PEER_HEADS, PEER_NKEYS, tn), lambda i, c: (0, 0, i))
    return pl.pallas_call(
        functools.partial(_peer_mix_kernel, i_per_chunk=i_per_chunk),
        grid=(t // tn, n_exp // ce),
        in_specs=[pl.BlockSpec((D_MODEL, tn), lambda i, c: (0, i)),
                  pl.BlockSpec((ce, D_MODEL), lambda i, c: (c, 0)),
                  pl.BlockSpec((D_MODEL, ce), lambda i, c: (0, c)),
                  bspec, bspec, bspec, bspec,
                  pl.BlockSpec((PEER_HEADS, tn), lambda i, c: (0, i))],
        out_specs=pl.BlockSpec((D_MODEL, tn), lambda i, c: (0, i)),
        out_shape=jax.ShapeDtypeStruct((D_MODEL, t), F32),
        scratch_shapes=[pltpu.VMEM((ce, tn), BF16)],
        compiler_params=_cparams(("parallel", "arbitrary")), name="peer_mix",
    )(ht16, u16, vt16, s1, s2, e1, e2, thr)


def _final_kernel(h_ref, h16_ref, f_ref, p_ref, wg_ref, wp_ref, g_ref, b_ref, o_ref, *, alpha):
    gate = jax.nn.sigmoid(jnp.dot(h16_ref[...], wg_ref[...], preferred_element_type=F32))
    e = gate * jnp.dot(p_ref[...].astype(BF16), wp_ref[...], preferred_element_type=F32)
    o_ref[...] = _layer_norm(alpha * h_ref[...] + f_ref[...] + e, g_ref[...], b_ref[...])


def _final(h, h16, f, p2d, wg, wp, g, b, *, alpha, tm):
    t = h.shape[0]
    rspec = lambda n: pl.BlockSpec((tm, n), lambda i: (i, 0))
    return pl.pallas_call(
        functools.partial(_final_kernel, alpha=alpha),
        grid=(t // tm,),
        in_specs=[rspec(D_MODEL), rspec(D_MODEL), rspec(D_MODEL), rspec(PLE_DIM),
                  _resident((D_MODEL, D_MODEL)), _resident((PLE_DIM, D_MODEL)),
                  _resident((1, D_MODEL)), _resident((1, D_MODEL))],
        out_specs=rspec(D_MODEL),
        out_shape=jax.ShapeDtypeStruct((t, D_MODEL), F32),
        compiler_params=_cparams(("parallel",)), name="ple_ln2",
    )(h, h16, f, p2d, wg, wp, g, b)


def _pick(n, prefs):
    for c in prefs:
        if n % c == 0:
            return c
    return n


def _layer_weights(l, w_in, w_branch, w_out, ln1_g, ln1_b, w_pq, peer_sub_keys, peer_u, peer_v, ln2_g, ln2_b,
                   w_ple_gate, w_ple_proj):
    sk = peer_sub_keys[l]
    half = PEER_DQ // 2
    blk = jnp.zeros((2 * PEER_NKEYS, PEER_DQ), F32)
    blk = blk.at[:PEER_NKEYS, :half].set(sk[0]).at[PEER_NKEYS:, half:].set(sk[1])
    return dict(
        w_perm=_permute_w_in(w_in[l]),
        wba=w_branch[l, :W_ATT].astype(BF16), wbb=w_branch[l, W_ATT:].astype(BF16), wo=w_out[l].astype(BF16),
        g1=ln1_g[l][None], b1=ln1_b[l][None], g2=ln2_g[l][None], b2=ln2_b[l][None],
        wq_t=w_pq[l].T.astype(BF16),
        wk_t=jnp.kron(jnp.eye(PEER_HEADS, dtype=F32), blk).astype(BF16),
        u16=peer_u[l].astype(BF16), vt16=peer_v[l].T.astype(BF16),
        wg=w_ple_gate[l].astype(BF16), wp=w_ple_proj[l].astype(BF16),
    )


def _token_tail(x2d, p2d, oa, ob, sga, sgb, lw, alpha):
    t = x2d.shape[0]
    tm = _pick(t, (256, 128))
    h, h16 = _merge(oa, ob, sga, sgb, x2d, lw["wba"], lw["wbb"], lw["wo"], lw["g1"], lw["b1"], alpha=alpha, tm=tm)
    ht16 = h16.T
    tn = _pick(t, (512, 256, 128))
    s1, s2, e1, e2, thr = _peer_keys(ht16, lw["wq_t"], lw["wk_t"], tn=tn)
    ft = _peer_mix(ht16, lw["u16"], lw["vt16"], s1, s2, e1, e2, thr, tn=tn, i_per_chunk=8)
    return _final(h, h16, ft.T, p2d, lw["wg"], lw["wp"], lw["g2"], lw["b2"], alpha=alpha, tm=tm)


def _heads_major(qi16, rows_shape):
    x = qi16.reshape(rows_shape + (H_IDX, D_IDX))
    return jnp.moveaxis(x, -2, -3)


def kernel(x_prompt, x_sample, p_prompt, p_sample, cache_a_k, cache_a_v, cache_idx_k, cache_b_k, cache_b_v,
           page_table, w_in, w_branch, w_out, ln1_g, ln1_b, w_pq, peer_sub_keys, peer_u, peer_v, ln2_g, ln2_b,
           w_ple_gate, w_ple_proj):
    depth = w_in.shape[0]
    alpha = (2.0 * depth) ** 0.25
    n_batch, seq, _ = x_prompt.shape
    dec_b, dec_t, _ = x_sample.shape
    n_pool, page = cache_a_k.shape[1], cache_a_k.shape[2]
    n_pages = page_table.shape[1]
    n_past = n_pages * page
    rows = 8
    assert dec_t <= rows and page == LANES

    pos_p = jnp.arange(seq, dtype=I32)
    pos_row = n_past + jnp.minimum(jnp.arange(rows, dtype=I32), dec_t - 1)
    pos_s = jnp.tile(pos_row, dec_b)

    xp = x_prompt
    xs = jnp.pad(x_sample, ((0, 0), (0, rows - dec_t), (0, 0))).reshape(dec_b * rows, D_MODEL)
    outs_p = [[] for _ in range(5)]
    outs_s = [[] for _ in range(5)]
    for l in range(depth):
        lw = _layer_weights(l, w_in, w_branch, w_out, ln1_g, ln1_b, w_pq, peer_sub_keys, peer_u, peer_v, ln2_g,
                            ln2_b, w_ple_gate, w_ple_proj)
        new_xp = []
        per_b = [[] for _ in range(5)]
        for b in range(n_batch):
            x2d = xp[b]
            (qa, ka, ka16, va, va16, qi16, ki, wi, qb, kb, kb16, vb, vb16, sga, sgb) = _project(
                x2d, lw["w_perm"], pos_p, _pick(seq, (512, 256, 128)))
            tq = _pick(seq, (256, 128))
            oa = _dsa_prompt(_heads_major(qi16, (seq,)), wi, ki.astype(BF16), qa, ka16, va16,
                             tq=tq, tk=_pick(seq, (512, 256, 128)))
            ob = _sb_prompt(qb, kb16, vb16, tq=tq, tk=tq)
            new_xp.append(_token_tail(x2d, p_prompt[l, b], oa, ob, sga, sgb, lw, alpha))
            for dst, a in zip(per_b, (ka, va, ki, kb, vb)):
                dst.append(a)
        xp = jnp.stack(new_xp)
        shapes = ((N_HEADS, HEAD_DIM), (N_HEADS, HEAD_DIM), (D_IDX,), (N_HEADS, HEAD_DIM), (N_HEADS, HEAD_DIM))
        for dst, a, sh in zip(outs_p, per_b, shapes):
            dst.append(jnp.stack(a).reshape((n_batch, seq) + sh))

        (qa, ka, ka16, va, va16, qi16, ki, wi, qb, kb, kb16, vb, vb16, sga, sgb) = _project(
            xs, lw["w_perm"], pos_s, _pick(dec_b * rows, (256, 128)))

        def new_block(a16):
            a3 = a16.reshape(dec_b, rows, -1)
            return jnp.pad(a3, ((0, 0), (0, page - rows), (0, 0)))

        oa = _dsa_sample(page_table, _heads_major(qi16, (dec_b, rows)), wi.reshape(dec_b, rows, H_IDX),
                         cache_idx_k[l], new_block(ki.astype(BF16)), qa.reshape(dec_b, rows, W_ATT),
                         cache_a_k[l].reshape(n_pool, page, W_ATT), cache_a_v[l].reshape(n_pool, page, W_ATT),
                         new_block(ka16), new_block(va16), n_new=dec_t)
        ob = _sb_sample(page_table, qb.reshape(dec_b, rows, W_ATT), new_block(kb16), new_block(vb16),
                        cache_b_k[l].reshape(n_pool, page, W_ATT), cache_b_v[l].reshape(n_pool, page, W_ATT),
                        n_new=dec_t)
        ps = jnp.pad(p_sample[l], ((0, 0), (0, rows - dec_t), (0, 0))).reshape(dec_b * rows, PLE_DIM)
        xs = _token_tail(xs, ps, oa.reshape(dec_b * rows, W_ATT), ob.reshape(dec_b * rows, W_ATT), sga, sgb, lw,
                         alpha)
        for dst, a, sh in zip(outs_s, (ka, va, ki, kb, vb), shapes):
            dst.append(a.reshape((dec_b, rows) + sh)[:, :dec_t])

    y_sample = xs.reshape(dec_b, rows, D_MODEL)[:, :dec_t]
    return (xp, y_sample) + tuple(jnp.stack(o) for o in outs_p) + tuple(jnp.stack(o) for o in outs_s)
```

```python
import functools
import math

import numpy as np
import jax
import jax.numpy as jnp
from jax import lax
from jax.experimental import pallas as pl
from jax.experimental.pallas import tpu as pltpu

F32 = jnp.float32
BF16 = jnp.bfloat16
I32 = jnp.int32

D_MODEL = 1024
HEAD_DIM = 64
N_HEADS = 8
W_ATT = N_HEADS * HEAD_DIM
ROPE_THETA = 500000.0
ROT_64 = 16
ROT_32 = 8
H_IDX = 8
D_IDX = 32
TOPK_MAX = 256
PEER_HEADS = 8
PEER_NKEYS = 128
PEER_DQ = 128
PEER_TOPK = 16
PLE_DIM = 256
LN_EPS = 1e-5
IN_SIZES = (W_ATT, W_ATT, W_ATT, H_IDX * D_IDX, D_IDX, H_IDX, W_ATT, W_ATT, W_ATT, D_MODEL, D_MODEL)

LANES = 128
SUBLANES = 8
VMEM_LIMIT_BYTES = 56 * 1024 * 1024
INT_MIN = -(2 ** 31)
NEG_BIG = -1e30
M_INIT = -1e20
SB_DEAD = -104.0

C_QA, C_KA, C_VA, C_QI, C_KW = 0, 512, 1024, 1536, 1792
C_QB, C_KB, C_VB, C_GA, C_GB, N_COLS = 1920, 2432, 2944, 3456, 4480, 5504

SAMPLE_ROWS = SUBLANES
STACK = N_HEADS * SAMPLE_ROWS
PAGE_FLAT = N_HEADS * LANES


def _cparams(sem):
    return pltpu.CompilerParams(dimension_semantics=sem, vmem_limit_bytes=VMEM_LIMIT_BYTES)


def _resident(shape):
    zeros = (0,) * len(shape)
    return pl.BlockSpec(shape, lambda *_: zeros, pipeline_mode=pl.Buffered(1))


def _rope_tables(pos):
    posf = pos.astype(F32)

    def cos_sin(r):
        half = r // 2
        inv = ROPE_THETA ** (-(jnp.arange(half, dtype=F32) * 2.0 / r))
        ang = posf[:, None] * inv[None, :]
        return jnp.cos(ang), jnp.sin(ang)

    def lanes(cos, sin, head_dim, half, n_rot_lanes):
        lane = np.arange(LANES)
        m = lane % head_dim
        first = (m < half) & (lane < n_rot_lanes)
        second = (m >= half) & (m < 2 * half) & (lane < n_rot_lanes)
        idx = np.where(first, m, np.where(second, m - half, 0))
        c = jnp.where(first | second, cos[:, idx], 1.0)
        sa = jnp.where(first, -sin[:, idx], 0.0)
        sb = jnp.where(second, sin[:, idx], 0.0)
        return [c, sa, sb]

    c64, s64 = cos_sin(ROT_64)
    c32, s32 = cos_sin(ROT_32)
    tabs = (lanes(c64, s64, HEAD_DIM, ROT_64 // 2, LANES) + lanes(c32, s32, D_IDX, ROT_32 // 2, LANES)
            + lanes(c32, s32, D_IDX, ROT_32 // 2, D_IDX))
    return jnp.stack(tabs).astype(F32)


def _proj_kernel(x_ref, w_ref, tab_ref, qa_ref, ka_ref, ka16_ref, va_ref, va16_ref, qi_ref, ki_ref, ki16_ref,
                 wi_ref, qb_ref, kb_ref, kb16_ref, vb_ref, vb16_ref, sga_ref, sgb_ref):
    xb = x_ref[...].astype(BF16)

    def mm(c0, n):
        return jnp.dot(xb, w_ref[:, c0:c0 + n], preferred_element_type=F32)

    def rope(z, k, sh):
        out = []
        for c in range(z.shape[1] // LANES):
            zc = z[:, c * LANES:(c + 1) * LANES]
            out.append(zc * tab_ref[3 * k] + pltpu.roll(zc, LANES - sh, 1) * tab_ref[3 * k + 1]
                       + pltpu.roll(zc, sh, 1) * tab_ref[3 * k + 2])
        return out[0] if len(out) == 1 else jnp.concatenate(out, axis=1)

    def heads_major(ref, z, width):
        for h in range(z.shape[1] // width):
            ref[h] = z[:, h * width:(h + 1) * width].astype(BF16)

    heads_major(qa_ref, rope(mm(C_QA, W_ATT), 0, ROT_64 // 2), HEAD_DIM)
    z = rope(mm(C_KA, W_ATT), 0, ROT_64 // 2)
    ka_ref[...] = z
    heads_major(ka16_ref, z, HEAD_DIM)
    z = mm(C_VA, W_ATT)
    va_ref[...] = z
    heads_major(va16_ref, z, HEAD_DIM)
    heads_major(qi_ref, rope(mm(C_QI, H_IDX * D_IDX), 1, ROT_32 // 2), D_IDX)
    r = rope(mm(C_KW, LANES), 2, ROT_32 // 2)
    ki_ref[...] = r[:, :D_IDX]
    ki16_ref[...] = r[:, :D_IDX].astype(BF16)
    wi_ref[...] = r[:, D_IDX:D_IDX + H_IDX]
    heads_major(qb_ref, mm(C_QB, W_ATT), HEAD_DIM)
    z = mm(C_KB, W_ATT)
    kb_ref[...] = z
    heads_major(kb16_ref, z, HEAD_DIM)
    z = mm(C_VB, W_ATT)
    vb_ref[...] = z
    heads_major(vb16_ref, z, HEAD_DIM)
    sga_ref[...] = jax.nn.sigmoid(mm(C_GA, D_MODEL)).astype(BF16)
    sgb_ref[...] = jax.nn.sigmoid(mm(C_GB, D_MODEL)).astype(BF16)


def _permute_w_in(w_in):
    qa, ka, va, qi, ki, wi, qb, kb, vb, ga, gb = jnp.split(w_in, np.cumsum(IN_SIZES)[:-1].tolist(), axis=1)
    scale = 1.0 / math.sqrt(HEAD_DIM)
    pad = jnp.zeros((D_MODEL, LANES - D_IDX - H_IDX), w_in.dtype)
    w = jnp.concatenate([qa * scale, ka, va, qi, ki, wi, pad, qb * scale, kb, vb, ga, gb], axis=1)
    assert w.shape[1] == N_COLS
    return w.astype(BF16)


def _project(x2d, w_perm, pos, tm):
    t = x2d.shape[0]
    tabs = _rope_tables(pos)
    rm = lambda n, dt: (jax.ShapeDtypeStruct((t, n), dt), pl.BlockSpec((tm, n), lambda i: (i, 0)))
    hm = lambda d: (jax.ShapeDtypeStruct((N_HEADS, t, d), BF16), pl.BlockSpec((N_HEADS, tm, d), lambda i: (0, i, 0)))
    outs = [hm(HEAD_DIM), rm(W_ATT, F32), hm(HEAD_DIM), rm(W_ATT, F32), hm(HEAD_DIM), hm(D_IDX), rm(D_IDX, F32),
            rm(D_IDX, BF16), rm(H_IDX, F32), hm(HEAD_DIM), rm(W_ATT, F32), hm(HEAD_DIM), rm(W_ATT, F32),
            hm(HEAD_DIM), rm(D_MODEL, BF16), rm(D_MODEL, BF16)]
    return pl.pallas_call(
        _proj_kernel,
        grid=(t // tm,),
        in_specs=[pl.BlockSpec((tm, D_MODEL), lambda i: (i, 0)), _resident((D_MODEL, N_COLS)),
                  pl.BlockSpec((9, tm, LANES), lambda i: (0, i, 0))],
        out_specs=[s for _, s in outs],
        out_shape=[s for s, _ in outs],
        compiler_params=_cparams(("parallel",)),
        name="proj_rope",
    )(x2d, w_perm, tabs)


def _sortable_key(score):
    b = lax.bitcast_convert_type(score, I32)
    return jnp.where(b < 0, b ^ jnp.int32(0x7FFFFFFF), b)


def _kth_largest_key(count_ge, k, rows_shape):
    v = jnp.where(count_ge(jnp.zeros(rows_shape, I32)) >= k, jnp.int32(0), jnp.int32(INT_MIN))

    def bit_body(t, v):
        cand = v + lax.shift_left(jnp.int32(1), jnp.int32(30) - t)
        return jnp.where(count_ge(cand) >= k, cand, v)

    return lax.fori_loop(0, 31, bit_body, v)


def _tie_cut(count_eq_below, need, n_bits, rows_shape):
    def bit_body(t, x):
        cand = x + lax.shift_left(jnp.int32(1), jnp.int32(n_bits - 1) - t)
        return jnp.where(count_eq_below(cand) < need, cand, x)

    return lax.fori_loop(0, n_bits, bit_body, jnp.zeros(rows_shape, I32))


def _selected(key, vt, jt, col, on, off):
    return jnp.where(key > vt, on, jnp.where(key == vt, jnp.where(col <= jt, on, off), off))


def _tri_pairs(t, tq, tk, reverse, j_window=None):
    qi, kj, first, last = [], [], [], []
    for i in range(t // tq):
        jd = ((i + 1) * tq - 1) // tk
        js = list(range(jd + 1))
        if j_window is not None:
            js = [j for j in js if j_window[0] <= jd - j < j_window[1]]
        if reverse:
            js = js[::-1]
        for n, j in enumerate(js):
            qi.append(i), kj.append(j), first.append(int(n == 0)), last.append(int(n == len(js) - 1))
    return tuple(np.asarray(a, np.int32) for a in (qi, kj, first, last))


def _dsa_prompt_kernel(qi_s, kj_s, first_s, last_s, qis_ref, wi_ref, ki_ref, qa_ref, ka_ref, va_ref, o_ref,
                       sc_ref, vthr_ref, jthr_ref, m_ref, l_ref, acc_ref, *, tq, tk, rg, n_top, n_idx_bits):
    p = pl.program_id(0)
    i = qi_s[p]
    j = kj_s[p]
    row = i * tq + lax.broadcasted_iota(I32, (tq, tk), 0)
    lane = lax.broadcasted_iota(I32, (tq, tk), 1)

    @pl.when(first_s[p] == 1)
    def _():
        m_ref[...] = jnp.full(m_ref.shape, M_INIT, F32)
        l_ref[...] = jnp.zeros(l_ref.shape, F32)
        acc_ref[...] = jnp.zeros(acc_ref.shape, F32)
        n_chunks = ((i + 1) * tq + tk - 1) // tk
        w = wi_ref[...]

        def score_chunk(c, carry):
            k_blk = ki_ref[pl.ds(pl.multiple_of(c * tk, tk), tk), :]
            score = jnp.zeros((tq, tk), F32)
            for h in range(H_IDX):
                s = lax.dot_general(qis_ref[h], k_blk, (((1,), (1,)), ((), ())), preferred_element_type=F32)
                score = score + w[:, h:h + 1] * jnp.maximum(s, 0.0)
            score = jnp.where(c * tk + lane <= row, score, -jnp.inf)
            sc_ref[c] = _sortable_key(score)
            return carry

        lax.fori_loop(0, n_chunks, score_chunk, 0)

        lane1 = lax.broadcasted_iota(I32, (rg, LANES), 1)
        for g in range(tq // rg):
            rows = slice(g * rg, (g + 1) * rg)

            def lane_count(ones_fn, rows=rows):
                def body(c, acc):
                    for t in range(tk // LANES):
                        acc = acc + ones_fn(sc_ref[c, rows, t * LANES:(t + 1) * LANES], c * tk + t * LANES)
                    return acc
                acc = lax.fori_loop(0, n_chunks, body, jnp.zeros((rg, LANES), F32))
                return jnp.sum(acc, axis=1, keepdims=True).astype(I32)

            def count_ge(cand):
                cand_b = jnp.broadcast_to(cand, (rg, LANES))
                return lane_count(lambda slab, k0: jnp.where(slab >= cand_b, 1.0, 0.0))

            v = _kth_largest_key(count_ge, n_top, (rg, 1))
            c_gt = count_ge(v + 1)
            need = n_top - c_gt
            excess = count_ge(v) - c_gt - need
            v_b = jnp.broadcast_to(v, (rg, LANES))
            vthr_ref[rows, :] = v_b
            jthr_ref[rows, :] = jnp.full((rg, LANES), 2 ** 30, I32)

            @pl.when(jnp.max(excess.astype(F32)) > 0.0)
            def _():
                def count_eq_below(x):
                    x_b = jnp.broadcast_to(x, (rg, LANES))
                    return lane_count(
                        lambda slab, k0: jnp.where(slab == v_b, jnp.where(k0 + lane1 < x_b, 1.0, 0.0), 0.0))
                jthr_ref[rows, :] = jnp.broadcast_to(_tie_cut(count_eq_below, need, n_idx_bits, (rg, 1)),
                                                     (rg, LANES))

    reps = tk // LANES
    vt = jnp.concatenate([vthr_ref[...]] * reps, axis=1)
    jt = jnp.concatenate([jthr_ref[...]] * reps, axis=1)
    col = j * tk + lane
    sel = jnp.where(col <= row, _selected(sc_ref[j], vt, jt, col, 0.0, NEG_BIG), NEG_BIG)
    for h in range(N_HEADS):
        s = lax.dot_general(qa_ref[h], ka_ref[h], (((1,), (1,)), ((), ())), preferred_element_type=F32) + sel
        m_prev = m_ref[h]
        m_new = jnp.maximum(m_prev, jnp.max(s, axis=1, keepdims=True))
        pr = jnp.exp(s - m_new)
        alpha = jnp.exp(m_prev - m_new)
        l_ref[h] = alpha * l_ref[h] + jnp.sum(pr, axis=1, keepdims=True)
        acc_ref[h] = alpha * acc_ref[h] + jnp.dot(pr.astype(BF16), va_ref[h], preferred_element_type=F32)
        m_ref[h] = m_new

    @pl.when(last_s[p] == 1)
    def _():
        for h in range(N_HEADS):
            o_ref[:, h * HEAD_DIM:(h + 1) * HEAD_DIM] = (acc_ref[h] / l_ref[h]).astype(o_ref.dtype)


def _dsa_prompt(qis, wi, ki16, qa, ka16, va16, *, tq, tk):
    t = qa.shape[1]
    n_top = min(TOPK_MAX, t // 4)
    pairs = _tri_pairs(t, tq, tk, reverse=False)
    kern = functools.partial(_dsa_prompt_kernel, tq=tq, tk=tk, rg=min(tq, LANES), n_top=n_top,
                             n_idx_bits=max(1, int(math.ceil(math.log2(t)))))
    qmap = lambda p, qi, kj, fi, la: (0, qi[p], 0)
    kmap = lambda p, qi, kj, fi, la: (0, kj[p], 0)
    grid_spec = pltpu.PrefetchScalarGridSpec(
        num_scalar_prefetch=4,
        grid=(int(pairs[0].shape[0]),),
        in_specs=[
            pl.BlockSpec((H_IDX, tq, D_IDX), qmap),
            pl.BlockSpec((tq, H_IDX), lambda p, qi, kj, fi, la: (qi[p], 0)),
            pl.BlockSpec((t, D_IDX), lambda p, qi, kj, fi, la: (0, 0)),
            pl.BlockSpec((N_HEADS, tq, HEAD_DIM), qmap),
            pl.BlockSpec((N_HEADS, tk, HEAD_DIM), kmap),
            pl.BlockSpec((N_HEADS, tk, HEAD_DIM), kmap),
        ],
        out_specs=pl.BlockSpec((tq, W_ATT), lambda p, qi, kj, fi, la: (qi[p], 0)),
        scratch_shapes=[
            pltpu.VMEM((t // tk, tq, tk), I32),
            pltpu.VMEM((tq, LANES), I32),
            pltpu.VMEM((tq, LANES), I32),
            pltpu.VMEM((N_HEADS, tq, 1), F32),
            pltpu.VMEM((N_HEADS, tq, 1), F32),
            pltpu.VMEM((N_HEADS, tq, HEAD_DIM), F32),
        ],
    )
    return pl.pallas_call(
        kern, grid_spec=grid_spec, out_shape=jax.ShapeDtypeStruct((t, W_ATT), BF16),
        compiler_params=_cparams(("arbitrary",)), name="dsa_prompt",
    )(*pairs, qis, wi, ki16, qa, ka16, va16)


def _log_sigmoid_neg(z):
    return -(jnp.maximum(z, 0.0) + jnp.log1p(jnp.exp(-jnp.abs(z))))


def _sum_of_later(lg, tri):
    hi = lg.astype(BF16)
    lo = (lg - hi.astype(F32)).astype(BF16)
    return jnp.dot(hi, tri, preferred_element_type=F32) + jnp.dot(lo, tri, preferred_element_type=F32)


def _later_matrix(n, group):
    r = np.arange(n) // group
    return jnp.asarray((r[:, None] > r[None, :]).astype(np.float32)).astype(BF16)


def _sb_prompt_kernel(qi_s, kj_s, first_s, last_s, q_ref, k_ref, v_ref, tri_ref, *rest, tq, tk, resume):
    if resume:
        acc_in_ref, carry_in_ref, acc_out_ref, carry_out_ref, carry_ref, acc_ref, done_ref = rest
    else:
        acc_out_ref, carry_out_ref, carry_ref, acc_ref, done_ref = rest
    p = pl.program_id(0)
    i = qi_s[p]
    j = kj_s[p]

    @pl.when(first_s[p] == 1)
    def _():
        if resume:
            worst = None
            for h in range(N_HEADS):
                acc_ref[h] = acc_in_ref[:, h * HEAD_DIM:(h + 1) * HEAD_DIM]
                c = carry_in_ref[:, h:h + 1]
                carry_ref[h] = c
                worst = c if worst is None else jnp.maximum(worst, c)
            done_ref[0] = (jnp.max(worst) < SB_DEAD).astype(I32)
        else:
            carry_ref[...] = jnp.zeros(carry_ref.shape, F32)
            acc_ref[...] = jnp.zeros(acc_ref.shape, F32)
            done_ref[0] = 0

    @pl.when(done_ref[0] == 0)
    def _():
        row = i * tq + lax.broadcasted_iota(I32, (tq, tk), 0)
        col = j * tk + lax.broadcasted_iota(I32, (tq, tk), 1)
        mask = col < row
        tri = tri_ref[...]
        worst = None
        for h in range(N_HEADS):
            z = lax.dot_general(q_ref[h], k_ref[h], (((1,), (1,)), ((), ())), preferred_element_type=F32)
            lsm = _log_sigmoid_neg(z)
            lg = jnp.where(mask, lsm, 0.0)
            later = _sum_of_later(lg, tri)
            carry = carry_ref[h]
            a = jnp.where(mask, jnp.exp(lsm + z + later + carry), 0.0)
            acc_ref[h] = acc_ref[h] + jnp.dot(a.astype(BF16), v_ref[h], preferred_element_type=F32)
            carry = carry + later[:, 0:1] + lg[:, 0:1]
            carry_ref[h] = carry
            worst = carry if worst is None else jnp.maximum(worst, carry)
        done_ref[0] = (jnp.max(worst) < SB_DEAD).astype(I32)

    @pl.when(last_s[p] == 1)
    def _():
        for h in range(N_HEADS):
            acc_out_ref[:, h * HEAD_DIM:(h + 1) * HEAD_DIM] = acc_ref[h]
            carry_out_ref[:, h:h + 1] = carry_ref[h]


def _sb_prompt_call(pairs, qb, kb16, vb16, tri, state, *, tq, tk):
    t = qb.shape[1]
    resume = state is not None
    qmap = lambda p, qi, kj, fi, la: (0, qi[p], 0)
    kmap = lambda p, qi, kj, fi, la: (0, kj[p], 0)
    rmap = lambda p, qi, kj, fi, la: (qi[p], 0)
    in_specs = [pl.BlockSpec((N_HEADS, tq, HEAD_DIM), qmap), pl.BlockSpec((N_HEADS, tk, HEAD_DIM), kmap),
                pl.BlockSpec((N_HEADS, tk, HEAD_DIM), kmap), pl.BlockSpec((tk, tk), lambda p, *_: (0, 0))]
    args = [qb, kb16, vb16, tri]
    if resume:
        in_specs += [pl.BlockSpec((tq, W_ATT), rmap), pl.BlockSpec((tq, N_HEADS), rmap)]
        args += list(state)
    grid_spec = pltpu.PrefetchScalarGridSpec(
        num_scalar_prefetch=4,
        grid=(int(pairs[0].shape[0]),),
        in_specs=in_specs,
        out_specs=[pl.BlockSpec((tq, W_ATT), rmap), pl.BlockSpec((tq, N_HEADS), rmap)],
        scratch_shapes=[pltpu.VMEM((N_HEADS, tq, 1), F32), pltpu.VMEM((N_HEADS, tq, HEAD_DIM), F32),
                        pltpu.SMEM((1,), I32)],
    )
    return pl.pallas_call(
        functools.partial(_sb_prompt_kernel, tq=tq, tk=tk, resume=resume),
        grid_spec=grid_spec,
        out_shape=[jax.ShapeDtypeStruct((t, W_ATT), F32), jax.ShapeDtypeStruct((t, N_HEADS), F32)],
        input_output_aliases=({4 + 4: 0, 4 + 5: 1} if resume else {}),
        compiler_params=_cparams(("arbitrary",)), name="sb_prompt_far" if resume else "sb_prompt_near",
    )(*pairs, *args)


def _sb_prompt(qb, kb16, vb16, *, tq, tk, near_blocks):
    t = qb.shape[1]
    tri = _later_matrix(tk, 1)
    near = _tri_pairs(t, tq, tk, True, (0, near_blocks))
    far = _tri_pairs(t, tq, tk, True, (near_blocks, t))
    acc, carry = _sb_prompt_call(near, qb, kb16, vb16, tri, None, tq=tq, tk=tk)
    if far[0].shape[0] == 0:
        return acc
    first_row = int(far[0][0]) * tq
    alive = jnp.max(carry[first_row:]) >= SB_DEAD
    return lax.cond(
        alive,
        lambda a, c: _sb_prompt_call(far, qb, kb16, vb16, tri, (a, c), tq=tq, tk=tk)[0],
        lambda a, c: a, acc, carry)


def _page_flat(ref):
    return ref[0, 0].reshape(PAGE_FLAT, HEAD_DIM).astype(BF16)


def _dsa_sample_kernel(pt_s, qis_ref, w_ref, qa_ref, kin_ref, kan_ref, van_ref, exp_ref, hm_ref, *rest, g_pages,
                       n_pages, n_new, n_top, n_idx_bits):
    kip = rest[:g_pages]
    kap = rest[g_pages:2 * g_pages]
    vap = rest[2 * g_pages:3 * g_pages]
    o_ref, sc_ref, vthr_ref, jthr_ref, m_ref, l_ref, acc_ref = rest[3 * g_pages:]
    s = pl.program_id(1)
    n_steps = n_pages // g_pages
    n_blocks = n_pages + 1
    r_new = jnp.minimum(lax.broadcasted_iota(I32, (SAMPLE_ROWS, LANES), 0), n_new - 1)
    lane = lax.broadcasted_iota(I32, (SAMPLE_ROWS, LANES), 1)
    new_valid = (lane <= r_new, lane < n_new)

    def scores(k_cat):
        n = k_cat.shape[0] // LANES
        s1 = lax.dot_general(qis_ref[0], k_cat, (((1,), (1,)), ((), ())), preferred_element_type=F32)
        wt = w_ref[0] if n == 1 else jnp.concatenate([w_ref[0]] * n, axis=1)
        t = wt * jnp.maximum(s1, 0.0)
        score = t[0:SAMPLE_ROWS]
        for h in range(1, H_IDX):
            score = score + t[h * SAMPLE_ROWS:(h + 1) * SAMPLE_ROWS]
        return score

    @pl.when(s < n_steps)
    def _():
        key = _sortable_key(scores(jnp.concatenate([r[0, 0].astype(BF16) for r in kip], axis=0)))
        for g in range(g_pages):
            sc_ref[s * g_pages + g] = key[:, g * LANES:(g + 1) * LANES]

    @pl.when(s == n_steps)
    def _():
        m_ref[...] = jnp.full(m_ref.shape, M_INIT, F32)
        l_ref[...] = jnp.zeros(l_ref.shape, F32)
        acc_ref[...] = jnp.zeros(acc_ref.shape, F32)
        key = _sortable_key(scores(kin_ref[0]))
        causal, exists = new_valid
        neg_inf_key = _sortable_key(jnp.full((SAMPLE_ROWS, LANES), -jnp.inf, F32))
        sc_ref[n_pages] = jnp.where(exists, jnp.where(causal, key, neg_inf_key), jnp.int32(INT_MIN))
        keys = sc_ref[...]
        idx = (lax.broadcasted_iota(I32, (n_blocks, SAMPLE_ROWS, LANES), 0) * LANES
               + lax.broadcasted_iota(I32, (n_blocks, SAMPLE_ROWS, LANES), 2))

        def count(hit):
            return jnp.sum(jnp.sum(jnp.where(hit, 1.0, 0.0), axis=0), axis=1, keepdims=True).astype(I32)

        def count_ge(cand):
            return count(keys >= cand[None])

        v = _kth_largest_key(count_ge, n_top, (SAMPLE_ROWS, 1))
        c_gt = count_ge(v + 1)
        need = n_top - c_gt
        excess = count_ge(v) - c_gt - need
        vthr_ref[...] = jnp.broadcast_to(v, (SAMPLE_ROWS, LANES))
        jthr_ref[...] = jnp.full((SAMPLE_ROWS, LANES), 2 ** 30, I32)

        @pl.when(jnp.max(excess.astype(F32)) > 0.0)
        def _():
            def count_eq_below(x):
                return count(jnp.logical_and(keys == v[None], idx < x[None]))
            jthr_ref[...] = jnp.broadcast_to(_tie_cut(count_eq_below, need, n_idx_bits, (SAMPLE_ROWS, 1)),
                                             (SAMPLE_ROWS, LANES))

    def chosen(blk_idx, valid):
        one = _selected(sc_ref[blk_idx], vthr_ref[...], jthr_ref[...], blk_idx * LANES + lane, 1.0, 0.0)
        if valid is not None:
            one = jnp.where(jnp.logical_and(valid[0], valid[1]), one, 0.0)
        return jnp.dot(one.astype(BF16), exp_ref[...], preferred_element_type=F32)

    def attend(k_flat, v_flat, picked):
        n = k_flat.shape[0]
        hm = hm_ref[...]
        hm = hm[:, :n] if n <= PAGE_FLAT else jnp.concatenate([hm] * (n // PAGE_FLAT), axis=1)
        sel = jnp.where(jnp.concatenate([picked] * N_HEADS, axis=0) * hm > 0.5, 0.0, NEG_BIG)
        sc = lax.dot_general(qa_ref[0], k_flat, (((1,), (1,)), ((), ())), preferred_element_type=F32) + sel
        m_prev = m_ref[...]
        m_new = jnp.maximum(m_prev, jnp.max(sc, axis=1, keepdims=True))
        pr = jnp.exp(sc - m_new)
        alpha = jnp.exp(m_prev - m_new)
        l_ref[...] = alpha * l_ref[...] + jnp.sum(pr, axis=1, keepdims=True)
        acc_ref[...] = alpha * acc_ref[...] + jnp.dot(pr.astype(BF16), v_flat, preferred_element_type=F32)
        m_ref[...] = m_new

    @pl.when(jnp.logical_and(s > n_steps, s <= 2 * n_steps))
    def _():
        p0 = (s - n_steps - 1) * g_pages
        attend(jnp.concatenate([_page_flat(r) for r in kap], axis=0),
               jnp.concatenate([_page_flat(r) for r in vap], axis=0),
               jnp.concatenate([chosen(p0 + g, None) for g in range(g_pages)], axis=1))

    @pl.when(s == 2 * n_steps + 1)
    def _():
        n = kan_ref.shape[1]
        attend(kan_ref[0], van_ref[0], chosen(n_pages, new_valid)[:, :n])
        o_ref[0] = (acc_ref[...] / l_ref[...]).astype(o_ref.dtype)


def _expand_matrix():
    return jnp.asarray((np.arange(LANES)[:, None] == np.arange(PAGE_FLAT)[None, :] // N_HEADS)
                       .astype(np.float32)).astype(BF16)


def _head_match():
    return jnp.asarray((np.arange(STACK)[:, None] // SAMPLE_ROWS == np.arange(PAGE_FLAT)[None, :] % N_HEADS)
                       .astype(np.float32))


def _dsa_sample(page_table, layer, qis, w_rep, qa, ki_new, ka_new, va_new, cache_ik, cache_ak, cache_av, *,
                n_new, g_pages):
    b, n_pages = page_table.shape
    page = cache_ak.shape[2]
    assert page == LANES and n_pages % g_pages == 0
    n_steps = n_pages // g_pages
    n_top = min(TOPK_MAX, (n_pages * page + n_new) // 4)
    kern = functools.partial(_dsa_sample_kernel, g_pages=g_pages, n_pages=n_pages, n_new=n_new, n_top=n_top,
                             n_idx_bits=max(1, int(math.ceil(math.log2((n_pages + 1) * page)))))
    seq3 = lambda i, s, pt: (i, 0, 0)

    def score_page(g):
        return lambda i, s, pt: (layer, pt[i, jnp.minimum(s, n_steps - 1) * g_pages + g], 0, 0)

    def attend_page(g):
        return lambda i, s, pt: (layer, pt[i, jnp.clip(s - n_steps - 1, 0, n_steps - 1) * g_pages + g], 0, 0, 0)

    grid_spec = pltpu.PrefetchScalarGridSpec(
        num_scalar_prefetch=1,
        grid=(b, 2 * n_steps + 2),
        in_specs=[pl.BlockSpec((1, STACK, D_IDX), seq3), pl.BlockSpec((1, STACK, LANES), seq3),
                  pl.BlockSpec((1, STACK, HEAD_DIM), seq3), pl.BlockSpec((1, page, D_IDX), seq3),
                  pl.BlockSpec((1, STACK, HEAD_DIM), seq3), pl.BlockSpec((1, STACK, HEAD_DIM), seq3),
                  pl.BlockSpec((LANES, PAGE_FLAT), lambda i, s, pt: (0, 0)),
                  pl.BlockSpec((STACK, PAGE_FLAT), lambda i, s, pt: (0, 0))]
        + [pl.BlockSpec((1, 1, page, D_IDX), score_page(g)) for g in range(g_pages)]
        + [pl.BlockSpec((1, 1, page, N_HEADS, HEAD_DIM), attend_page(g)) for g in range(g_pages)]
        + [pl.BlockSpec((1, 1, page, N_HEADS, HEAD_DIM), attend_page(g)) for g in range(g_pages)],
        out_specs=pl.BlockSpec((1, STACK, HEAD_DIM), seq3),
        scratch_shapes=[
            pltpu.VMEM((n_pages + 1, SAMPLE_ROWS, LANES), I32),
            pltpu.VMEM((SAMPLE_ROWS, LANES), I32),
            pltpu.VMEM((SAMPLE_ROWS, LANES), I32),
            pltpu.VMEM((STACK, 1), F32),
            pltpu.VMEM((STACK, 1), F32),
            pltpu.VMEM((STACK, HEAD_DIM), F32),
        ],
    )
    return pl.pallas_call(
        kern, grid_spec=grid_spec, out_shape=jax.ShapeDtypeStruct((b, STACK, HEAD_DIM), BF16),
        compiler_params=_cparams(("arbitrary", "arbitrary")), name="dsa_sample",
    )(page_table, qis, w_rep, qa, ki_new, ka_new, va_new, _expand_matrix(), _head_match(),
      *([cache_ik] * g_pages), *([cache_ak] * g_pages), *([cache_av] * g_pages))


def _sb_stack_block(q, k_flat, v_flat, valid, tri, carry, acc):
    z = lax.dot_general(q, k_flat, (((1,), (1,)), ((), ())), preferred_element_type=F32)
    lsm = _log_sigmoid_neg(z)
    lg = jnp.where(valid, lsm, 0.0)
    later = _sum_of_later(lg, tri)
    a = jnp.where(valid, jnp.exp(lsm + z + later + carry), 0.0)
    acc = acc + jnp.dot(a.astype(BF16), v_flat, preferred_element_type=F32)
    return carry + jnp.sum(lg, axis=1, keepdims=True), acc


def _sb_sample_near_kernel(pt_s, q_ref, kn_ref, vn_ref, tri_ref, hm_ref, *rest, n_near, n_new):
    kp = rest[:n_near]
    vp = rest[n_near:2 * n_near]
    acc_ref, carry_ref = rest[2 * n_near:]
    q = q_ref[0]
    n = kn_ref.shape[1]
    row = lax.broadcasted_iota(I32, (STACK, n), 0)
    col = lax.broadcasted_iota(I32, (STACK, n), 1)
    r_new = jnp.minimum(row % SAMPLE_ROWS, n_new - 1)
    valid = jnp.logical_and(col % N_HEADS == row // SAMPLE_ROWS, col // N_HEADS < r_new)
    carry, acc = _sb_stack_block(q, kn_ref[0], vn_ref[0], valid, tri_ref[:n, :n],
                                 jnp.zeros((STACK, 1), F32), jnp.zeros((STACK, HEAD_DIM), F32))
    valid = hm_ref[...] > 0.5
    for g in range(n_near):
        carry, acc = _sb_stack_block(q, _page_flat(kp[g]), _page_flat(vp[g]), valid, tri_ref[...], carry, acc)
    acc_ref[0] = acc
    carry_ref[0] = jnp.broadcast_to(carry, (STACK, LANES))


def _sb_sample_far_kernel(pt_s, q_ref, tri_ref, hm_ref, kp_ref, vp_ref, acc_in_ref, carry_in_ref, acc_ref,
                          carry_ref, done_ref):
    s = pl.program_id(1)

    @pl.when(s == 0)
    def _():
        acc_ref[...] = acc_in_ref[...]
        carry_ref[...] = carry_in_ref[...]
        done_ref[0] = (jnp.max(carry_in_ref[...]) < SB_DEAD).astype(I32)

    @pl.when(done_ref[0] == 0)
    def _():
        carry, acc = _sb_stack_block(q_ref[0], _page_flat(kp_ref), _page_flat(vp_ref), hm_ref[...] > 0.5,
                                     tri_ref[...], carry_ref[0][:, 0:1], acc_ref[0])
        acc_ref[0] = acc
        carry_ref[0] = jnp.broadcast_to(carry, (STACK, LANES))
        done_ref[0] = (jnp.max(carry) < SB_DEAD).astype(I32)


def _sb_sample(page_table, layer, qb, kb_new, vb_new, cache_bk, cache_bv, *, n_new, n_near):
    b, n_pages = page_table.shape
    page = cache_bk.shape[2]
    n_near = min(n_near, n_pages)
    tri = _later_matrix(PAGE_FLAT, N_HEADS)
    hm = _head_match()
    seq3 = lambda i, *_: (i, 0, 0)
    const2 = lambda *_: (0, 0)
    page_blk = (1, 1, page, N_HEADS, HEAD_DIM)
    state_shapes = [jax.ShapeDtypeStruct((b, STACK, HEAD_DIM), F32), jax.ShapeDtypeStruct((b, STACK, LANES), F32)]
    state_specs = [pl.BlockSpec((1, STACK, HEAD_DIM), seq3), pl.BlockSpec((1, STACK, LANES), seq3)]

    def near_page(g):
        return lambda i, pt: (layer, pt[i, n_pages - 1 - g], 0, 0, 0)

    near_spec = pltpu.PrefetchScalarGridSpec(
        num_scalar_prefetch=1, grid=(b,),
        in_specs=[pl.BlockSpec((1, STACK, HEAD_DIM), seq3), pl.BlockSpec((1, STACK, HEAD_DIM), seq3),
                  pl.BlockSpec((1, STACK, HEAD_DIM), seq3), pl.BlockSpec((PAGE_FLAT, PAGE_FLAT), const2),
                  pl.BlockSpec((STACK, PAGE_FLAT), const2)]
        + [pl.BlockSpec(page_blk, near_page(g)) for g in range(n_near)] * 2,
        out_specs=state_specs)
    acc, carry = pl.pallas_call(
        functools.partial(_sb_sample_near_kernel, n_near=n_near, n_new=n_new),
        grid_spec=near_spec, out_shape=state_shapes,
        compiler_params=_cparams(("arbitrary",)), name="sb_sample_near",
    )(page_table, qb, kb_new, vb_new, tri, hm, *([cache_bk] * n_near), *([cache_bv] * n_near))
    n_far = n_pages - n_near
    if n_far == 0:
        return acc

    far_page = lambda i, s, pt: (layer, pt[i, n_far - 1 - s], 0, 0, 0)
    far_spec = pltpu.PrefetchScalarGridSpec(
        num_scalar_prefetch=1, grid=(b, n_far),
        in_specs=[pl.BlockSpec((1, STACK, HEAD_DIM), seq3), pl.BlockSpec((PAGE_FLAT, PAGE_FLAT), const2),
                  pl.BlockSpec((STACK, PAGE_FLAT), const2), pl.BlockSpec(page_blk, far_page),
                  pl.BlockSpec(page_blk, far_page)] + state_specs,
        out_specs=state_specs, scratch_shapes=[pltpu.SMEM((1,), I32)])

    def far(a, c):
        return pl.pallas_call(
            _sb_sample_far_kernel, grid_spec=far_spec, out_shape=state_shapes,
            compiler_params=_cparams(("arbitrary", "arbitrary")), name="sb_sample_far",
        )(page_table, qb, tri, hm, cache_bk, cache_bv, a, c)[0]

    return lax.cond(jnp.max(carry) >= SB_DEAD, far, lambda a, c: a, acc, carry)


def _layer_norm(y, g, b):
    mu = jnp.mean(y, axis=1, keepdims=True)
    d = y - mu
    var = jnp.mean(d * d, axis=1, keepdims=True)
    return d * lax.rsqrt(var + LN_EPS) * g + b


def _merge_kernel(oa_ref, ob_ref, sga_ref, sgb_ref, x_ref, wba_ref, wbb_ref, wo_ref, g_ref, b_ref, h_ref, h16_ref,
                  *, alpha):
    oa = jnp.dot(oa_ref[...].astype(BF16), wba_ref[...], preferred_element_type=F32)
    ob = jnp.dot(ob_ref[...].astype(BF16), wbb_ref[...], preferred_element_type=F32)
    mix = sga_ref[...].astype(F32) * oa + sgb_ref[...].astype(F32) * ob
    y = alpha * x_ref[...] + jnp.dot(mix.astype(BF16), wo_ref[...], preferred_element_type=F32)
    h = _layer_norm(y, g_ref[...], b_ref[...])
    h_ref[...] = h
    h16_ref[...] = h.astype(BF16)


def _merge(oa, ob, sga, sgb, x2d, wba, wbb, wo, g, b, *, alpha, tm):
    t = x2d.shape[0]
    rspec = lambda n: pl.BlockSpec((tm, n), lambda i: (i, 0))
    return pl.pallas_call(
        functools.partial(_merge_kernel, alpha=alpha),
        grid=(t // tm,),
        in_specs=[rspec(W_ATT), rspec(W_ATT), rspec(D_MODEL), rspec(D_MODEL), rspec(D_MODEL),
                  _resident((W_ATT, D_MODEL)), _resident((W_ATT, D_MODEL)), _resident((D_MODEL, D_MODEL)),
                  _resident((1, D_MODEL)), _resident((1, D_MODEL))],
        out_specs=[rspec(D_MODEL), rspec(D_MODEL)],
        out_shape=[jax.ShapeDtypeStruct((t, D_MODEL), F32), jax.ShapeDtypeStruct((t, D_MODEL), BF16)],
        compiler_params=_cparams(("parallel",)), name="merge_ln1",
    )(oa, ob, sga, sgb, x2d, wba, wbb, wo, g, b)


def _top_rows(x, n):
    out = []
    for _ in range(n):
        m = jnp.max(x, axis=0, keepdims=True)
        out.append(m)
        x = jnp.where(x == m, -jnp.inf, x)
    return out


def _peer_keys_kernel(ht_ref, wq_ref, wk_ref, s1_ref, s2_ref, e1_ref, e2_ref, thr_ref):
    qt = jnp.dot(wq_ref[...], ht_ref[...], preferred_element_type=F32)
    st = jnp.dot(wk_ref[...], qt.astype(BF16), preferred_element_type=F32)
    for h in range(PEER_HEADS):
        s1 = st[h * 2 * PEER_NKEYS:h * 2 * PEER_NKEYS + PEER_NKEYS]
        s2 = st[h * 2 * PEER_NKEYS + PEER_NKEYS:(h + 1) * 2 * PEER_NKEYS]
        t1 = _top_rows(s1, PEER_TOPK)
        t2 = jnp.concatenate(_top_rows(s2, PEER_TOPK), axis=0)
        cand = jnp.concatenate([t1[a] + t2 for a in range(PEER_TOPK)], axis=0)
        cmax = t1[0] + t2[0:1]
        cur = cand
        tot = jnp.zeros_like(cmax)
        thr = cmax
        for _ in range(PEER_TOPK):
            m = jnp.max(cur, axis=0, keepdims=True)
            hit = cur == m
            tot_new = tot + jnp.sum(jnp.where(hit, 1.0, 0.0), axis=0, keepdims=True)
            thr = jnp.where(jnp.logical_and(tot < PEER_TOPK, tot_new >= PEER_TOPK), m, thr)
            tot = tot_new
            cur = jnp.where(hit, -jnp.inf, cur)
        z = jnp.sum(jnp.where(cand >= thr, jnp.exp(cand - cmax), 0.0), axis=0, keepdims=True)
        s1_ref[h] = s1
        s2_ref[h] = s2
        e1_ref[h] = jnp.exp(s1 - t1[0]) / z
        e2_ref[h] = jnp.exp(s2 - t2[0:1])
        thr_ref[h:h + 1, :] = thr


def _peer_keys(ht16, wq_t, wk_t, *, tn):
    t = ht16.shape[1]
    big = jax.ShapeDtypeStruct((PEER_HEADS, PEER_NKEYS, t), F32)
    bspec = pl.BlockSpec((PEER_HEADS, PEER_NKEYS, tn), lambda i: (0, 0, i))
    return pl.pallas_call(
        _peer_keys_kernel,
        grid=(t // tn,),
        in_specs=[pl.BlockSpec((D_MODEL, tn), lambda i: (0, i)), _resident(wq_t.shape), _resident(wk_t.shape)],
        out_specs=[bspec, bspec, bspec, bspec, pl.BlockSpec((PEER_HEADS, tn), lambda i: (0, i))],
        out_shape=[big, big, big, big, jax.ShapeDtypeStruct((PEER_HEADS, t), F32)],
        compiler_params=_cparams(("parallel",)), name="peer_keys",
    )(ht16, wq_t, wk_t)


def _gelu_tanh(x):
    return 0.5 * x * (1.0 + jnp.tanh(math.sqrt(2.0 / math.pi) * (x + 0.044715 * (x * x * x))))


def _peer_mix_kernel(ht_ref, u_ref, vt_ref, s1_ref, s2_ref, e1_ref, e2_ref, thr_ref, ft_ref, g_ref, *, i_per_chunk):
    c = pl.program_id(1)

    @pl.when(c == 0)
    def _():
        ft_ref[...] = jnp.zeros(ft_ref.shape, F32)

    act = _gelu_tanh(jnp.dot(u_ref[...], ht_ref[...], preferred_element_type=F32))
    for ii in range(i_per_chunk):
        i = c * i_per_chunk + ii
        w = None
        for h in range(PEER_HEADS):
            s1_row = s1_ref[h, pl.ds(i, 1), :]
            e1_row = e1_ref[h, pl.ds(i, 1), :]
            gate = jnp.where(s1_row + s2_ref[h] >= thr_ref[h:h + 1, :], e1_row * e2_ref[h], 0.0)
            w = gate if w is None else w + gate
        rs = slice(ii * PEER_NKEYS, (ii + 1) * PEER_NKEYS)
        g_ref[rs, :] = (w * act[rs, :]).astype(BF16)
    ft_ref[...] += jnp.dot(vt_ref[...], g_ref[...], preferred_element_type=F32)


def _peer_mix(ht16, u16, vt16, s1, s2, e1, e2, thr, *, tn, i_per_chunk):
    t = ht16.shape[1]
    ce = i_per_chunk * PEER_NKEYS
    n_exp = u16.shape[0]
    bspec = pl.BlockSpec((PEER_HEADS, PEER_NKEYS, tn), lambda i, c: (0, 0, i))
    return pl.pallas_call(
        functools.partial(_peer_mix_kernel, i_per_chunk=i_per_chunk),
        grid=(t // tn, n_exp // ce),
        in_specs=[pl.BlockSpec((D_MODEL, tn), lambda i, c: (0, i)),
                  pl.BlockSpec((ce, D_MODEL), lambda i, c: (c, 0)),
                  pl.BlockSpec((D_MODEL, ce), lambda i, c: (0, c)),
                  bspec, bspec, bspec, bspec,
                  pl.BlockSpec((PEER_HEADS, tn), lambda i, c: (0, i))],
        out_specs=pl.BlockSpec((D_MODEL, tn), lambda i, c: (0, i)),
        out_shape=jax.ShapeDtypeStruct((D_MODEL, t), F32),
        scratch_shapes=[pltpu.VMEM((ce, tn), BF16)],
        compiler_params=_cparams(("parallel", "arbitrary")), name="peer_mix",
    )(ht16, u16, vt16, s1, s2, e1, e2, thr)


def _final_kernel(h_ref, h16_ref, f_ref, p_ref, wg_ref, wp_ref, g_ref, b_ref, o_ref, *, alpha):
    gate = jax.nn.sigmoid(jnp.dot(h16_ref[...], wg_ref[...], preferred_element_type=F32))
    e = gate * jnp.dot(p_ref[...].astype(BF16), wp_ref[...], preferred_element_type=F32)
    o_ref[...] = _layer_norm(alpha * h_ref[...] + f_ref[...] + e, g_ref[...], b_ref[...])


def _final(h, h16, f, p2d, wg, wp, g, b, *, alpha, tm):
    t = h.shape[0]
    rspec = lambda n: pl.BlockSpec((tm, n), lambda i: (i, 0))
    return pl.pallas_call(
        functools.partial(_final_kernel, alpha=alpha),
        grid=(t // tm,),
        in_specs=[rspec(D_MODEL), rspec(D_MODEL), rspec(D_MODEL), rspec(PLE_DIM),
                  _resident((D_MODEL, D_MODEL)), _resident((PLE_DIM, D_MODEL)),
                  _resident((1, D_MODEL)), _resident((1, D_MODEL))],
        out_specs=rspec(D_MODEL),
        out_shape=jax.ShapeDtypeStruct((t, D_MODEL), F32),
        compiler_params=_cparams(("parallel",)), name="ple_ln2",
    )(h, h16, f, p2d, wg, wp, g, b)


def _pick(n, prefs):
    for c in prefs:
        if n % c == 0:
            return c
    return n


def _tiles(t):
    return dict(proj=_pick(t, (512, 256, 128)), tq=_pick(t, (256, 128)), tk=_pick(t, (512, 256, 128)),
                sb=_pick(t, (256, 128)), rows=_pick(t, (256, 128)), peer=_pick(t, (512, 256, 128)))


def _layer_weights(l, w_in, w_branch, w_out, ln1_g, ln1_b, w_pq, peer_sub_keys, peer_u, peer_v, ln2_g, ln2_b,
                   w_ple_gate, w_ple_proj):
    sk = peer_sub_keys[l]
    half = PEER_DQ // 2
    blk = jnp.zeros((2 * PEER_NKEYS, PEER_DQ), F32)
    blk = blk.at[:PEER_NKEYS, :half].set(sk[0]).at[PEER_NKEYS:, half:].set(sk[1])
    return dict(
        w_perm=_permute_w_in(w_in[l]),
        wba=w_branch[l, :W_ATT].astype(BF16), wbb=w_branch[l, W_ATT:].astype(BF16), wo=w_out[l].astype(BF16),
        g1=ln1_g[l][None], b1=ln1_b[l][None], g2=ln2_g[l][None], b2=ln2_b[l][None],
        wq_t=w_pq[l].T.astype(BF16),
        wk_t=jnp.kron(jnp.eye(PEER_HEADS, dtype=F32), blk).astype(BF16),
        u16=peer_u[l].astype(BF16), vt16=peer_v[l].T.astype(BF16),
        wg=w_ple_gate[l].astype(BF16), wp=w_ple_proj[l].astype(BF16),
    )


def _token_tail(x2d, p2d, oa, ob, sga, sgb, lw, alpha):
    tl = _tiles(x2d.shape[0])
    h, h16 = _merge(oa, ob, sga, sgb, x2d, lw["wba"], lw["wbb"], lw["wo"], lw["g1"], lw["b1"], alpha=alpha,
                    tm=tl["rows"])
    ht16 = h16.T
    s1, s2, e1, e2, thr = _peer_keys(ht16, lw["wq_t"], lw["wk_t"], tn=tl["peer"])
    ft = _peer_mix(ht16, lw["u16"], lw["vt16"], s1, s2, e1, e2, thr, tn=tl["peer"], i_per_chunk=8)
    return _final(h, h16, ft.T, p2d, lw["wg"], lw["wp"], lw["g2"], lw["b2"], alpha=alpha, tm=tl["rows"])


def _stack_heads(a_hm, dec_b):
    h, _, d = a_hm.shape
    return a_hm.reshape(h, dec_b, SAMPLE_ROWS, d).transpose(1, 0, 2, 3).reshape(dec_b, h * SAMPLE_ROWS, d)


def _unstack_heads(o, dec_b):
    return o.reshape(dec_b, N_HEADS, SAMPLE_ROWS, HEAD_DIM).transpose(0, 2, 1, 3).reshape(dec_b * SAMPLE_ROWS, W_ATT)


def kernel(x_prompt, x_sample, p_prompt, p_sample, cache_a_k, cache_a_v, cache_idx_k, cache_b_k, cache_b_v,
           page_table, w_in, w_branch, w_out, ln1_g, ln1_b, w_pq, peer_sub_keys, peer_u, peer_v, ln2_g, ln2_b,
           w_ple_gate, w_ple_proj):
    depth = w_in.shape[0]
    alpha = (2.0 * depth) ** 0.25
    n_batch, seq, _ = x_prompt.shape
    dec_b, dec_t, _ = x_sample.shape
    page = cache_a_k.shape[2]
    n_pages = page_table.shape[1]
    n_past = n_pages * page
    rows = SAMPLE_ROWS
    assert dec_t <= rows and page == LANES and cache_a_k.shape[3:] == (N_HEADS, HEAD_DIM)

    pos_p = jnp.arange(seq, dtype=I32)
    pos_row = n_past + jnp.minimum(jnp.arange(rows, dtype=I32), dec_t - 1)
    pos_s = jnp.tile(pos_row, dec_b)
    g_pages = _pick(n_pages, (8, 4, 2, 1))

    xp = x_prompt
    xs = jnp.pad(x_sample, ((0, 0), (0, rows - dec_t), (0, 0))).reshape(dec_b * rows, D_MODEL)
    shapes = ((N_HEADS, HEAD_DIM), (N_HEADS, HEAD_DIM), (D_IDX,), (N_HEADS, HEAD_DIM), (N_HEADS, HEAD_DIM))
    outs_p = [[] for _ in range(5)]
    outs_s = [[] for _ in range(5)]
    for l in range(depth):
        lw = _layer_weights(l, w_in, w_branch, w_out, ln1_g, ln1_b, w_pq, peer_sub_keys, peer_u, peer_v, ln2_g,
                            ln2_b, w_ple_gate, w_ple_proj)
        tl = _tiles(seq)
        new_xp = []
        per_b = [[] for _ in range(5)]
        for b in range(n_batch):
            x2d = xp[b]
            (qa, ka, ka16, va, va16, qis, ki, ki16, wi, qb, kb, kb16, vb, vb16, sga, sgb) = _project(
                x2d, lw["w_perm"], pos_p, tl["proj"])
            oa = _dsa_prompt(qis, wi, ki16, qa, ka16, va16, tq=tl["tq"], tk=tl["tk"])
            ob = _sb_prompt(qb, kb16, vb16, tq=tl["sb"], tk=tl["sb"], near_blocks=3)
            new_xp.append(_token_tail(x2d, p_prompt[l, b], oa, ob, sga, sgb, lw, alpha))
            for dst, a in zip(per_b, (ka, va, ki, kb, vb)):
                dst.append(a)
        xp = jnp.stack(new_xp)
        for dst, a, sh in zip(outs_p, per_b, shapes):
            dst.append(jnp.stack(a).reshape((n_batch, seq) + sh))

        (qa, ka, ka16, va, va16, qis, ki, ki16, wi, qb, kb, kb16, vb, vb16, sga, sgb) = _project(
            xs, lw["w_perm"], pos_s, _tiles(dec_b * rows)["proj"])

        def new_flat(a):
            return a.astype(BF16).reshape(dec_b, rows * N_HEADS, HEAD_DIM)

        w_rep = jnp.broadcast_to(
            wi.reshape(dec_b, rows, H_IDX).transpose(0, 2, 1).reshape(dec_b, STACK, 1), (dec_b, STACK, LANES))
        ki_new = jnp.pad(ki16.reshape(dec_b, rows, D_IDX), ((0, 0), (0, page - rows), (0, 0)))
        oa = _dsa_sample(page_table, l, _stack_heads(qis, dec_b), w_rep, _stack_heads(qa, dec_b), ki_new,
                         new_flat(ka), new_flat(va), cache_idx_k, cache_a_k, cache_a_v, n_new=dec_t,
                         g_pages=g_pages)
        ob = _sb_sample(page_table, l, _stack_heads(qb, dec_b), new_flat(kb), new_flat(vb), cache_b_k, cache_b_v,
                        n_new=dec_t, n_near=4)
        ps = jnp.pad(p_sample[l], ((0, 0), (0, rows - dec_t), (0, 0))).reshape(dec_b * rows, PLE_DIM)
        xs = _token_tail(xs, ps, _unstack_heads(oa, dec_b), _unstack_heads(ob, dec_b), sga, sgb, lw, alpha)
        for dst, a, sh in zip(outs_s, (ka, va, ki, kb, vb), shapes):
            dst.append(a.reshape((dec_b, rows) + sh)[:, :dec_t])

    y_sample = xs.reshape(dec_b, rows, D_MODEL)[:, :dec_t]
    return (xp, y_sample) + tuple(jnp.stack(o) for o in outs_p) + tuple(jnp.stack(o) for o in outs_s)
```

```python
import functools
import math

import numpy as np
import jax
import jax.numpy as jnp
from jax import lax
from jax.experimental import pallas as pl
from jax.experimental.pallas import tpu as pltpu

F32 = jnp.float32
BF16 = jnp.bfloat16
I32 = jnp.int32

D_MODEL = 1024
HEAD_DIM = 64
N_HEADS = 8
W_ATT = N_HEADS * HEAD_DIM
ROPE_THETA = 500000.0
ROT_64 = 16
ROT_32 = 8
H_IDX = 8
D_IDX = 32
TOPK_MAX = 256
PEER_HEADS = 8
PEER_NKEYS = 128
PEER_DQ = 128
PEER_TOPK = 16
PLE_DIM = 256
LN_EPS = 1e-5
IN_SIZES = (W_ATT, W_ATT, W_ATT, H_IDX * D_IDX, D_IDX, H_IDX, W_ATT, W_ATT, W_ATT, D_MODEL, D_MODEL)

LANES = 128
SUBLANES = 8
VMEM_LIMIT_BYTES = 56 * 1024 * 1024
INT_MIN = -(2 ** 31)
NO_TIE_CUT = 2 ** 30
NEG_BIG = -1e30
M_INIT = -1e20
SB_DEAD = -104.0

C_QA, C_KA, C_VA, C_QI, C_KW = 0, 512, 1024, 1536, 1792
C_QB, C_KB, C_VB, C_GA, C_GB, N_COLS = 1920, 2432, 2944, 3456, 4480, 5504

SAMPLE_ROWS = SUBLANES
STACK = N_HEADS * SAMPLE_ROWS


def _cparams(sem):
    return pltpu.CompilerParams(dimension_semantics=sem, vmem_limit_bytes=VMEM_LIMIT_BYTES)


def _resident(shape):
    zeros = (0,) * len(shape)
    return pl.BlockSpec(shape, lambda *_: zeros, pipeline_mode=pl.Buffered(1))


def _rope_tables(pos):
    posf = pos.astype(F32)

    def cos_sin(r):
        half = r // 2
        inv = ROPE_THETA ** (-(jnp.arange(half, dtype=F32) * 2.0 / r))
        ang = posf[:, None] * inv[None, :]
        return jnp.cos(ang), jnp.sin(ang)

    def lanes(cos, sin, head_dim, half, n_rot_lanes):
        lane = np.arange(LANES)
        m = lane % head_dim
        first = (m < half) & (lane < n_rot_lanes)
        second = (m >= half) & (m < 2 * half) & (lane < n_rot_lanes)
        idx = np.where(first, m, np.where(second, m - half, 0))
        c = jnp.where(first | second, cos[:, idx], 1.0)
        sa = jnp.where(first, -sin[:, idx], 0.0)
        sb = jnp.where(second, sin[:, idx], 0.0)
        return [c, sa, sb]

    c64, s64 = cos_sin(ROT_64)
    c32, s32 = cos_sin(ROT_32)
    tabs = (lanes(c64, s64, HEAD_DIM, ROT_64 // 2, LANES) + lanes(c32, s32, D_IDX, ROT_32 // 2, LANES)
            + lanes(c32, s32, D_IDX, ROT_32 // 2, D_IDX))
    return jnp.stack(tabs).astype(F32)


def _proj_kernel(x_ref, w_ref, tab_ref, qa_ref, ka_ref, ka16_ref, va_ref, va16_ref, qi_ref, ki_ref, ki16_ref,
                 wi_ref, qb_ref, kb_ref, kb16_ref, vb_ref, vb16_ref, sga_ref, sgb_ref):
    xb = x_ref[...].astype(BF16)

    def mm(c0, n):
        return jnp.dot(xb, w_ref[:, c0:c0 + n], preferred_element_type=F32)

    def rope(z, k, sh):
        out = []
        for c in range(z.shape[1] // LANES):
            zc = z[:, c * LANES:(c + 1) * LANES]
            out.append(zc * tab_ref[3 * k] + pltpu.roll(zc, LANES - sh, 1) * tab_ref[3 * k + 1]
                       + pltpu.roll(zc, sh, 1) * tab_ref[3 * k + 2])
        return out[0] if len(out) == 1 else jnp.concatenate(out, axis=1)

    def heads_major(ref, z, width):
        for h in range(z.shape[1] // width):
            ref[h] = z[:, h * width:(h + 1) * width].astype(BF16)

    heads_major(qa_ref, rope(mm(C_QA, W_ATT), 0, ROT_64 // 2), HEAD_DIM)
    z = rope(mm(C_KA, W_ATT), 0, ROT_64 // 2)
    ka_ref[...] = z
    heads_major(ka16_ref, z, HEAD_DIM)
    z = mm(C_VA, W_ATT)
    va_ref[...] = z
    heads_major(va16_ref, z, HEAD_DIM)
    heads_major(qi_ref, rope(mm(C_QI, H_IDX * D_IDX), 1, ROT_32 // 2), D_IDX)
    r = rope(mm(C_KW, LANES), 2, ROT_32 // 2)
    ki_ref[...] = r[:, :D_IDX]
    ki16_ref[...] = r[:, :D_IDX].astype(BF16)
    wi_ref[...] = r[:, D_IDX:D_IDX + H_IDX]
    heads_major(qb_ref, mm(C_QB, W_ATT), HEAD_DIM)
    z = mm(C_KB, W_ATT)
    kb_ref[...] = z
    heads_major(kb16_ref, z, HEAD_DIM)
    z = mm(C_VB, W_ATT)
    vb_ref[...] = z
    heads_major(vb16_ref, z, HEAD_DIM)
    sga_ref[...] = jax.nn.sigmoid(mm(C_GA, D_MODEL)).astype(BF16)
    sgb_ref[...] = jax.nn.sigmoid(mm(C_GB, D_MODEL)).astype(BF16)


def _permute_w_in(w_in):
    qa, ka, va, qi, ki, wi, qb, kb, vb, ga, gb = jnp.split(w_in, np.cumsum(IN_SIZES)[:-1].tolist(), axis=1)
    scale = 1.0 / math.sqrt(HEAD_DIM)
    pad = jnp.zeros((D_MODEL, LANES - D_IDX - H_IDX), w_in.dtype)
    w = jnp.concatenate([qa * scale, ka, va, qi, ki, wi, pad, qb * scale, kb, vb, ga, gb], axis=1)
    assert w.shape[1] == N_COLS
    return w.astype(BF16)


def _project(x2d, w_perm, pos, tm):
    t = x2d.shape[0]
    tabs = _rope_tables(pos)
    rm = lambda n, dt: (jax.ShapeDtypeStruct((t, n), dt), pl.BlockSpec((tm, n), lambda i: (i, 0)))
    hm = lambda d: (jax.ShapeDtypeStruct((N_HEADS, t, d), BF16), pl.BlockSpec((N_HEADS, tm, d), lambda i: (0, i, 0)))
    outs = [hm(HEAD_DIM), rm(W_ATT, F32), hm(HEAD_DIM), rm(W_ATT, F32), hm(HEAD_DIM), hm(D_IDX), rm(D_IDX, F32),
            rm(D_IDX, BF16), rm(H_IDX, F32), hm(HEAD_DIM), rm(W_ATT, F32), hm(HEAD_DIM), rm(W_ATT, F32),
            hm(HEAD_DIM), rm(D_MODEL, BF16), rm(D_MODEL, BF16)]
    return pl.pallas_call(
        _proj_kernel,
        grid=(t // tm,),
        in_specs=[pl.BlockSpec((tm, D_MODEL), lambda i: (i, 0)), _resident((D_MODEL, N_COLS)),
                  pl.BlockSpec((9, tm, LANES), lambda i: (0, i, 0))],
        out_specs=[s for _, s in outs],
        out_shape=[s for s, _ in outs],
        compiler_params=_cparams(("parallel",)),
        name="proj_rope",
    )(x2d, w_perm, tabs)


def _sortable_key(score):
    b = lax.bitcast_convert_type(score, I32)
    return jnp.where(b < 0, b ^ jnp.int32(0x7FFFFFFF), b)


def _top_k_threshold(count_ge, k, n_keys, shape):
    kf = jnp.float32(k)
    c0 = count_ge(jnp.zeros(shape, I32))
    v0 = jnp.where(c0 >= kf, jnp.int32(0), jnp.int32(INT_MIN))
    cnt0 = jnp.where(c0 >= kf, c0, n_keys)

    def cond(st):
        t, _, cnt = st
        return jnp.logical_and(t < 31, jnp.max(jnp.abs(cnt - kf)) > 0.0)

    def body(st):
        t, v, cnt = st
        cand = v + lax.shift_left(jnp.int32(1), jnp.int32(30) - t)
        c = count_ge(cand)
        ok = c >= kf
        return t + 1, jnp.where(ok, cand, v), jnp.where(ok, c, cnt)

    _, v, cnt = lax.while_loop(cond, body, (jnp.int32(0), v0, cnt0))
    return v, cnt


def _tie_cut(count_eq_below, need, n_bits, shape):
    def bit_body(t, x):
        cand = x + lax.shift_left(jnp.int32(1), jnp.int32(n_bits - 1) - t)
        return jnp.where(count_eq_below(cand) < need, cand, x)

    return lax.fori_loop(0, n_bits, bit_body, jnp.zeros(shape, I32))


def _reduce_keys(x, op, width=64):
    n, w = x.shape
    if n % width == 0 and n > width:
        x = op(x.reshape(n // width, width, w), axis=0)
    return op(x, axis=0, keepdims=True)


def _selected(key, vt, jt, idx, on, off):
    return jnp.where(key > vt, on, jnp.where(key == vt, jnp.where(idx <= jt, on, off), off))


def _tri_pairs(t, tq, tk, reverse, j_window=None):
    qi, kj, first, last = [], [], [], []
    for i in range(t // tq):
        jd = ((i + 1) * tq - 1) // tk
        js = list(range(jd + 1))
        if j_window is not None:
            js = [j for j in js if j_window[0] <= jd - j < j_window[1]]
        if reverse:
            js = js[::-1]
        for n, j in enumerate(js):
            qi.append(i), kj.append(j), first.append(int(n == 0)), last.append(int(n == len(js) - 1))
    return tuple(np.asarray(a, np.int32) for a in (qi, kj, first, last))


def _dsa_prompt_kernel(qi_s, kj_s, first_s, last_s, qis_ref, wi_ref, ki_ref, qa_ref, ka_ref, va_ref, o_ref,
                       sc_ref, vthr_ref, jthr_ref, m_ref, l_ref, acc_ref, *, tq, tk, n_top, n_idx_bits):
    p = pl.program_id(0)
    i = qi_s[p]
    j = kj_s[p]
    q_idx = i * tq + lax.broadcasted_iota(I32, (tk, tq), 1)
    k_off = lax.broadcasted_iota(I32, (tk, tq), 0)

    @pl.when(first_s[p] == 1)
    def _():
        m_ref[...] = jnp.full(m_ref.shape, M_INIT, F32)
        l_ref[...] = jnp.zeros(l_ref.shape, F32)
        acc_ref[...] = jnp.zeros(acc_ref.shape, F32)
        n_chunks = ((i + 1) * tq + tk - 1) // tk
        w = wi_ref[...]

        def score_chunk(c, carry):
            k_blk = ki_ref[pl.ds(pl.multiple_of(c * tk, tk), tk), :]
            score = jnp.zeros((tk, tq), F32)
            for h in range(H_IDX):
                s = jnp.dot(k_blk, qis_ref[h], preferred_element_type=F32)
                score = score + w[h:h + 1, :] * jnp.maximum(s, 0.0)
            score = jnp.where(c * tk + k_off <= q_idx, score, -jnp.inf)
            sc_ref[c] = _sortable_key(score)
            return carry

        lax.fori_loop(0, n_chunks, score_chunk, 0)

        cw = min(tk, 64)

        def key_count(ones_fn):
            def body(c, acc):
                for r in range(tk // cw):
                    acc = acc + ones_fn(sc_ref[c, r * cw:(r + 1) * cw, :], c * tk + r * cw)
                return acc
            acc = lax.fori_loop(0, n_chunks, body, jnp.zeros((cw, tq), F32))
            return jnp.sum(acc, axis=0, keepdims=True)

        def count_ge(cand):
            cand_b = jnp.broadcast_to(cand, (cw, tq))
            return key_count(lambda blk, k0: jnp.where(blk >= cand_b, 1.0, 0.0))

        v, cnt = _top_k_threshold(count_ge, n_top, (n_chunks * tk).astype(F32), (1, tq))
        vthr_ref[...] = jnp.broadcast_to(v, (SUBLANES, tq))
        jthr_ref[...] = jnp.full((SUBLANES, tq), NO_TIE_CUT, I32)

        @pl.when(jnp.max(cnt) > n_top)
        def _():
            need = n_top - count_ge(v + 1)

            v_b = jnp.broadcast_to(v, (cw, tq))
            off = lax.broadcasted_iota(I32, (cw, tq), 0)

            def count_eq_below(x):
                x_b = jnp.broadcast_to(x, (cw, tq))
                return key_count(
                    lambda blk, k0: jnp.where(blk == v_b, jnp.where(k0 + off < x_b, 1.0, 0.0), 0.0))
            jthr_ref[...] = jnp.broadcast_to(_tie_cut(count_eq_below, need, n_idx_bits, (1, tq)), (SUBLANES, tq))

    k_idx = j * tk + k_off
    sel = jnp.where(k_idx <= q_idx,
                    _selected(sc_ref[j], vthr_ref[0:1, :], jthr_ref[0:1, :], k_idx, 0.0, NEG_BIG), NEG_BIG)
    m_all = m_ref[...]
    l_all = l_ref[...]
    m_out, l_out = [], []

    def logits(h):
        return jnp.dot(ka_ref[h], qa_ref[h], preferred_element_type=F32) + sel

    ahead = 2
    pending = [logits(h) for h in range(ahead)]
    for h in range(N_HEADS):
        if h % ahead == 0 and h + ahead < N_HEADS:
            pending += [logits(h + ahead + a) for a in range(ahead)]
        s = pending.pop(0)
        m_prev = m_all[h:h + 1, :]
        m_new = jnp.maximum(m_prev, _reduce_keys(s, jnp.max))
        pr = jnp.exp(s - m_new)
        alpha = jnp.exp(m_prev - m_new)
        l_out.append(alpha * l_all[h:h + 1, :] + _reduce_keys(pr, jnp.sum))
        acc_ref[h] = alpha * acc_ref[h] + jnp.dot(va_ref[h], pr.astype(BF16), preferred_element_type=F32)
        m_out.append(m_new)
    m_ref[...] = jnp.concatenate(m_out, axis=0)
    l_ref[...] = jnp.concatenate(l_out, axis=0)

    @pl.when(last_s[p] == 1)
    def _():
        for h in range(N_HEADS):
            o_ref[h * HEAD_DIM:(h + 1) * HEAD_DIM, :] = (acc_ref[h] / l_ref[h:h + 1, :]).astype(o_ref.dtype)


def _dsa_prompt(qis_t, wi_t, ki16, qa_t, ka16, va_t, *, tq, tk):
    t = ki16.shape[0]
    n_top = min(TOPK_MAX, t // 4)
    pairs = _tri_pairs(t, tq, tk, reverse=False)
    kern = functools.partial(_dsa_prompt_kernel, tq=tq, tk=tk, n_top=n_top,
                             n_idx_bits=max(1, int(math.ceil(math.log2(t)))))
    qlane = lambda p, qi, kj, fi, la: (0, 0, qi[p])
    grid_spec = pltpu.PrefetchScalarGridSpec(
        num_scalar_prefetch=4,
        grid=(int(pairs[0].shape[0]),),
        in_specs=[
            pl.BlockSpec((H_IDX, D_IDX, tq), qlane),
            pl.BlockSpec((H_IDX, tq), lambda p, qi, kj, fi, la: (0, qi[p])),
            pl.BlockSpec((t, D_IDX), lambda p, qi, kj, fi, la: (0, 0)),
            pl.BlockSpec((N_HEADS, HEAD_DIM, tq), qlane),
            pl.BlockSpec((N_HEADS, tk, HEAD_DIM), lambda p, qi, kj, fi, la: (0, kj[p], 0)),
            pl.BlockSpec((N_HEADS, HEAD_DIM, tk), lambda p, qi, kj, fi, la: (0, 0, kj[p])),
        ],
        out_specs=pl.BlockSpec((W_ATT, tq), lambda p, qi, kj, fi, la: (0, qi[p])),
        scratch_shapes=[
            pltpu.VMEM((t // tk, tk, tq), I32),
            pltpu.VMEM((SUBLANES, tq), I32),
            pltpu.VMEM((SUBLANES, tq), I32),
            pltpu.VMEM((N_HEADS, tq), F32),
            pltpu.VMEM((N_HEADS, tq), F32),
            pltpu.VMEM((N_HEADS, HEAD_DIM, tq), F32),
        ],
    )
    return pl.pallas_call(
        kern, grid_spec=grid_spec, out_shape=jax.ShapeDtypeStruct((W_ATT, t), BF16),
        compiler_params=_cparams(("arbitrary",)), name="dsa_prompt",
    )(*pairs, qis_t, wi_t, ki16, qa_t, ka16, va_t)


def _log_sigmoid_neg(z):
    return -(jnp.maximum(z, 0.0) + jnp.log1p(jnp.exp(-jnp.abs(z))))


def _sum_of_later(lg, tri):
    hi = lg.astype(BF16)
    lo = (lg - hi.astype(F32)).astype(BF16)
    return jnp.dot(hi, tri, preferred_element_type=F32) + jnp.dot(lo, tri, preferred_element_type=F32)


def _later_matrix(n):
    r = np.arange(n)
    return jnp.asarray((r[:, None] > r[None, :]).astype(np.float32)).astype(BF16)


def _sb_prompt_kernel(qi_s, kj_s, first_s, last_s, q_ref, k_ref, v_ref, tri_ref, *rest, tq, tk, resume):
    if resume:
        acc_in_ref, carry_in_ref, acc_out_ref, carry_out_ref, carry_ref, acc_ref, done_ref = rest
    else:
        acc_out_ref, carry_out_ref, carry_ref, acc_ref, done_ref = rest
    p = pl.program_id(0)
    i = qi_s[p]
    j = kj_s[p]

    @pl.when(first_s[p] == 1)
    def _():
        if resume:
            worst = None
            for h in range(N_HEADS):
                acc_ref[h] = acc_in_ref[:, h * HEAD_DIM:(h + 1) * HEAD_DIM]
                c = carry_in_ref[:, h:h + 1]
                carry_ref[h] = c
                worst = c if worst is None else jnp.maximum(worst, c)
            done_ref[0] = (jnp.max(worst) < SB_DEAD).astype(I32)
        else:
            carry_ref[...] = jnp.zeros(carry_ref.shape, F32)
            acc_ref[...] = jnp.zeros(acc_ref.shape, F32)
            done_ref[0] = 0

    @pl.when(done_ref[0] == 0)
    def _():
        row = i * tq + lax.broadcasted_iota(I32, (tq, tk), 0)
        col = j * tk + lax.broadcasted_iota(I32, (tq, tk), 1)
        mask = col < row
        tri = tri_ref[...]
        zs = [lax.dot_general(q_ref[h], k_ref[h], (((1,), (1,)), ((), ())), preferred_element_type=F32)
              for h in range(N_HEADS)]
        lsms = [_log_sigmoid_neg(z) for z in zs]
        lgs = [jnp.where(mask, lsm, 0.0) for lsm in lsms]
        laters = [_sum_of_later(lg, tri) for lg in lgs]
        worst = None
        for h in range(N_HEADS):
            lg, later = lgs[h], laters[h]
            carry = carry_ref[h]
            a = jnp.where(mask, jnp.exp(lsms[h] + zs[h] + later + carry), 0.0)
            acc_ref[h] = acc_ref[h] + jnp.dot(a.astype(BF16), v_ref[h], preferred_element_type=F32)
            carry = carry + later[:, 0:1] + lg[:, 0:1]
            carry_ref[h] = carry
            worst = carry if worst is None else jnp.maximum(worst, carry)
        done_ref[0] = (jnp.max(worst) < SB_DEAD).astype(I32)

    @pl.when(last_s[p] == 1)
    def _():
        for h in range(N_HEADS):
            acc_out_ref[:, h * HEAD_DIM:(h + 1) * HEAD_DIM] = acc_ref[h]
            carry_out_ref[:, h:h + 1] = carry_ref[h]


def _sb_prompt_call(pairs, qb, kb16, vb16, tri, state, *, tq, tk):
    t = qb.shape[1]
    resume = state is not None
    qmap = lambda p, qi, kj, fi, la: (0, qi[p], 0)
    kmap = lambda p, qi, kj, fi, la: (0, kj[p], 0)
    rmap = lambda p, qi, kj, fi, la: (qi[p], 0)
    in_specs = [pl.BlockSpec((N_HEADS, tq, HEAD_DIM), qmap), pl.BlockSpec((N_HEADS, tk, HEAD_DIM), kmap),
                pl.BlockSpec((N_HEADS, tk, HEAD_DIM), kmap), pl.BlockSpec((tk, tk), lambda p, *_: (0, 0))]
    args = [qb, kb16, vb16, tri]
    if resume:
        in_specs += [pl.BlockSpec((tq, W_ATT), rmap), pl.BlockSpec((tq, N_HEADS), rmap)]
        args += list(state)
    grid_spec = pltpu.PrefetchScalarGridSpec(
        num_scalar_prefetch=4,
        grid=(int(pairs[0].shape[0]),),
        in_specs=in_specs,
        out_specs=[pl.BlockSpec((tq, W_ATT), rmap), pl.BlockSpec((tq, N_HEADS), rmap)],
        scratch_shapes=[pltpu.VMEM((N_HEADS, tq, 1), F32), pltpu.VMEM((N_HEADS, tq, HEAD_DIM), F32),
                        pltpu.SMEM((1,), I32)],
    )
    return pl.pallas_call(
        functools.partial(_sb_prompt_kernel, tq=tq, tk=tk, resume=resume),
        grid_spec=grid_spec,
        out_shape=[jax.ShapeDtypeStruct((t, W_ATT), F32), jax.ShapeDtypeStruct((t, N_HEADS), F32)],
        input_output_aliases=({4 + 4: 0, 4 + 5: 1} if resume else {}),
        compiler_params=_cparams(("arbitrary",)), name="sb_prompt_far" if resume else "sb_prompt_near",
    )(*pairs, *args)


def _sb_prompt(qb, kb16, vb16, *, tq, tk, near_blocks):
    t = qb.shape[1]
    tri = _later_matrix(tk)
    near = _tri_pairs(t, tq, tk, True, (0, near_blocks))
    far = _tri_pairs(t, tq, tk, True, (near_blocks, t))
    acc, carry = _sb_prompt_call(near, qb, kb16, vb16, tri, None, tq=tq, tk=tk)
    if far[0].shape[0] == 0:
        return acc
    first_row = int(far[0][0]) * tq
    alive = jnp.max(carry[first_row:]) >= SB_DEAD
    return lax.cond(
        alive,
        lambda a, c: _sb_prompt_call(far, qb, kb16, vb16, tri, (a, c), tq=tq, tk=tk)[0],
        lambda a, c: a, acc, carry)


def _dsa_sample_kernel(pt_s, qis_ref, w_ref, qa_ref, kin_ref, kan_ref, van_ref, *rest, g_pages, n_pages, n_new,
                       n_top, n_idx_bits):
    kip = rest[:g_pages]
    kap = rest[g_pages:2 * g_pages]
    vap = rest[2 * g_pages:3 * g_pages]
    o_ref, sc_ref, vthr_ref, jthr_ref, m_ref, l_ref, acc_ref = rest[3 * g_pages:]
    s = pl.program_id(1)
    n_steps = n_pages // g_pages
    n_blocks = n_pages + 1
    r_new = jnp.minimum(lax.broadcasted_iota(I32, (SAMPLE_ROWS, LANES), 0), n_new - 1)
    lane = lax.broadcasted_iota(I32, (SAMPLE_ROWS, LANES), 1)

    def scores(k_t):
        n = k_t.shape[1] // LANES
        s1 = jnp.dot(qis_ref[0], k_t, preferred_element_type=F32)
        wt = w_ref[0] if n == 1 else jnp.concatenate([w_ref[0]] * n, axis=1)
        t = wt * jnp.maximum(s1, 0.0)
        score = t[0:SAMPLE_ROWS]
        for h in range(1, H_IDX):
            score = score + t[h * SAMPLE_ROWS:(h + 1) * SAMPLE_ROWS]
        return score

    @pl.when(s < n_steps)
    def _():
        key = _sortable_key(scores(jnp.concatenate([r[0, 0].astype(BF16) for r in kip], axis=1)))
        for g in range(g_pages):
            sc_ref[s * g_pages + g] = key[:, g * LANES:(g + 1) * LANES]

    @pl.when(s == n_steps)
    def _():
        m_ref[...] = jnp.full(m_ref.shape, M_INIT, F32)
        l_ref[...] = jnp.zeros(l_ref.shape, F32)
        acc_ref[...] = jnp.zeros(acc_ref.shape, F32)
        key = _sortable_key(scores(kin_ref[0]))
        neg_inf_key = _sortable_key(jnp.full((SAMPLE_ROWS, LANES), -jnp.inf, F32))
        sc_ref[n_pages] = jnp.where(lane < n_new, jnp.where(lane <= r_new, key, neg_inf_key), jnp.int32(INT_MIN))
        keys = sc_ref[...]
        idx = (lax.broadcasted_iota(I32, (n_blocks, SAMPLE_ROWS, LANES), 0) * LANES
               + lax.broadcasted_iota(I32, (n_blocks, SAMPLE_ROWS, LANES), 2))

        def count(hit):
            return jnp.sum(jnp.sum(jnp.where(hit, 1.0, 0.0), axis=0), axis=1, keepdims=True)

        def count_ge(cand):
            return count(keys >= cand[None])

        v, cnt = _top_k_threshold(count_ge, n_top, jnp.float32(n_blocks * LANES), (SAMPLE_ROWS, 1))
        vthr_ref[...] = jnp.broadcast_to(v, (SAMPLE_ROWS, LANES))
        jthr_ref[...] = jnp.full((SAMPLE_ROWS, LANES), NO_TIE_CUT, I32)

        @pl.when(jnp.max(cnt) > n_top)
        def _():
            need = n_top - count_ge(v + 1)

            def count_eq_below(x):
                return count(jnp.logical_and(keys == v[None], idx < x[None]))
            jthr_ref[...] = jnp.broadcast_to(_tie_cut(count_eq_below, need, n_idx_bits, (SAMPLE_ROWS, 1)),
                                             (SAMPLE_ROWS, LANES))

    def chosen(blk_idx):
        return _selected(sc_ref[blk_idx], vthr_ref[...], jthr_ref[...], blk_idx * LANES + lane, 0.0, NEG_BIG)

    def attend(k_t, v_t, sel):
        q = qa_ref[0]
        sc = jnp.concatenate(
            [jnp.dot(q[h * SAMPLE_ROWS:(h + 1) * SAMPLE_ROWS], k_t[h], preferred_element_type=F32) + sel
             for h in range(N_HEADS)], axis=0)
        m_prev = m_ref[...]
        m_new = jnp.maximum(m_prev, jnp.max(sc, axis=1, keepdims=True))
        pr = jnp.exp(sc - m_new)
        alpha = jnp.exp(m_prev - m_new)
        l_ref[...] = alpha * l_ref[...] + jnp.sum(pr, axis=1, keepdims=True)
        pv = jnp.concatenate(
            [lax.dot_general(pr[h * SAMPLE_ROWS:(h + 1) * SAMPLE_ROWS].astype(BF16), v_t[h],
                             (((1,), (1,)), ((), ())), preferred_element_type=F32) for h in range(N_HEADS)], axis=0)
        acc_ref[...] = alpha * acc_ref[...] + pv
        m_ref[...] = m_new

    def head_pages(refs, h):
        return jnp.concatenate([r[0, 0, h].astype(BF16) for r in refs], axis=1)

    @pl.when(jnp.logical_and(s > n_steps, s <= 2 * n_steps))
    def _():
        p0 = (s - n_steps - 1) * g_pages
        attend([head_pages(kap, h) for h in range(N_HEADS)], [head_pages(vap, h) for h in range(N_HEADS)],
               jnp.concatenate([chosen(p0 + g) for g in range(g_pages)], axis=1))

    @pl.when(s == 2 * n_steps + 1)
    def _():
        sel = jnp.where(jnp.logical_and(lane <= r_new, lane < n_new), chosen(n_pages), NEG_BIG)
        attend([kan_ref[0, h] for h in range(N_HEADS)], [van_ref[0, h] for h in range(N_HEADS)], sel)
        o_ref[0] = (acc_ref[...] / l_ref[...]).astype(o_ref.dtype)


def _dsa_sample(page_table, layer, qis, w_rep, qa, ki_new, ka_new, va_new, cache_ik, cache_ak, cache_av, *,
                n_new, g_pages):
    b, n_pages = page_table.shape
    page = cache_ak.shape[-1]
    assert page == LANES and n_pages % g_pages == 0
    n_steps = n_pages // g_pages
    n_top = min(TOPK_MAX, (n_pages * page + n_new) // 4)
    kern = functools.partial(_dsa_sample_kernel, g_pages=g_pages, n_pages=n_pages, n_new=n_new, n_top=n_top,
                             n_idx_bits=max(1, int(math.ceil(math.log2((n_pages + 1) * page)))))
    seq3 = lambda i, s, pt: (i, 0, 0)
    seq4 = lambda i, s, pt: (i, 0, 0, 0)

    def score_page(g):
        return lambda i, s, pt: (layer, pt[i, jnp.minimum(s, n_steps - 1) * g_pages + g], 0, 0)

    def attend_page(g):
        return lambda i, s, pt: (layer, pt[i, jnp.clip(s - n_steps - 1, 0, n_steps - 1) * g_pages + g], 0, 0, 0)

    page_blk = (1, 1, N_HEADS, HEAD_DIM, page)
    grid_spec = pltpu.PrefetchScalarGridSpec(
        num_scalar_prefetch=1,
        grid=(b, 2 * n_steps + 2),
        in_specs=[pl.BlockSpec((1, STACK, D_IDX), seq3), pl.BlockSpec((1, STACK, LANES), seq3),
                  pl.BlockSpec((1, STACK, HEAD_DIM), seq3), pl.BlockSpec((1, D_IDX, page), seq3),
                  pl.BlockSpec((1, N_HEADS, HEAD_DIM, page), seq4), pl.BlockSpec((1, N_HEADS, HEAD_DIM, page), seq4)]
        + [pl.BlockSpec((1, 1, D_IDX, page), score_page(g)) for g in range(g_pages)]
        + [pl.BlockSpec(page_blk, attend_page(g)) for g in range(g_pages)] * 2,
        out_specs=pl.BlockSpec((1, STACK, HEAD_DIM), seq3),
        scratch_shapes=[
            pltpu.VMEM((n_pages + 1, SAMPLE_ROWS, LANES), I32),
            pltpu.VMEM((SAMPLE_ROWS, LANES), I32),
            pltpu.VMEM((SAMPLE_ROWS, LANES), I32),
            pltpu.VMEM((STACK, 1), F32),
            pltpu.VMEM((STACK, 1), F32),
            pltpu.VMEM((STACK, HEAD_DIM), F32),
        ],
    )
    return pl.pallas_call(
        kern, grid_spec=grid_spec, out_shape=jax.ShapeDtypeStruct((b, STACK, HEAD_DIM), BF16),
        compiler_params=_cparams(("arbitrary", "arbitrary")), name="dsa_sample",
    )(page_table, qis, w_rep, qa, ki_new, ka_new, va_new,
      *([cache_ik] * g_pages), *([cache_ak] * g_pages), *([cache_av] * g_pages))


def _sb_stack_block(q, k_t, v_t, valid, tri, carry, acc):
    z = jnp.concatenate([jnp.dot(q[h * SAMPLE_ROWS:(h + 1) * SAMPLE_ROWS], k_t[h], preferred_element_type=F32)
                         for h in range(N_HEADS)], axis=0)
    lsm = _log_sigmoid_neg(z)
    lg = lsm if valid is None else jnp.where(valid, lsm, 0.0)
    later = _sum_of_later(lg, tri)
    a = jnp.exp(lsm + z + later + carry)
    if valid is not None:
        a = jnp.where(valid, a, 0.0)
    pv = jnp.concatenate(
        [lax.dot_general(a[h * SAMPLE_ROWS:(h + 1) * SAMPLE_ROWS].astype(BF16), v_t[h],
                         (((1,), (1,)), ((), ())), preferred_element_type=F32) for h in range(N_HEADS)], axis=0)
    return carry + jnp.sum(lg, axis=1, keepdims=True), acc + pv


def _page_heads(ref):
    return [ref[0, 0, h].astype(BF16) for h in range(N_HEADS)]


def _sb_sample_near_kernel(pt_s, q_ref, kn_ref, vn_ref, tri_ref, *rest, n_near, n_new):
    kp = rest[:n_near]
    vp = rest[n_near:2 * n_near]
    acc_ref, carry_ref = rest[2 * n_near:]
    q = q_ref[0]
    tri = tri_ref[...]
    row = lax.broadcasted_iota(I32, (STACK, LANES), 0)
    lane = lax.broadcasted_iota(I32, (STACK, LANES), 1)
    valid = lane < jnp.minimum(row % SAMPLE_ROWS, n_new - 1)
    carry, acc = _sb_stack_block(q, [kn_ref[0, h] for h in range(N_HEADS)], [vn_ref[0, h] for h in range(N_HEADS)],
                                 valid, tri, jnp.zeros((STACK, 1), F32), jnp.zeros((STACK, HEAD_DIM), F32))
    for g in range(n_near):
        carry, acc = _sb_stack_block(q, _page_heads(kp[g]), _page_heads(vp[g]), None, tri, carry, acc)
    acc_ref[0] = acc
    carry_ref[0] = jnp.broadcast_to(carry, (STACK, LANES))


def _sb_sample_far_kernel(pt_s, q_ref, tri_ref, kp_ref, vp_ref, acc_in_ref, carry_in_ref, acc_ref, carry_ref,
                          done_ref):
    s = pl.program_id(1)

    @pl.when(s == 0)
    def _():
        acc_ref[...] = acc_in_ref[...]
        carry_ref[...] = carry_in_ref[...]
        done_ref[0] = (jnp.max(carry_in_ref[...]) < SB_DEAD).astype(I32)

    @pl.when(done_ref[0] == 0)
    def _():
        carry, acc = _sb_stack_block(q_ref[0], _page_heads(kp_ref), _page_heads(vp_ref), None, tri_ref[...],
                                     carry_ref[0][:, 0:1], acc_ref[0])
        acc_ref[0] = acc
        carry_ref[0] = jnp.broadcast_to(carry, (STACK, LANES))
        done_ref[0] = (jnp.max(carry) < SB_DEAD).astype(I32)


def _sb_sample(page_table, layer, qb, kb_new, vb_new, cache_bk, cache_bv, *, n_new, n_near):
    b, n_pages = page_table.shape
    page = cache_bk.shape[-1]
    n_near = min(n_near, n_pages)
    tri = _later_matrix(page)
    seq3 = lambda i, *_: (i, 0, 0)
    seq4 = lambda i, *_: (i, 0, 0, 0)
    const2 = lambda *_: (0, 0)
    page_blk = (1, 1, N_HEADS, HEAD_DIM, page)
    new_blk = (1, N_HEADS, HEAD_DIM, page)
    state_shapes = [jax.ShapeDtypeStruct((b, STACK, HEAD_DIM), F32), jax.ShapeDtypeStruct((b, STACK, LANES), F32)]
    state_specs = [pl.BlockSpec((1, STACK, HEAD_DIM), seq3), pl.BlockSpec((1, STACK, LANES), seq3)]

    def near_page(g):
        return lambda i, pt: (layer, pt[i, n_pages - 1 - g], 0, 0, 0)

    near_spec = pltpu.PrefetchScalarGridSpec(
        num_scalar_prefetch=1, grid=(b,),
        in_specs=[pl.BlockSpec((1, STACK, HEAD_DIM), seq3), pl.BlockSpec(new_blk, seq4), pl.BlockSpec(new_blk, seq4),
                  pl.BlockSpec((page, page), const2)]
        + [pl.BlockSpec(page_blk, near_page(g)) for g in range(n_near)] * 2,
        out_specs=state_specs)
    acc, carry = pl.pallas_call(
        functools.partial(_sb_sample_near_kernel, n_near=n_near, n_new=n_new),
        grid_spec=near_spec, out_shape=state_shapes,
        compiler_params=_cparams(("arbitrary",)), name="sb_sample_near",
    )(page_table, qb, kb_new, vb_new, tri, *([cache_bk] * n_near), *([cache_bv] * n_near))
    n_far = n_pages - n_near
    if n_far == 0:
        return acc

    far_page = lambda i, s, pt: (layer, pt[i, n_far - 1 - s], 0, 0, 0)
    far_spec = pltpu.PrefetchScalarGridSpec(
        num_scalar_prefetch=1, grid=(b, n_far),
        in_specs=[pl.BlockSpec((1, STACK, HEAD_DIM), seq3), pl.BlockSpec((page, page), const2),
                  pl.BlockSpec(page_blk, far_page), pl.BlockSpec(page_blk, far_page)] + state_specs,
        out_specs=state_specs, scratch_shapes=[pltpu.SMEM((1,), I32)])

    def far(a, c):
        return pl.pallas_call(
            _sb_sample_far_kernel, grid_spec=far_spec, out_shape=state_shapes,
            compiler_params=_cparams(("arbitrary", "arbitrary")), name="sb_sample_far",
        )(page_table, qb, tri, cache_bk, cache_bv, a, c)[0]

    return lax.cond(jnp.max(carry) >= SB_DEAD, far, lambda a, c: a, acc, carry)


def _layer_norm(y, g, b):
    mu = jnp.mean(y, axis=1, keepdims=True)
    d = y - mu
    var = jnp.mean(d * d, axis=1, keepdims=True)
    return d * lax.rsqrt(var + LN_EPS) * g + b


def _merge_kernel(oa_ref, ob_ref, sga_ref, sgb_ref, x_ref, wba_ref, wbb_ref, wo_ref, g_ref, b_ref, h_ref, h16_ref,
                  *, alpha):
    oa = jnp.dot(oa_ref[...].astype(BF16), wba_ref[...], preferred_element_type=F32)
    ob = jnp.dot(ob_ref[...].astype(BF16), wbb_ref[...], preferred_element_type=F32)
    mix = sga_ref[...].astype(F32) * oa + sgb_ref[...].astype(F32) * ob
    y = alpha * x_ref[...] + jnp.dot(mix.astype(BF16), wo_ref[...], preferred_element_type=F32)
    h = _layer_norm(y, g_ref[...], b_ref[...])
    h_ref[...] = h
    h16_ref[...] = h.astype(BF16)


def _merge(oa, ob, sga, sgb, x2d, wba, wbb, wo, g, b, *, alpha, tm):
    t = x2d.shape[0]
    rspec = lambda n: pl.BlockSpec((tm, n), lambda i: (i, 0))
    return pl.pallas_call(
        functools.partial(_merge_kernel, alpha=alpha),
        grid=(t // tm,),
        in_specs=[rspec(W_ATT), rspec(W_ATT), rspec(D_MODEL), rspec(D_MODEL), rspec(D_MODEL),
                  _resident((W_ATT, D_MODEL)), _resident((W_ATT, D_MODEL)), _resident((D_MODEL, D_MODEL)),
                  _resident((1, D_MODEL)), _resident((1, D_MODEL))],
        out_specs=[rspec(D_MODEL), rspec(D_MODEL)],
        out_shape=[jax.ShapeDtypeStruct((t, D_MODEL), F32), jax.ShapeDtypeStruct((t, D_MODEL), BF16)],
        compiler_params=_cparams(("parallel",)), name="merge_ln1",
    )(oa, ob, sga, sgb, x2d, wba, wbb, wo, g, b)


def _top_rows(x, n):
    out = []
    for _ in range(n):
        m = jnp.max(x, axis=0, keepdims=True)
        out.append(m)
        x = jnp.where(x == m, -jnp.inf, x)
    return out


def _peer_keys_kernel(ht_ref, wq_ref, wk_ref, s1_ref, s2_ref, e1_ref, e2_ref, thr_ref):
    qt = jnp.dot(wq_ref[...], ht_ref[...], preferred_element_type=F32)
    st = jnp.dot(wk_ref[...], qt.astype(BF16), preferred_element_type=F32)
    for h in range(PEER_HEADS):
        s1 = st[h * 2 * PEER_NKEYS:h * 2 * PEER_NKEYS + PEER_NKEYS]
        s2 = st[h * 2 * PEER_NKEYS + PEER_NKEYS:(h + 1) * 2 * PEER_NKEYS]
        t1 = _top_rows(s1, PEER_TOPK)
        t2 = jnp.concatenate(_top_rows(s2, PEER_TOPK), axis=0)
        cand = jnp.concatenate([t1[a] + t2 for a in range(PEER_TOPK)], axis=0)
        cmax = t1[0] + t2[0:1]
        cur = cand
        tot = jnp.zeros_like(cmax)
        thr = cmax
        for _ in range(PEER_TOPK):
            m = jnp.max(cur, axis=0, keepdims=True)
            hit = cur == m
            tot_new = tot + jnp.sum(jnp.where(hit, 1.0, 0.0), axis=0, keepdims=True)
            thr = jnp.where(jnp.logical_and(tot < PEER_TOPK, tot_new >= PEER_TOPK), m, thr)
            tot = tot_new
            cur = jnp.where(hit, -jnp.inf, cur)
        z = jnp.sum(jnp.where(cand >= thr, jnp.exp(cand - cmax), 0.0), axis=0, keepdims=True)
        s1_ref[h] = s1
        s2_ref[h] = s2
        e1_ref[h] = jnp.exp(s1 - t1[0]) / z
        e2_ref[h] = jnp.exp(s2 - t2[0:1])
        thr_ref[h:h + 1, :] = thr


def _peer_keys(ht16, wq_t, wk_t, *, tn):
    t = ht16.shape[1]
    big = jax.ShapeDtypeStruct((PEER_HEADS, PEER_NKEYS, t), F32)
    bspec = pl.BlockSpec((PEER_HEADS, PEER_NKEYS, tn), lambda i: (0, 0, i))
    return pl.pallas_call(
        _peer_keys_kernel,
        grid=(t // tn,),
        in_specs=[pl.BlockSpec((D_MODEL, tn), lambda i: (0, i)), _resident(wq_t.shape), _resident(wk_t.shape)],
        out_specs=[bspec, bspec, bspec, bspec, pl.BlockSpec((PEER_HEADS, tn), lambda i: (0, i))],
        out_shape=[big, big, big, big, jax.ShapeDtypeStruct((PEER_HEADS, t), F32)],
        compiler_params=_cparams(("parallel",)), name="peer_keys",
    )(ht16, wq_t, wk_t)


def _gelu_tanh(x):
    return 0.5 * x * (1.0 + jnp.tanh(math.sqrt(2.0 / math.pi) * (x + 0.044715 * (x * x * x))))


def _peer_mix_kernel(ht_ref, u_ref, vt_ref, s1_ref, s2_ref, e1_ref, e2_ref, thr_ref, ft_ref, g_ref, *, i_per_chunk):
    c = pl.program_id(1)

    @pl.when(c == 0)
    def _():
        ft_ref[...] = jnp.zeros(ft_ref.shape, F32)

    act = _gelu_tanh(jnp.dot(u_ref[...], ht_ref[...], preferred_element_type=F32))
    for ii in range(i_per_chunk):
        i = c * i_per_chunk + ii
        w = None
        for h in range(PEER_HEADS):
            s1_row = s1_ref[h, pl.ds(i, 1), :]
            e1_row = e1_ref[h, pl.ds(i, 1), :]
            gate = jnp.where(s1_row + s2_ref[h] >= thr_ref[h:h + 1, :], e1_row * e2_ref[h], 0.0)
            w = gate if w is None else w + gate
        rs = slice(ii * PEER_NKEYS, (ii + 1) * PEER_NKEYS)
        g_ref[rs, :] = (w * act[rs, :]).astype(BF16)
    ft_ref[...] += jnp.dot(vt_ref[...], g_ref[...], preferred_element_type=F32)


def _peer_mix(ht16, u16, vt16, s1, s2, e1, e2, thr, *, tn, i_per_chunk):
    t = ht16.shape[1]
    ce = i_per_chunk * PEER_NKEYS
    n_exp = u16.shape[0]
    bspec = pl.BlockSpec((PEER_HEADS, PEER_NKEYS, tn), lambda i, c: (0, 0, i))
    return pl.pallas_call(
        functools.partial(_peer_mix_kernel, i_per_chunk=i_per_chunk),
        grid=(t // tn, n_exp // ce),
        in_specs=[pl.BlockSpec((D_MODEL, tn), lambda i, c: (0, i)),
                  pl.BlockSpec((ce, D_MODEL), lambda i, c: (c, 0)),
                  pl.BlockSpec((D_MODEL, ce), lambda i, c: (0, c)),
                  bspec, bspec, bspec, bspec,
                  pl.BlockSpec((PEER_HEADS, tn), lambda i, c: (0, i))],
        out_specs=pl.BlockSpec((D_MODEL, tn), lambda i, c: (0, i)),
        out_shape=jax.ShapeDtypeStruct((D_MODEL, t), F32),
        scratch_shapes=[pltpu.VMEM((ce, tn), BF16)],
        compiler_params=_cparams(("parallel", "arbitrary")), name="peer_mix",
    )(ht16, u16, vt16, s1, s2, e1, e2, thr)


def _final_kernel(h_ref, h16_ref, f_ref, p_ref, wg_ref, wp_ref, g_ref, b_ref, o_ref, *, alpha):
    gate = jax.nn.sigmoid(jnp.dot(h16_ref[...], wg_ref[...], preferred_element_type=F32))
    e = gate * jnp.dot(p_ref[...].astype(BF16), wp_ref[...], preferred_element_type=F32)
    o_ref[...] = _layer_norm(alpha * h_ref[...] + f_ref[...] + e, g_ref[...], b_ref[...])


def _final(h, h16, f, p2d, wg, wp, g, b, *, alpha, tm):
    t = h.shape[0]
    rspec = lambda n: pl.BlockSpec((tm, n), lambda i: (i, 0))
    return pl.pallas_call(
        functools.partial(_final_kernel, alpha=alpha),
        grid=(t // tm,),
        in_specs=[rspec(D_MODEL), rspec(D_MODEL), rspec(D_MODEL), rspec(PLE_DIM),
                  _resident((D_MODEL, D_MODEL)), _resident((PLE_DIM, D_MODEL)),
                  _resident((1, D_MODEL)), _resident((1, D_MODEL))],
        out_specs=rspec(D_MODEL),
        out_shape=jax.ShapeDtypeStruct((t, D_MODEL), F32),
        compiler_params=_cparams(("parallel",)), name="ple_ln2",
    )(h, h16, f, p2d, wg, wp, g, b)


def _pick(n, prefs):
    for c in prefs:
        if n % c == 0:
            return c
    return n


def _tiles(t):
    return dict(proj=_pick(t, (512, 256, 128)), tq=_pick(t, (256, 128)), tk=_pick(t, (512, 256, 128)),
                sb=_pick(t, (256, 128)), rows=_pick(t, (256, 128)), peer=_pick(t, (512, 256, 128)))


def _layer_weights(l, w_in, w_branch, w_out, ln1_g, ln1_b, w_pq, peer_sub_keys, peer_u, peer_v, ln2_g, ln2_b,
                   w_ple_gate, w_ple_proj):
    sk = peer_sub_keys[l]
    half = PEER_DQ // 2
    blk = jnp.zeros((2 * PEER_NKEYS, PEER_DQ), F32)
    blk = blk.at[:PEER_NKEYS, :half].set(sk[0]).at[PEER_NKEYS:, half:].set(sk[1])
    return dict(
        w_perm=_permute_w_in(w_in[l]),
        wba=w_branch[l, :W_ATT].astype(BF16), wbb=w_branch[l, W_ATT:].astype(BF16), wo=w_out[l].astype(BF16),
        g1=ln1_g[l][None], b1=ln1_b[l][None], g2=ln2_g[l][None], b2=ln2_b[l][None],
        wq_t=w_pq[l].T.astype(BF16),
        wk_t=jnp.kron(jnp.eye(PEER_HEADS, dtype=F32), blk).astype(BF16),
        u16=peer_u[l].astype(BF16), vt16=peer_v[l].T.astype(BF16),
        wg=w_ple_gate[l].astype(BF16), wp=w_ple_proj[l].astype(BF16),
    )


def _token_tail(x2d, p2d, oa, ob, sga, sgb, lw, alpha):
    tl = _tiles(x2d.shape[0])
    h, h16 = _merge(oa, ob, sga, sgb, x2d, lw["wba"], lw["wbb"], lw["wo"], lw["g1"], lw["b1"], alpha=alpha,
                    tm=tl["rows"])
    ht16 = h16.T
    s1, s2, e1, e2, thr = _peer_keys(ht16, lw["wq_t"], lw["wk_t"], tn=tl["peer"])
    ft = _peer_mix(ht16, lw["u16"], lw["vt16"], s1, s2, e1, e2, thr, tn=tl["peer"], i_per_chunk=8)
    return _final(h, h16, ft.T, p2d, lw["wg"], lw["wp"], lw["g2"], lw["b2"], alpha=alpha, tm=tl["rows"])


def _stack_heads(a_hm, dec_b):
    h, _, d = a_hm.shape
    return a_hm.reshape(h, dec_b, SAMPLE_ROWS, d).transpose(1, 0, 2, 3).reshape(dec_b, h * SAMPLE_ROWS, d)


def _unstack_heads(o, dec_b):
    return o.reshape(dec_b, N_HEADS, SAMPLE_ROWS, HEAD_DIM).transpose(0, 2, 1, 3).reshape(dec_b * SAMPLE_ROWS, W_ATT)


def _new_page(a_hm, dec_b, page):
    h, _, d = a_hm.shape
    a = a_hm.reshape(h, dec_b, SAMPLE_ROWS, d).transpose(1, 0, 3, 2)
    return jnp.pad(a, ((0, 0), (0, 0), (0, 0), (0, page - SAMPLE_ROWS)))


def kernel(x_prompt, x_sample, p_prompt, p_sample, cache_a_k, cache_a_v, cache_idx_k, cache_b_k, cache_b_v,
           page_table, w_in, w_branch, w_out, ln1_g, ln1_b, w_pq, peer_sub_keys, peer_u, peer_v, ln2_g, ln2_b,
           w_ple_gate, w_ple_proj):
    depth = w_in.shape[0]
    alpha = (2.0 * depth) ** 0.25
    n_batch, seq, _ = x_prompt.shape
    dec_b, dec_t, _ = x_sample.shape
    page = cache_a_k.shape[2]
    n_pages = page_table.shape[1]
    n_past = n_pages * page
    rows = SAMPLE_ROWS
    assert dec_t <= rows and page == LANES and cache_a_k.shape[3:] == (N_HEADS, HEAD_DIM)

    slot_minor = lambda c: jnp.transpose(c, (0, 1, 3, 4, 2))
    cak, cav, cbk, cbv = (slot_minor(c) for c in (cache_a_k, cache_a_v, cache_b_k, cache_b_v))
    cik = jnp.transpose(cache_idx_k, (0, 1, 3, 2))

    pos_p = jnp.arange(seq, dtype=I32)
    pos_row = n_past + jnp.minimum(jnp.arange(rows, dtype=I32), dec_t - 1)
    pos_s = jnp.tile(pos_row, dec_b)
    g_pages = _pick(n_pages, (8, 4, 2, 1))

    xp = x_prompt
    xs = jnp.pad(x_sample, ((0, 0), (0, rows - dec_t), (0, 0))).reshape(dec_b * rows, D_MODEL)
    shapes = ((N_HEADS, HEAD_DIM), (N_HEADS, HEAD_DIM), (D_IDX,), (N_HEADS, HEAD_DIM), (N_HEADS, HEAD_DIM))
    outs_p = [[] for _ in range(5)]
    outs_s = [[] for _ in range(5)]
    for l in range(depth):
        lw = _layer_weights(l, w_in, w_branch, w_out, ln1_g, ln1_b, w_pq, peer_sub_keys, peer_u, peer_v, ln2_g,
                            ln2_b, w_ple_gate, w_ple_proj)
        tl = _tiles(seq)
        new_xp = []
        per_b = [[] for _ in range(5)]
        for b in range(n_batch):
            x2d = xp[b]
            (qa, ka, ka16, va, va16, qis, ki, ki16, wi, qb, kb, kb16, vb, vb16, sga, sgb) = _project(
                x2d, lw["w_perm"], pos_p, tl["proj"])
            oa_t = _dsa_prompt(jnp.swapaxes(qis, 1, 2), wi.T, ki16, jnp.swapaxes(qa, 1, 2), ka16,
                               jnp.swapaxes(va16, 1, 2), tq=tl["tq"], tk=tl["tk"])
            ob = _sb_prompt(qb, kb16, vb16, tq=tl["sb"], tk=tl["sb"], near_blocks=3)
            new_xp.append(_token_tail(x2d, p_prompt[l, b], oa_t.T, ob, sga, sgb, lw, alpha))
            for dst, a in zip(per_b, (ka, va, ki, kb, vb)):
                dst.append(a)
        xp = jnp.stack(new_xp)
        for dst, a, sh in zip(outs_p, per_b, shapes):
            dst.append(jnp.stack(a).reshape((n_batch, seq) + sh))

        (qa, ka, ka16, va, va16, qis, ki, ki16, wi, qb, kb, kb16, vb, vb16, sga, sgb) = _project(
            xs, lw["w_perm"], pos_s, _tiles(dec_b * rows)["proj"])
        w_rep = jnp.broadcast_to(
            wi.reshape(dec_b, rows, H_IDX).transpose(0, 2, 1).reshape(dec_b, STACK, 1), (dec_b, STACK, LANES))
        ki_new = jnp.pad(ki16.reshape(dec_b, rows, D_IDX).transpose(0, 2, 1), ((0, 0), (0, 0), (0, page - rows)))
        oa = _dsa_sample(page_table, l, _stack_heads(qis, dec_b), w_rep, _stack_heads(qa, dec_b), ki_new,
                         _new_page(ka16, dec_b, page), _new_page(va16, dec_b, page), cik, cak, cav, n_new=dec_t,
                         g_pages=g_pages)
        ob = _sb_sample(page_table, l, _stack_heads(qb, dec_b), _new_page(kb16, dec_b, page),
                        _new_page(vb16, dec_b, page), cbk, cbv, n_new=dec_t, n_near=4)
        ps = jnp.pad(p_sample[l], ((0, 0), (0, rows - dec_t), (0, 0))).reshape(dec_b * rows, PLE_DIM)
        xs = _token_tail(xs, ps, _unstack_heads(oa, dec_b), _unstack_heads(ob, dec_b), sga, sgb, lw, alpha)
        for dst, a, sh in zip(outs_s, (ka, va, ki, kb, vb), shapes):
            dst.append(a.reshape((dec_b, rows) + sh)[:, :dec_t])

    y_sample = xs.reshape(dec_b, rows, D_MODEL)[:, :dec_t]
    return (xp, y_sample) + tuple(jnp.stack(o) for o in outs_p) + tuple(jnp.stack(o) for o in outs_s)
```

```python
import functools
import math

import numpy as np
import jax
import jax.numpy as jnp
from jax import lax
from jax.experimental import pallas as pl
from jax.experimental.pallas import tpu as pltpu

F32 = jnp.float32
BF16 = jnp.bfloat16
I32 = jnp.int32

D_MODEL = 1024
HEAD_DIM = 64
N_HEADS = 8
W_ATT = N_HEADS * HEAD_DIM
ROPE_THETA = 500000.0
ROT_64 = 16
ROT_32 = 8
H_IDX = 8
D_IDX = 32
TOPK_MAX = 256
PEER_HEADS = 8
PEER_NKEYS = 128
PEER_DQ = 128
PEER_TOPK = 16
PLE_DIM = 256
LN_EPS = 1e-5
IN_SIZES = (W_ATT, W_ATT, W_ATT, H_IDX * D_IDX, D_IDX, H_IDX, W_ATT, W_ATT, W_ATT, D_MODEL, D_MODEL)

LANES = 128
SUBLANES = 8
VMEM_LIMIT_BYTES = 56 * 1024 * 1024
INT_MIN = -(2 ** 31)
NO_TIE_CUT = 2 ** 30
NEG_BIG = -1e30
M_INIT = -1e20
SB_DEAD = -104.0

C_QA, C_KA, C_VA, C_QI, C_KW = 0, 512, 1024, 1536, 1792
C_QB, C_KB, C_VB, C_GA, C_GB, N_COLS = 1920, 2432, 2944, 3456, 4480, 5504

SAMPLE_ROWS = SUBLANES
STACK = N_HEADS * SAMPLE_ROWS


def _cparams(sem):
    return pltpu.CompilerParams(dimension_semantics=sem, vmem_limit_bytes=VMEM_LIMIT_BYTES)


def _resident(shape):
    zeros = (0,) * len(shape)
    return pl.BlockSpec(shape, lambda *_: zeros, pipeline_mode=pl.Buffered(1))


def _rope_tables(pos):
    posf = pos.astype(F32)

    def cos_sin(r):
        half = r // 2
        inv = ROPE_THETA ** (-(jnp.arange(half, dtype=F32) * 2.0 / r))
        ang = posf[:, None] * inv[None, :]
        return jnp.cos(ang), jnp.sin(ang)

    def lanes(cos, sin, head_dim, half, n_rot_lanes):
        lane = np.arange(LANES)
        m = lane % head_dim
        first = (m < half) & (lane < n_rot_lanes)
        second = (m >= half) & (m < 2 * half) & (lane < n_rot_lanes)
        idx = np.where(first, m, np.where(second, m - half, 0))
        c = jnp.where(first | second, cos[:, idx], 1.0)
        sa = jnp.where(first, -sin[:, idx], 0.0)
        sb = jnp.where(second, sin[:, idx], 0.0)
        return [c, sa, sb]

    c64, s64 = cos_sin(ROT_64)
    c32, s32 = cos_sin(ROT_32)
    tabs = (lanes(c64, s64, HEAD_DIM, ROT_64 // 2, LANES) + lanes(c32, s32, D_IDX, ROT_32 // 2, LANES)
            + lanes(c32, s32, D_IDX, ROT_32 // 2, D_IDX))
    return jnp.stack(tabs).astype(F32)


def _proj_kernel(x_ref, w_ref, tab_ref, qa_ref, ka_ref, ka16_ref, va_ref, va16_ref, qi_ref, ki_ref, ki16_ref,
                 wi_ref, qb_ref, kb_ref, kb16_ref, vb_ref, vb16_ref, sga_ref, sgb_ref):
    xb = x_ref[...].astype(BF16)

    def mm(c0, n):
        return jnp.dot(xb, w_ref[:, c0:c0 + n], preferred_element_type=F32)

    def rope(z, k, sh):
        out = []
        for c in range(z.shape[1] // LANES):
            zc = z[:, c * LANES:(c + 1) * LANES]
            out.append(zc * tab_ref[3 * k] + pltpu.roll(zc, LANES - sh, 1) * tab_ref[3 * k + 1]
                       + pltpu.roll(zc, sh, 1) * tab_ref[3 * k + 2])
        return out[0] if len(out) == 1 else jnp.concatenate(out, axis=1)

    def heads_major(ref, z, width):
        for h in range(z.shape[1] // width):
            ref[h] = z[:, h * width:(h + 1) * width].astype(BF16)

    heads_major(qa_ref, rope(mm(C_QA, W_ATT), 0, ROT_64 // 2), HEAD_DIM)
    z = rope(mm(C_KA, W_ATT), 0, ROT_64 // 2)
    ka_ref[...] = z
    heads_major(ka16_ref, z, HEAD_DIM)
    z = mm(C_VA, W_ATT)
    va_ref[...] = z
    heads_major(va16_ref, z, HEAD_DIM)
    heads_major(qi_ref, rope(mm(C_QI, H_IDX * D_IDX), 1, ROT_32 // 2), D_IDX)
    r = rope(mm(C_KW, LANES), 2, ROT_32 // 2)
    ki_ref[...] = r[:, :D_IDX]
    ki16_ref[...] = r[:, :D_IDX].astype(BF16)
    wi_ref[...] = r[:, D_IDX:D_IDX + H_IDX]
    heads_major(qb_ref, mm(C_QB, W_ATT), HEAD_DIM)
    z = mm(C_KB, W_ATT)
    kb_ref[...] = z
    heads_major(kb16_ref, z, HEAD_DIM)
    z = mm(C_VB, W_ATT)
    vb_ref[...] = z
    heads_major(vb16_ref, z, HEAD_DIM)
    sga_ref[...] = jax.nn.sigmoid(mm(C_GA, D_MODEL)).astype(BF16)
    sgb_ref[...] = jax.nn.sigmoid(mm(C_GB, D_MODEL)).astype(BF16)


def _permute_w_in(w_in):
    qa, ka, va, qi, ki, wi, qb, kb, vb, ga, gb = jnp.split(w_in, np.cumsum(IN_SIZES)[:-1].tolist(), axis=1)
    scale = 1.0 / math.sqrt(HEAD_DIM)
    pad = jnp.zeros((D_MODEL, LANES - D_IDX - H_IDX), w_in.dtype)
    w = jnp.concatenate([qa * scale, ka, va, qi, ki, wi, pad, qb * scale, kb, vb, ga, gb], axis=1)
    assert w.shape[1] == N_COLS
    return w.astype(BF16)


def _project(x2d, w_perm, pos, tm):
    t = x2d.shape[0]
    tabs = _rope_tables(pos)
    rm = lambda n, dt: (jax.ShapeDtypeStruct((t, n), dt), pl.BlockSpec((tm, n), lambda i: (i, 0)))
    hm = lambda d: (jax.ShapeDtypeStruct((N_HEADS, t, d), BF16), pl.BlockSpec((N_HEADS, tm, d), lambda i: (0, i, 0)))
    outs = [hm(HEAD_DIM), rm(W_ATT, F32), hm(HEAD_DIM), rm(W_ATT, F32), hm(HEAD_DIM), hm(D_IDX), rm(D_IDX, F32),
            rm(D_IDX, BF16), rm(H_IDX, F32), hm(HEAD_DIM), rm(W_ATT, F32), hm(HEAD_DIM), rm(W_ATT, F32),
            hm(HEAD_DIM), rm(D_MODEL, BF16), rm(D_MODEL, BF16)]
    return pl.pallas_call(
        _proj_kernel,
        grid=(t // tm,),
        in_specs=[pl.BlockSpec((tm, D_MODEL), lambda i: (i, 0)), _resident((D_MODEL, N_COLS)),
                  pl.BlockSpec((9, tm, LANES), lambda i: (0, i, 0))],
        out_specs=[s for _, s in outs],
        out_shape=[s for s, _ in outs],
        compiler_params=_cparams(("parallel",)),
        name="proj_rope",
    )(x2d, w_perm, tabs)


def _sortable_key(score):
    b = lax.bitcast_convert_type(score, I32)
    return jnp.where(b < 0, b ^ jnp.int32(0x7FFFFFFF), b)


def _top_k_threshold(count_ge, k, n_keys, shape):
    kf = jnp.float32(k)
    c0 = count_ge(jnp.zeros(shape, I32))
    v0 = jnp.where(c0 >= kf, jnp.int32(0), jnp.int32(INT_MIN))
    cnt0 = jnp.where(c0 >= kf, c0, n_keys)

    def cond(st):
        t, _, cnt = st
        return jnp.logical_and(t < 31, jnp.max(jnp.abs(cnt - kf)) > 0.0)

    def body(st):
        t, v, cnt = st
        cand = v + lax.shift_left(jnp.int32(1), jnp.int32(30) - t)
        c = count_ge(cand)
        ok = c >= kf
        return t + 1, jnp.where(ok, cand, v), jnp.where(ok, c, cnt)

    _, v, cnt = lax.while_loop(cond, body, (jnp.int32(0), v0, cnt0))
    return v, cnt


def _tie_cut(count_eq_below, need, n_bits, shape):
    def bit_body(t, x):
        cand = x + lax.shift_left(jnp.int32(1), jnp.int32(n_bits - 1) - t)
        return jnp.where(count_eq_below(cand) < need, cand, x)

    return lax.fori_loop(0, n_bits, bit_body, jnp.zeros(shape, I32))


def _reduce_keys(x, op, width=64):
    n, w = x.shape
    if n % width == 0 and n > width:
        x = op(x.reshape(n // width, width, w), axis=0)
    return op(x, axis=0, keepdims=True)


def _selected(key, vt, jt, idx, on, off):
    return jnp.where(key > vt, on, jnp.where(key == vt, jnp.where(idx <= jt, on, off), off))


def _tri_pairs(t, tq, tk, reverse, j_window=None):
    qi, kj, first, last = [], [], [], []
    for i in range(t // tq):
        jd = ((i + 1) * tq - 1) // tk
        js = list(range(jd + 1))
        if j_window is not None:
            js = [j for j in js if j_window[0] <= jd - j < j_window[1]]
        if reverse:
            js = js[::-1]
        for n, j in enumerate(js):
            qi.append(i), kj.append(j), first.append(int(n == 0)), last.append(int(n == len(js) - 1))
    return tuple(np.asarray(a, np.int32) for a in (qi, kj, first, last))


def _dsa_prompt_kernel(qi_s, kj_s, first_s, last_s, qis_ref, wi_ref, ki_ref, qa_ref, ka_ref, va_ref, o_ref,
                       sc_ref, vthr_ref, jthr_ref, m_ref, l_ref, acc_ref, *, tq, tk, n_top, n_idx_bits):
    p = pl.program_id(0)
    i = qi_s[p]
    j = kj_s[p]
    q_idx = i * tq + lax.broadcasted_iota(I32, (tk, tq), 1)
    k_off = lax.broadcasted_iota(I32, (tk, tq), 0)

    @pl.when(first_s[p] == 1)
    def _():
        m_ref[...] = jnp.full(m_ref.shape, M_INIT, F32)
        l_ref[...] = jnp.zeros(l_ref.shape, F32)
        acc_ref[...] = jnp.zeros(acc_ref.shape, F32)
        n_chunks = ((i + 1) * tq + tk - 1) // tk
        w = wi_ref[...]

        def score_chunk(c, carry):
            k_blk = ki_ref[pl.ds(pl.multiple_of(c * tk, tk), tk), :]
            score = jnp.zeros((tk, tq), F32)
            for h in range(H_IDX):
                s = jnp.dot(k_blk, qis_ref[h], preferred_element_type=F32)
                score = score + w[h:h + 1, :] * jnp.maximum(s, 0.0)
            score = jnp.where(c * tk + k_off <= q_idx, score, -jnp.inf)
            sc_ref[c] = _sortable_key(score)
            return carry

        lax.fori_loop(0, n_chunks, score_chunk, 0)

        cw = min(tk, 64)

        def key_count(ones_fn):
            def body(c, acc):
                for r in range(tk // cw):
                    acc = acc + ones_fn(sc_ref[c, r * cw:(r + 1) * cw, :], c * tk + r * cw)
                return acc
            acc = lax.fori_loop(0, n_chunks, body, jnp.zeros((cw, tq), F32))
            return jnp.sum(acc, axis=0, keepdims=True)

        def count_ge(cand):
            cand_b = jnp.broadcast_to(cand, (cw, tq))
            return key_count(lambda blk, k0: jnp.where(blk >= cand_b, 1.0, 0.0))

        v, cnt = _top_k_threshold(count_ge, n_top, (n_chunks * tk).astype(F32), (1, tq))
        vthr_ref[...] = jnp.broadcast_to(v, (SUBLANES, tq))
        jthr_ref[...] = jnp.full((SUBLANES, tq), NO_TIE_CUT, I32)

        @pl.when(jnp.max(cnt) > n_top)
        def _():
            need = n_top - count_ge(v + 1)

            v_b = jnp.broadcast_to(v, (cw, tq))
            off = lax.broadcasted_iota(I32, (cw, tq), 0)

            def count_eq_below(x):
                x_b = jnp.broadcast_to(x, (cw, tq))
                return key_count(
                    lambda blk, k0: jnp.where(blk == v_b, jnp.where(k0 + off < x_b, 1.0, 0.0), 0.0))
            jthr_ref[...] = jnp.broadcast_to(_tie_cut(count_eq_below, need, n_idx_bits, (1, tq)), (SUBLANES, tq))

    k_idx = j * tk + k_off
    sel = jnp.where(k_idx <= q_idx,
                    _selected(sc_ref[j], vthr_ref[0:1, :], jthr_ref[0:1, :], k_idx, 0.0, NEG_BIG), NEG_BIG)
    m_all = m_ref[...]
    l_all = l_ref[...]
    m_out, l_out = [], []

    def logits(h):
        return jnp.dot(ka_ref[h], qa_ref[h], preferred_element_type=F32) + sel

    ahead = 2
    pending = [logits(h) for h in range(ahead)]
    for h in range(N_HEADS):
        if h % ahead == 0 and h + ahead < N_HEADS:
            pending += [logits(h + ahead + a) for a in range(ahead)]
        s = pending.pop(0)
        m_prev = m_all[h:h + 1, :]
        m_new = jnp.maximum(m_prev, _reduce_keys(s, jnp.max))
        pr = jnp.exp(s - m_new)
        alpha = jnp.exp(m_prev - m_new)
        l_out.append(alpha * l_all[h:h + 1, :] + _reduce_keys(pr, jnp.sum))
        acc_ref[h] = alpha * acc_ref[h] + jnp.dot(va_ref[h], pr.astype(BF16), preferred_element_type=F32)
        m_out.append(m_new)
    m_ref[...] = jnp.concatenate(m_out, axis=0)
    l_ref[...] = jnp.concatenate(l_out, axis=0)

    @pl.when(last_s[p] == 1)
    def _():
        for h in range(N_HEADS):
            o_ref[h * HEAD_DIM:(h + 1) * HEAD_DIM, :] = (acc_ref[h] / l_ref[h:h + 1, :]).astype(o_ref.dtype)


def _dsa_prompt(qis_t, wi_t, ki16, qa_t, ka16, va_t, *, tq, tk):
    t = ki16.shape[0]
    n_top = min(TOPK_MAX, t // 4)
    pairs = _tri_pairs(t, tq, tk, reverse=False)
    kern = functools.partial(_dsa_prompt_kernel, tq=tq, tk=tk, n_top=n_top,
                             n_idx_bits=max(1, int(math.ceil(math.log2(t)))))
    qlane = lambda p, qi, kj, fi, la: (0, 0, qi[p])
    grid_spec = pltpu.PrefetchScalarGridSpec(
        num_scalar_prefetch=4,
        grid=(int(pairs[0].shape[0]),),
        in_specs=[
            pl.BlockSpec((H_IDX, D_IDX, tq), qlane),
            pl.BlockSpec((H_IDX, tq), lambda p, qi, kj, fi, la: (0, qi[p])),
            pl.BlockSpec((t, D_IDX), lambda p, qi, kj, fi, la: (0, 0)),
            pl.BlockSpec((N_HEADS, HEAD_DIM, tq), qlane),
            pl.BlockSpec((N_HEADS, tk, HEAD_DIM), lambda p, qi, kj, fi, la: (0, kj[p], 0)),
            pl.BlockSpec((N_HEADS, HEAD_DIM, tk), lambda p, qi, kj, fi, la: (0, 0, kj[p])),
        ],
        out_specs=pl.BlockSpec((W_ATT, tq), lambda p, qi, kj, fi, la: (0, qi[p])),
        scratch_shapes=[
            pltpu.VMEM((t // tk, tk, tq), I32),
            pltpu.VMEM((SUBLANES, tq), I32),
            pltpu.VMEM((SUBLANES, tq), I32),
            pltpu.VMEM((N_HEADS, tq), F32),
            pltpu.VMEM((N_HEADS, tq), F32),
            pltpu.VMEM((N_HEADS, HEAD_DIM, tq), F32),
        ],
    )
    return pl.pallas_call(
        kern, grid_spec=grid_spec, out_shape=jax.ShapeDtypeStruct((W_ATT, t), BF16),
        compiler_params=_cparams(("arbitrary",)), name="dsa_prompt",
    )(*pairs, qis_t, wi_t, ki16, qa_t, ka16, va_t)


def _log_sigmoid_neg(z):
    return -(jnp.maximum(z, 0.0) + jnp.log1p(jnp.exp(-jnp.abs(z))))


def _sum_of_later(lg, tri):
    hi = lg.astype(BF16)
    lo = (lg - hi.astype(F32)).astype(BF16)
    return jnp.dot(hi, tri, preferred_element_type=F32) + jnp.dot(lo, tri, preferred_element_type=F32)


def _later_matrix(n):
    r = np.arange(n)
    return jnp.asarray((r[:, None] > r[None, :]).astype(np.float32)).astype(BF16)


def _sb_prompt_kernel(qi_s, kj_s, first_s, last_s, q_ref, k_ref, v_ref, tri_ref, *rest, tq, tk, resume):
    if resume:
        acc_in_ref, carry_in_ref, acc_out_ref, carry_out_ref, carry_ref, acc_ref, done_ref = rest
    else:
        acc_out_ref, carry_out_ref, carry_ref, acc_ref, done_ref = rest
    p = pl.program_id(0)
    i = qi_s[p]
    j = kj_s[p]

    @pl.when(first_s[p] == 1)
    def _():
        if resume:
            worst = None
            for h in range(N_HEADS):
                acc_ref[h] = acc_in_ref[:, h * HEAD_DIM:(h + 1) * HEAD_DIM]
                c = carry_in_ref[:, h:h + 1]
                carry_ref[h] = c
                worst = c if worst is None else jnp.maximum(worst, c)
            done_ref[0] = (jnp.max(worst) < SB_DEAD).astype(I32)
        else:
            carry_ref[...] = jnp.zeros(carry_ref.shape, F32)
            acc_ref[...] = jnp.zeros(acc_ref.shape, F32)
            done_ref[0] = 0

    @pl.when(done_ref[0] == 0)
    def _():
        row = i * tq + lax.broadcasted_iota(I32, (tq, tk), 0)
        col = j * tk + lax.broadcasted_iota(I32, (tq, tk), 1)
        mask = col < row
        tri = tri_ref[...]
        zs = [lax.dot_general(q_ref[h], k_ref[h], (((1,), (1,)), ((), ())), preferred_element_type=F32)
              for h in range(N_HEADS)]
        lsms = [_log_sigmoid_neg(z) for z in zs]
        lgs = [jnp.where(mask, lsm, 0.0) for lsm in lsms]
        laters = [_sum_of_later(lg, tri) for lg in lgs]
        worst = None
        for h in range(N_HEADS):
            lg, later = lgs[h], laters[h]
            carry = carry_ref[h]
            a = jnp.where(mask, jnp.exp(lsms[h] + zs[h] + later + carry), 0.0)
            acc_ref[h] = acc_ref[h] + jnp.dot(a.astype(BF16), v_ref[h], preferred_element_type=F32)
            carry = carry + later[:, 0:1] + lg[:, 0:1]
            carry_ref[h] = carry
            worst = carry if worst is None else jnp.maximum(worst, carry)
        done_ref[0] = (jnp.max(worst) < SB_DEAD).astype(I32)

    @pl.when(last_s[p] == 1)
    def _():
        for h in range(N_HEADS):
            acc_out_ref[:, h * HEAD_DIM:(h + 1) * HEAD_DIM] = acc_ref[h]
            carry_out_ref[:, h:h + 1] = carry_ref[h]


def _sb_prompt_call(pairs, qb, kb16, vb16, tri, state, *, tq, tk):
    t = qb.shape[1]
    resume = state is not None
    qmap = lambda p, qi, kj, fi, la: (0, qi[p], 0)
    kmap = lambda p, qi, kj, fi, la: (0, kj[p], 0)
    rmap = lambda p, qi, kj, fi, la: (qi[p], 0)
    in_specs = [pl.BlockSpec((N_HEADS, tq, HEAD_DIM), qmap), pl.BlockSpec((N_HEADS, tk, HEAD_DIM), kmap),
                pl.BlockSpec((N_HEADS, tk, HEAD_DIM), kmap), pl.BlockSpec((tk, tk), lambda p, *_: (0, 0))]
    args = [qb, kb16, vb16, tri]
    if resume:
        in_specs += [pl.BlockSpec((tq, W_ATT), rmap), pl.BlockSpec((tq, N_HEADS), rmap)]
        args += list(state)
    grid_spec = pltpu.PrefetchScalarGridSpec(
        num_scalar_prefetch=4,
        grid=(int(pairs[0].shape[0]),),
        in_specs=in_specs,
        out_specs=[pl.BlockSpec((tq, W_ATT), rmap), pl.BlockSpec((tq, N_HEADS), rmap)],
        scratch_shapes=[pltpu.VMEM((N_HEADS, tq, 1), F32), pltpu.VMEM((N_HEADS, tq, HEAD_DIM), F32),
                        pltpu.SMEM((1,), I32)],
    )
    return pl.pallas_call(
        functools.partial(_sb_prompt_kernel, tq=tq, tk=tk, resume=resume),
        grid_spec=grid_spec,
        out_shape=[jax.ShapeDtypeStruct((t, W_ATT), F32), jax.ShapeDtypeStruct((t, N_HEADS), F32)],
        input_output_aliases=({4 + 4: 0, 4 + 5: 1} if resume else {}),
        compiler_params=_cparams(("arbitrary",)), name="sb_prompt_far" if resume else "sb_prompt_near",
    )(*pairs, *args)


def _sb_prompt(qb, kb16, vb16, *, tq, tk, near_blocks):
    t = qb.shape[1]
    tri = _later_matrix(tk)
    near = _tri_pairs(t, tq, tk, True, (0, near_blocks))
    far = _tri_pairs(t, tq, tk, True, (near_blocks, t))
    acc, carry = _sb_prompt_call(near, qb, kb16, vb16, tri, None, tq=tq, tk=tk)
    if far[0].shape[0] == 0:
        return acc
    first_row = int(far[0][0]) * tq
    alive = jnp.max(carry[first_row:]) >= SB_DEAD
    return lax.cond(
        alive,
        lambda a, c: _sb_prompt_call(far, qb, kb16, vb16, tri, (a, c), tq=tq, tk=tk)[0],
        lambda a, c: a, acc, carry)


def _dsa_sample_kernel(pt_s, qis_ref, w_ref, qa_ref, kin_ref, kan_ref, van_ref, *rest, g_pages, n_pages, n_new,
                       n_top, n_idx_bits):
    kip = rest[:g_pages]
    kap = rest[g_pages:2 * g_pages]
    vap = rest[2 * g_pages:3 * g_pages]
    o_ref, sc_ref, vthr_ref, jthr_ref, m_ref, l_ref, acc_ref = rest[3 * g_pages:]
    s = pl.program_id(1)
    n_steps = n_pages // g_pages
    n_blocks = n_pages + 1
    r_new = jnp.minimum(lax.broadcasted_iota(I32, (SAMPLE_ROWS, LANES), 0), n_new - 1)
    lane = lax.broadcasted_iota(I32, (SAMPLE_ROWS, LANES), 1)

    def scores(k_t):
        n = k_t.shape[1] // LANES
        s1 = jnp.dot(qis_ref[0], k_t, preferred_element_type=F32)
        wt = w_ref[0] if n == 1 else jnp.concatenate([w_ref[0]] * n, axis=1)
        t = wt * jnp.maximum(s1, 0.0)
        score = t[0:SAMPLE_ROWS]
        for h in range(1, H_IDX):
            score = score + t[h * SAMPLE_ROWS:(h + 1) * SAMPLE_ROWS]
        return score

    @pl.when(s < n_steps)
    def _():
        key = _sortable_key(scores(jnp.concatenate([r[0, 0].astype(BF16) for r in kip], axis=1)))
        for g in range(g_pages):
            sc_ref[s * g_pages + g] = key[:, g * LANES:(g + 1) * LANES]

    @pl.when(s == n_steps)
    def _():
        m_ref[...] = jnp.full(m_ref.shape, M_INIT, F32)
        l_ref[...] = jnp.zeros(l_ref.shape, F32)
        acc_ref[...] = jnp.zeros(acc_ref.shape, F32)
        key = _sortable_key(scores(kin_ref[0]))
        neg_inf_key = _sortable_key(jnp.full((SAMPLE_ROWS, LANES), -jnp.inf, F32))
        sc_ref[n_pages] = jnp.where(lane < n_new, jnp.where(lane <= r_new, key, neg_inf_key), jnp.int32(INT_MIN))
        keys = sc_ref[...]
        idx = (lax.broadcasted_iota(I32, (n_blocks, SAMPLE_ROWS, LANES), 0) * LANES
               + lax.broadcasted_iota(I32, (n_blocks, SAMPLE_ROWS, LANES), 2))

        def count(hit):
            ones = jnp.where(hit, 1.0, 0.0)
            parts = [jnp.sum(ones[b0:b0 + 16], axis=0) for b0 in range(0, n_blocks, 16)]
            while len(parts) > 1:
                parts = [sum(parts[k:k + 2]) for k in range(0, len(parts), 2)]
            return jnp.sum(parts[0], axis=1, keepdims=True)

        def count_ge(cand):
            return count(keys >= cand[None])

        v, cnt = _top_k_threshold(count_ge, n_top, jnp.float32(n_blocks * LANES), (SAMPLE_ROWS, 1))
        vthr_ref[...] = jnp.broadcast_to(v, (SAMPLE_ROWS, LANES))
        jthr_ref[...] = jnp.full((SAMPLE_ROWS, LANES), NO_TIE_CUT, I32)

        @pl.when(jnp.max(cnt) > n_top)
        def _():
            need = n_top - count_ge(v + 1)

            def count_eq_below(x):
                return count(jnp.logical_and(keys == v[None], idx < x[None]))
            jthr_ref[...] = jnp.broadcast_to(_tie_cut(count_eq_below, need, n_idx_bits, (SAMPLE_ROWS, 1)),
                                             (SAMPLE_ROWS, LANES))

    def chosen(blk_idx):
        return _selected(sc_ref[blk_idx], vthr_ref[...], jthr_ref[...], blk_idx * LANES + lane, 0.0, NEG_BIG)

    def attend(k_t, v_t, sel):
        q = qa_ref[0]
        sc = jnp.concatenate(
            [jnp.dot(q[h * SAMPLE_ROWS:(h + 1) * SAMPLE_ROWS], k_t[h], preferred_element_type=F32) + sel
             for h in range(N_HEADS)], axis=0)
        m_prev = m_ref[...]
        m_new = jnp.maximum(m_prev, jnp.max(sc, axis=1, keepdims=True))
        pr = jnp.exp(sc - m_new)
        alpha = jnp.exp(m_prev - m_new)
        l_ref[...] = alpha * l_ref[...] + jnp.sum(pr, axis=1, keepdims=True)
        pv = jnp.concatenate(
            [lax.dot_general(pr[h * SAMPLE_ROWS:(h + 1) * SAMPLE_ROWS].astype(BF16), v_t[h],
                             (((1,), (1,)), ((), ())), preferred_element_type=F32) for h in range(N_HEADS)], axis=0)
        acc_ref[...] = alpha * acc_ref[...] + pv
        m_ref[...] = m_new

    def head_pages(refs, h):
        return jnp.concatenate([r[0, 0, h].astype(BF16) for r in refs], axis=1)

    @pl.when(jnp.logical_and(s > n_steps, s <= 2 * n_steps))
    def _():
        p0 = (s - n_steps - 1) * g_pages
        attend([head_pages(kap, h) for h in range(N_HEADS)], [head_pages(vap, h) for h in range(N_HEADS)],
               jnp.concatenate([chosen(p0 + g) for g in range(g_pages)], axis=1))

    @pl.when(s == 2 * n_steps + 1)
    def _():
        sel = jnp.where(jnp.logical_and(lane <= r_new, lane < n_new), chosen(n_pages), NEG_BIG)
        attend([kan_ref[0, h] for h in range(N_HEADS)], [van_ref[0, h] for h in range(N_HEADS)], sel)
        o_ref[0] = (acc_ref[...] / l_ref[...]).astype(o_ref.dtype)


def _dsa_sample(page_table, layer, qis, w_rep, qa, ki_new, ka_new, va_new, cache_ik, cache_ak, cache_av, *,
                n_new, g_pages):
    b, n_pages = page_table.shape
    page = cache_ak.shape[-1]
    assert page == LANES and n_pages % g_pages == 0
    n_steps = n_pages // g_pages
    n_top = min(TOPK_MAX, (n_pages * page + n_new) // 4)
    kern = functools.partial(_dsa_sample_kernel, g_pages=g_pages, n_pages=n_pages, n_new=n_new, n_top=n_top,
                             n_idx_bits=max(1, int(math.ceil(math.log2((n_pages + 1) * page)))))
    seq3 = lambda i, s, pt: (i, 0, 0)
    seq4 = lambda i, s, pt: (i, 0, 0, 0)

    def score_page(g):
        return lambda i, s, pt: (layer, pt[i, jnp.minimum(s, n_steps - 1) * g_pages + g], 0, 0)

    def attend_page(g):
        return lambda i, s, pt: (layer, pt[i, jnp.clip(s - n_steps - 1, 0, n_steps - 1) * g_pages + g], 0, 0, 0)

    page_blk = (1, 1, N_HEADS, HEAD_DIM, page)
    grid_spec = pltpu.PrefetchScalarGridSpec(
        num_scalar_prefetch=1,
        grid=(b, 2 * n_steps + 2),
        in_specs=[pl.BlockSpec((1, STACK, D_IDX), seq3), pl.BlockSpec((1, STACK, LANES), seq3),
                  pl.BlockSpec((1, STACK, HEAD_DIM), seq3), pl.BlockSpec((1, D_IDX, page), seq3),
                  pl.BlockSpec((1, N_HEADS, HEAD_DIM, page), seq4), pl.BlockSpec((1, N_HEADS, HEAD_DIM, page), seq4)]
        + [pl.BlockSpec((1, 1, D_IDX, page), score_page(g)) for g in range(g_pages)]
        + [pl.BlockSpec(page_blk, attend_page(g)) for g in range(g_pages)] * 2,
        out_specs=pl.BlockSpec((1, STACK, HEAD_DIM), seq3),
        scratch_shapes=[
            pltpu.VMEM((n_pages + 1, SAMPLE_ROWS, LANES), I32),
            pltpu.VMEM((SAMPLE_ROWS, LANES), I32),
            pltpu.VMEM((SAMPLE_ROWS, LANES), I32),
            pltpu.VMEM((STACK, 1), F32),
            pltpu.VMEM((STACK, 1), F32),
            pltpu.VMEM((STACK, HEAD_DIM), F32),
        ],
    )
    return pl.pallas_call(
        kern, grid_spec=grid_spec, out_shape=jax.ShapeDtypeStruct((b, STACK, HEAD_DIM), BF16),
        compiler_params=_cparams(("arbitrary", "arbitrary")), name="dsa_sample",
    )(page_table, qis, w_rep, qa, ki_new, ka_new, va_new,
      *([cache_ik] * g_pages), *([cache_ak] * g_pages), *([cache_av] * g_pages))


def _sb_stack_block(q, k_t, v_t, valid, tri, carry, acc):
    z = jnp.concatenate([jnp.dot(q[h * SAMPLE_ROWS:(h + 1) * SAMPLE_ROWS], k_t[h], preferred_element_type=F32)
                         for h in range(N_HEADS)], axis=0)
    lsm = _log_sigmoid_neg(z)
    lg = lsm if valid is None else jnp.where(valid, lsm, 0.0)
    later = _sum_of_later(lg, tri)
    a = jnp.exp(lsm + z + later + carry)
    if valid is not None:
        a = jnp.where(valid, a, 0.0)
    pv = jnp.concatenate(
        [lax.dot_general(a[h * SAMPLE_ROWS:(h + 1) * SAMPLE_ROWS].astype(BF16), v_t[h],
                         (((1,), (1,)), ((), ())), preferred_element_type=F32) for h in range(N_HEADS)], axis=0)
    return carry + jnp.sum(lg, axis=1, keepdims=True), acc + pv


def _page_heads(ref):
    return [ref[0, 0, h].astype(BF16) for h in range(N_HEADS)]


def _sb_sample_near_kernel(pt_s, q_ref, kn_ref, vn_ref, tri_ref, *rest, n_near, n_new):
    kp = rest[:n_near]
    vp = rest[n_near:2 * n_near]
    acc_ref, carry_ref = rest[2 * n_near:]
    q = q_ref[0]
    tri = tri_ref[...]
    row = lax.broadcasted_iota(I32, (STACK, LANES), 0)
    lane = lax.broadcasted_iota(I32, (STACK, LANES), 1)
    valid = lane < jnp.minimum(row % SAMPLE_ROWS, n_new - 1)
    carry, acc = _sb_stack_block(q, [kn_ref[0, h] for h in range(N_HEADS)], [vn_ref[0, h] for h in range(N_HEADS)],
                                 valid, tri, jnp.zeros((STACK, 1), F32), jnp.zeros((STACK, HEAD_DIM), F32))
    for g in range(n_near):
        carry, acc = _sb_stack_block(q, _page_heads(kp[g]), _page_heads(vp[g]), None, tri, carry, acc)
    acc_ref[0] = acc
    carry_ref[0] = jnp.broadcast_to(carry, (STACK, LANES))


def _sb_sample_far_kernel(pt_s, q_ref, tri_ref, kp_ref, vp_ref, acc_in_ref, carry_in_ref, acc_ref, carry_ref,
                          done_ref):
    s = pl.program_id(1)

    @pl.when(s == 0)
    def _():
        acc_ref[...] = acc_in_ref[...]
        carry_ref[...] = carry_in_ref[...]
        done_ref[0] = (jnp.max(carry_in_ref[...]) < SB_DEAD).astype(I32)

    @pl.when(done_ref[0] == 0)
    def _():
        carry, acc = _sb_stack_block(q_ref[0], _page_heads(kp_ref), _page_heads(vp_ref), None, tri_ref[...],
                                     carry_ref[0][:, 0:1], acc_ref[0])
        acc_ref[0] = acc
        carry_ref[0] = jnp.broadcast_to(carry, (STACK, LANES))
        done_ref[0] = (jnp.max(carry) < SB_DEAD).astype(I32)


def _sb_sample(page_table, layer, qb, kb_new, vb_new, cache_bk, cache_bv, *, n_new, n_near):
    b, n_pages = page_table.shape
    page = cache_bk.shape[-1]
    n_near = min(n_near, n_pages)
    tri = _later_matrix(page)
    seq3 = lambda i, *_: (i, 0, 0)
    seq4 = lambda i, *_: (i, 0, 0, 0)
    const2 = lambda *_: (0, 0)
    page_blk = (1, 1, N_HEADS, HEAD_DIM, page)
    new_blk = (1, N_HEADS, HEAD_DIM, page)
    state_shapes = [jax.ShapeDtypeStruct((b, STACK, HEAD_DIM), F32), jax.ShapeDtypeStruct((b, STACK, LANES), F32)]
    state_specs = [pl.BlockSpec((1, STACK, HEAD_DIM), seq3), pl.BlockSpec((1, STACK, LANES), seq3)]

    def near_page(g):
        return lambda i, pt: (layer, pt[i, n_pages - 1 - g], 0, 0, 0)

    near_spec = pltpu.PrefetchScalarGridSpec(
        num_scalar_prefetch=1, grid=(b,),
        in_specs=[pl.BlockSpec((1, STACK, HEAD_DIM), seq3), pl.BlockSpec(new_blk, seq4), pl.BlockSpec(new_blk, seq4),
                  pl.BlockSpec((page, page), const2)]
        + [pl.BlockSpec(page_blk, near_page(g)) for g in range(n_near)] * 2,
        out_specs=state_specs)
    acc, carry = pl.pallas_call(
        functools.partial(_sb_sample_near_kernel, n_near=n_near, n_new=n_new),
        grid_spec=near_spec, out_shape=state_shapes,
        compiler_params=_cparams(("arbitrary",)), name="sb_sample_near",
    )(page_table, qb, kb_new, vb_new, tri, *([cache_bk] * n_near), *([cache_bv] * n_near))
    n_far = n_pages - n_near
    if n_far == 0:
        return acc

    far_page = lambda i, s, pt: (layer, pt[i, n_far - 1 - s], 0, 0, 0)
    far_spec = pltpu.PrefetchScalarGridSpec(
        num_scalar_prefetch=1, grid=(b, n_far),
        in_specs=[pl.BlockSpec((1, STACK, HEAD_DIM), seq3), pl.BlockSpec((page, page), const2),
                  pl.BlockSpec(page_blk, far_page), pl.BlockSpec(page_blk, far_page)] + state_specs,
        out_specs=state_specs, scratch_shapes=[pltpu.SMEM((1,), I32)])

    def far(a, c):
        return pl.pallas_call(
            _sb_sample_far_kernel, grid_spec=far_spec, out_shape=state_shapes,
            compiler_params=_cparams(("arbitrary", "arbitrary")), name="sb_sample_far",
        )(page_table, qb, tri, cache_bk, cache_bv, a, c)[0]

    return lax.cond(jnp.max(carry) >= SB_DEAD, far, lambda a, c: a, acc, carry)


def _layer_norm(y, g, b):
    mu = jnp.mean(y, axis=1, keepdims=True)
    d = y - mu
    var = jnp.mean(d * d, axis=1, keepdims=True)
    return d * lax.rsqrt(var + LN_EPS) * g + b


def _merge_kernel(oa_ref, ob_ref, sga_ref, sgb_ref, x_ref, wba_ref, wbb_ref, wo_ref, g_ref, b_ref, h_ref, h16_ref,
                  *, alpha):
    oa = jnp.dot(oa_ref[...].astype(BF16), wba_ref[...], preferred_element_type=F32)
    ob = jnp.dot(ob_ref[...].astype(BF16), wbb_ref[...], preferred_element_type=F32)
    mix = sga_ref[...].astype(F32) * oa + sgb_ref[...].astype(F32) * ob
    y = alpha * x_ref[...] + jnp.dot(mix.astype(BF16), wo_ref[...], preferred_element_type=F32)
    h = _layer_norm(y, g_ref[...], b_ref[...])
    h_ref[...] = h
    h16_ref[...] = h.astype(BF16)


def _merge(oa, ob, sga, sgb, x2d, wba, wbb, wo, g, b, *, alpha, tm):
    t = x2d.shape[0]
    rspec = lambda n: pl.BlockSpec((tm, n), lambda i: (i, 0))
    return pl.pallas_call(
        functools.partial(_merge_kernel, alpha=alpha),
        grid=(t // tm,),
        in_specs=[rspec(W_ATT), rspec(W_ATT), rspec(D_MODEL), rspec(D_MODEL), rspec(D_MODEL),
                  _resident((W_ATT, D_MODEL)), _resident((W_ATT, D_MODEL)), _resident((D_MODEL, D_MODEL)),
                  _resident((1, D_MODEL)), _resident((1, D_MODEL))],
        out_specs=[rspec(D_MODEL), rspec(D_MODEL)],
        out_shape=[jax.ShapeDtypeStruct((t, D_MODEL), F32), jax.ShapeDtypeStruct((t, D_MODEL), BF16)],
        compiler_params=_cparams(("parallel",)), name="merge_ln1",
    )(oa, ob, sga, sgb, x2d, wba, wbb, wo, g, b)


def _top_rows(x, n):
    out = []
    rank = jnp.full(x.shape, float(n), F32)
    for r in range(n):
        m = jnp.max(x, axis=0, keepdims=True)
        out.append(m)
        hit = x == m
        rank = jnp.where(hit, float(r), rank)
        x = jnp.where(hit, -jnp.inf, x)
    return out, rank


def _peer_keys_kernel(ht_ref, wq_ref, wk_ref, cnt_ref, e1_ref, rank2_ref, e2_ref):
    qt = jnp.dot(wq_ref[...], ht_ref[...], preferred_element_type=F32)
    st = jnp.dot(wk_ref[...], qt.astype(BF16), preferred_element_type=F32)
    for h in range(PEER_HEADS):
        s1 = st[h * 2 * PEER_NKEYS:h * 2 * PEER_NKEYS + PEER_NKEYS]
        s2 = st[h * 2 * PEER_NKEYS + PEER_NKEYS:(h + 1) * 2 * PEER_NKEYS]
        assert PEER_TOPK == 16
        t1, _ = _top_rows(s1, PEER_TOPK)
        t2_rows, rank2 = _top_rows(s2, PEER_TOPK)
        t2 = jnp.concatenate(t2_rows, axis=0)
        cand = jnp.concatenate(
            [t1[0] + t2, t1[1] + t2[0:8], t1[2] + t2[0:8], t1[3] + t2[0:8],
             t1[4] + t2[0:4], t1[5] + t2[0:4], t1[6] + t2[0:4], t1[7] + t2[0:4],
             jnp.concatenate(t1[8:], axis=0) + t2[0:1]], axis=0)
        cmax = t1[0] + t2[0:1]
        cur = cand
        tot = jnp.zeros_like(cmax)
        thr = cmax
        for _ in range(PEER_TOPK):
            m = jnp.max(cur, axis=0, keepdims=True)
            hit = cur == m
            tot_new = tot + jnp.sum(jnp.where(hit, 1.0, 0.0), axis=0, keepdims=True)
            thr = jnp.where(jnp.logical_and(tot < PEER_TOPK, tot_new >= PEER_TOPK), m, thr)
            tot = tot_new
            cur = jnp.where(hit, -jnp.inf, cur)
        z = jnp.sum(jnp.where(cand >= thr, jnp.exp(cand - cmax), 0.0), axis=0, keepdims=True)
        cnt = jnp.zeros_like(s1)
        for b in range(PEER_TOPK):
            cnt = cnt + jnp.where(s1 + t2_rows[b] >= thr, 1.0, 0.0)
        cnt_ref[h] = cnt
        e1_ref[h] = jnp.exp(s1 - t1[0]) / z
        rank2_ref[h] = rank2.astype(BF16)
        e2_ref[h] = jnp.exp(s2 - t2[0:1]).astype(BF16)


def _peer_keys(ht16, wq_t, wk_t, *, tn):
    t = ht16.shape[1]
    shape = lambda dt: jax.ShapeDtypeStruct((PEER_HEADS, PEER_NKEYS, t), dt)
    bspec = pl.BlockSpec((PEER_HEADS, PEER_NKEYS, tn), lambda i: (0, 0, i))
    return pl.pallas_call(
        _peer_keys_kernel,
        grid=(t // tn,),
        in_specs=[pl.BlockSpec((D_MODEL, tn), lambda i: (0, i)), _resident(wq_t.shape), _resident(wk_t.shape)],
        out_specs=[bspec, bspec, bspec, bspec],
        out_shape=[shape(F32), shape(F32), shape(BF16), shape(BF16)],
        compiler_params=_cparams(("parallel",)), name="peer_keys",
    )(ht16, wq_t, wk_t)


def _gelu_tanh(x):
    return 0.5 * x * (1.0 + jnp.tanh(math.sqrt(2.0 / math.pi) * (x + 0.044715 * (x * x * x))))


def _peer_mix_kernel(ht_ref, u_ref, vt_ref, cnt_ref, e1_ref, rank2_ref, e2_ref, ft_ref, g_ref, *, i_per_chunk):
    c = pl.program_id(1)
    tn = ht_ref.shape[1]
    pack = 2 * SUBLANES

    @pl.when(c == 0)
    def _():
        ft_ref[...] = jnp.zeros(ft_ref.shape, F32)

    def row_tile(ref, h, i):
        one = jnp.broadcast_to(ref[h, pl.ds(i, 1), :], (pack, tn)).astype(BF16)
        return jnp.concatenate([one] * (PEER_NKEYS // pack), axis=0)

    n_piece = 4
    ipp = i_per_chunk // n_piece
    rows = ipp * PEER_NKEYS
    ht = ht_ref[...]
    pre = [jnp.dot(u_ref[q * rows:(q + 1) * rows, :], ht, preferred_element_type=F32) for q in range(n_piece)]
    for q in range(n_piece):
        for ii in range(ipp):
            i = c * i_per_chunk + q * ipp + ii
            w = None
            for h in range(PEER_HEADS):
                gate = jnp.where(rank2_ref[h] < row_tile(cnt_ref, h, i), e2_ref[h] * row_tile(e1_ref, h, i),
                                 jnp.zeros((), BF16))
                w = gate if w is None else w + gate
            act = _gelu_tanh(pre[q][ii * PEER_NKEYS:(ii + 1) * PEER_NKEYS, :]).astype(BF16)
            g_ref[q * rows + ii * PEER_NKEYS:q * rows + (ii + 1) * PEER_NKEYS, :] = w * act
        ft_ref[...] += jnp.dot(vt_ref[:, q * rows:(q + 1) * rows], g_ref[q * rows:(q + 1) * rows, :],
                               preferred_element_type=F32)


def _peer_mix(ht16, u16, vt16, cnt, e1, rank2, e2, *, tn, i_per_chunk):
    t = ht16.shape[1]
    ce = i_per_chunk * PEER_NKEYS
    n_exp = u16.shape[0]
    bspec = pl.BlockSpec((PEER_HEADS, PEER_NKEYS, tn), lambda i, c: (0, 0, i))
    return pl.pallas_call(
        functools.partial(_peer_mix_kernel, i_per_chunk=i_per_chunk),
        grid=(t // tn, n_exp // ce),
        in_specs=[pl.BlockSpec((D_MODEL, tn), lambda i, c: (0, i)),
                  pl.BlockSpec((ce, D_MODEL), lambda i, c: (c, 0)),
                  pl.BlockSpec((D_MODEL, ce), lambda i, c: (0, c)),
                  bspec, bspec, bspec, bspec],
        out_specs=pl.BlockSpec((D_MODEL, tn), lambda i, c: (0, i)),
        out_shape=jax.ShapeDtypeStruct((D_MODEL, t), F32),
        scratch_shapes=[pltpu.VMEM((ce, tn), BF16)],
        compiler_params=_cparams(("parallel", "arbitrary")), name="peer_mix",
    )(ht16, u16, vt16, cnt, e1, rank2, e2)


def _final_kernel(h_ref, h16_ref, f_ref, p_ref, wg_ref, wp_ref, g_ref, b_ref, o_ref, *, alpha):
    gate = jax.nn.sigmoid(jnp.dot(h16_ref[...], wg_ref[...], preferred_element_type=F32))
    e = gate * jnp.dot(p_ref[...].astype(BF16), wp_ref[...], preferred_element_type=F32)
    o_ref[...] = _layer_norm(alpha * h_ref[...] + f_ref[...] + e, g_ref[...], b_ref[...])


def _final(h, h16, f, p2d, wg, wp, g, b, *, alpha, tm):
    t = h.shape[0]
    rspec = lambda n: pl.BlockSpec((tm, n), lambda i: (i, 0))
    return pl.pallas_call(
        functools.partial(_final_kernel, alpha=alpha),
        grid=(t // tm,),
        in_specs=[rspec(D_MODEL), rspec(D_MODEL), rspec(D_MODEL), rspec(PLE_DIM),
                  _resident((D_MODEL, D_MODEL)), _resident((PLE_DIM, D_MODEL)),
                  _resident((1, D_MODEL)), _resident((1, D_MODEL))],
        out_specs=rspec(D_MODEL),
        out_shape=jax.ShapeDtypeStruct((t, D_MODEL), F32),
        compiler_params=_cparams(("parallel",)), name="ple_ln2",
    )(h, h16, f, p2d, wg, wp, g, b)


def _pick(n, prefs):
    for c in prefs:
        if n % c == 0:
            return c
    return n


def _tiles(t):
    return dict(proj=_pick(t, (512, 256, 128)), tq=_pick(t, (256, 128)), tk=_pick(t, (512, 256, 128)),
                sb=_pick(t, (256, 128)), rows=_pick(t, (256, 128)), peer=_pick(t, (512, 256, 128)))


def _layer_weights(l, w_in, w_branch, w_out, ln1_g, ln1_b, w_pq, peer_sub_keys, peer_u, peer_v, ln2_g, ln2_b,
                   w_ple_gate, w_ple_proj):
    sk = peer_sub_keys[l]
    half = PEER_DQ // 2
    blk = jnp.zeros((2 * PEER_NKEYS, PEER_DQ), F32)
    blk = blk.at[:PEER_NKEYS, :half].set(sk[0]).at[PEER_NKEYS:, half:].set(sk[1])
    return dict(
        w_perm=_permute_w_in(w_in[l]),
        wba=w_branch[l, :W_ATT].astype(BF16), wbb=w_branch[l, W_ATT:].astype(BF16), wo=w_out[l].astype(BF16),
        g1=ln1_g[l][None], b1=ln1_b[l][None], g2=ln2_g[l][None], b2=ln2_b[l][None],
        wq_t=w_pq[l].T.astype(BF16),
        wk_t=jnp.kron(jnp.eye(PEER_HEADS, dtype=F32), blk).astype(BF16),
        u16=peer_u[l].astype(BF16), vt16=peer_v[l].T.astype(BF16),
        wg=w_ple_gate[l].astype(BF16), wp=w_ple_proj[l].astype(BF16),
    )


def _token_tail(x2d, p2d, oa, ob, sga, sgb, lw, alpha):
    tl = _tiles(x2d.shape[0])
    h, h16 = _merge(oa, ob, sga, sgb, x2d, lw["wba"], lw["wbb"], lw["wo"], lw["g1"], lw["b1"], alpha=alpha,
                    tm=tl["rows"])
    ht16 = h16.T
    cnt, e1, rank2, e2 = _peer_keys(ht16, lw["wq_t"], lw["wk_t"], tn=tl["peer"])
    ft = _peer_mix(ht16, lw["u16"], lw["vt16"], cnt, e1, rank2, e2, tn=tl["peer"], i_per_chunk=8)
    return _final(h, h16, ft.T, p2d, lw["wg"], lw["wp"], lw["g2"], lw["b2"], alpha=alpha, tm=tl["rows"])


def _stack_heads(a_hm, dec_b):
    h, _, d = a_hm.shape
    return a_hm.reshape(h, dec_b, SAMPLE_ROWS, d).transpose(1, 0, 2, 3).reshape(dec_b, h * SAMPLE_ROWS, d)


def _unstack_heads(o, dec_b):
    return o.reshape(dec_b, N_HEADS, SAMPLE_ROWS, HEAD_DIM).transpose(0, 2, 1, 3).reshape(dec_b * SAMPLE_ROWS, W_ATT)


def _new_page(a_hm, dec_b, page):
    h, _, d = a_hm.shape
    a = a_hm.reshape(h, dec_b, SAMPLE_ROWS, d).transpose(1, 0, 3, 2)
    return jnp.pad(a, ((0, 0), (0, 0), (0, 0), (0, page - SAMPLE_ROWS)))


def kernel(x_prompt, x_sample, p_prompt, p_sample, cache_a_k, cache_a_v, cache_idx_k, cache_b_k, cache_b_v,
           page_table, w_in, w_branch, w_out, ln1_g, ln1_b, w_pq, peer_sub_keys, peer_u, peer_v, ln2_g, ln2_b,
           w_ple_gate, w_ple_proj):
    depth = w_in.shape[0]
    alpha = (2.0 * depth) ** 0.25
    n_batch, seq, _ = x_prompt.shape
    dec_b, dec_t, _ = x_sample.shape
    page = cache_a_k.shape[2]
    n_pages = page_table.shape[1]
    n_past = n_pages * page
    rows = SAMPLE_ROWS
    assert dec_t <= rows and page == LANES and cache_a_k.shape[3:] == (N_HEADS, HEAD_DIM)

    slot_minor = lambda c: jnp.transpose(c, (0, 1, 3, 4, 2))
    cak, cav, cbk, cbv = (slot_minor(c) for c in (cache_a_k, cache_a_v, cache_b_k, cache_b_v))
    cik = jnp.transpose(cache_idx_k, (0, 1, 3, 2))

    pos_p = jnp.arange(seq, dtype=I32)
    pos_row = n_past + jnp.minimum(jnp.arange(rows, dtype=I32), dec_t - 1)
    pos_s = jnp.tile(pos_row, dec_b)
    g_pages = _pick(n_pages, (16, 8, 4, 2, 1))

    xp = x_prompt
    xs = jnp.pad(x_sample, ((0, 0), (0, rows - dec_t), (0, 0))).reshape(dec_b * rows, D_MODEL)
    shapes = ((N_HEADS, HEAD_DIM), (N_HEADS, HEAD_DIM), (D_IDX,), (N_HEADS, HEAD_DIM), (N_HEADS, HEAD_DIM))
    outs_p = [[] for _ in range(5)]
    outs_s = [[] for _ in range(5)]
    for l in range(depth):
        lw = _layer_weights(l, w_in, w_branch, w_out, ln1_g, ln1_b, w_pq, peer_sub_keys, peer_u, peer_v, ln2_g,
                            ln2_b, w_ple_gate, w_ple_proj)
        tl = _tiles(seq)
        new_xp = []
        per_b = [[] for _ in range(5)]
        for b in range(n_batch):
            x2d = xp[b]
            (qa, ka, ka16, va, va16, qis, ki, ki16, wi, qb, kb, kb16, vb, vb16, sga, sgb) = _project(
                x2d, lw["w_perm"], pos_p, tl["proj"])
            oa_t = _dsa_prompt(jnp.swapaxes(qis, 1, 2), wi.T, ki16, jnp.swapaxes(qa, 1, 2), ka16,
                               jnp.swapaxes(va16, 1, 2), tq=tl["tq"], tk=tl["tk"])
            ob = _sb_prompt(qb, kb16, vb16, tq=tl["sb"], tk=tl["sb"], near_blocks=3)
            new_xp.append(_token_tail(x2d, p_prompt[l, b], oa_t.T, ob, sga, sgb, lw, alpha))
            for dst, a in zip(per_b, (ka, va, ki, kb, vb)):
                dst.append(a)
        xp = jnp.stack(new_xp)
        for dst, a, sh in zip(outs_p, per_b, shapes):
            dst.append(jnp.stack(a).reshape((n_batch, seq) + sh))

        (qa, ka, ka16, va, va16, qis, ki, ki16, wi, qb, kb, kb16, vb, vb16, sga, sgb) = _project(
            xs, lw["w_perm"], pos_s, _tiles(dec_b * rows)["proj"])
        w_rep = jnp.broadcast_to(
            wi.reshape(dec_b, rows, H_IDX).transpose(0, 2, 1).reshape(dec_b, STACK, 1), (dec_b, STACK, LANES))
        ki_new = jnp.pad(ki16.reshape(dec_b, rows, D_IDX).transpose(0, 2, 1), ((0, 0), (0, 0), (0, page - rows)))
        oa = _dsa_sample(page_table, l, _stack_heads(qis, dec_b), w_rep, _stack_heads(qa, dec_b), ki_new,
                         _new_page(ka16, dec_b, page), _new_page(va16, dec_b, page), cik, cak, cav, n_new=dec_t,
                         g_pages=g_pages)
        ob = _sb_sample(page_table, l, _stack_heads(qb, dec_b), _new_page(kb16, dec_b, page),
                        _new_page(vb16, dec_b, page), cbk, cbv, n_new=dec_t, n_near=4)
        ps = jnp.pad(p_sample[l], ((0, 0), (0, rows - dec_t), (0, 0))).reshape(dec_b * rows, PLE_DIM)
        xs = _token_tail(xs, ps, _unstack_heads(oa, dec_b), _unstack_heads(ob, dec_b), sga, sgb, lw, alpha)
        for dst, a, sh in zip(outs_s, (ka, va, ki, kb, vb), shapes):
            dst.append(a.reshape((dec_b, rows) + sh)[:, :dec_t])

    y_sample = xs.reshape(dec_b, rows, D_MODEL)[:, :dec_t]
    return (xp, y_sample) + tuple(jnp.stack(o) for o in outs_p) + tuple(jnp.stack(o) for o in outs_s)
```

```python
import functools
import math

import numpy as np
import jax
import jax.numpy as jnp
from jax import lax
from jax.experimental import pallas as pl
from jax.experimental.pallas import tpu as pltpu

F32 = jnp.float32
BF16 = jnp.bfloat16
I32 = jnp.int32

D_MODEL = 1024
HEAD_DIM = 64
N_HEADS = 8
W_ATT = N_HEADS * HEAD_DIM
ROPE_THETA = 500000.0
ROT_64 = 16
ROT_32 = 8
H_IDX = 8
D_IDX = 32
TOPK_MAX = 256
PEER_HEADS = 8
PEER_NKEYS = 128
PEER_DQ = 128
PEER_TOPK = 16
PLE_DIM = 256
LN_EPS = 1e-5
IN_SIZES = (W_ATT, W_ATT, W_ATT, H_IDX * D_IDX, D_IDX, H_IDX, W_ATT, W_ATT, W_ATT, D_MODEL, D_MODEL)

LANES = 128
SUBLANES = 8
VMEM_LIMIT_BYTES = 56 * 1024 * 1024
INT_MIN = -(2 ** 31)
NO_TIE_CUT = 2 ** 30
NEG_BIG = -1e30
M_INIT = -1e20
SB_DEAD = -104.0
LOG2_E = 1.4426950408889634

C_QA, C_KA, C_VA, C_QI, C_KW = 0, 512, 1024, 1536, 1792
C_QB, C_KB, C_VB, C_GA, C_GB, N_COLS = 1920, 2432, 2944, 3456, 4480, 5504

SAMPLE_ROWS = SUBLANES
STACK = N_HEADS * SAMPLE_ROWS


def _cparams(sem):
    return pltpu.CompilerParams(dimension_semantics=sem, vmem_limit_bytes=VMEM_LIMIT_BYTES)


def _resident(shape):
    zeros = (0,) * len(shape)
    return pl.BlockSpec(shape, lambda *_: zeros, pipeline_mode=pl.Buffered(1))


def _rope_tables(pos):
    posf = pos.astype(F32)

    def cos_sin(r):
        half = r // 2
        inv = ROPE_THETA ** (-(jnp.arange(half, dtype=F32) * 2.0 / r))
        ang = posf[:, None] * inv[None, :]
        return jnp.cos(ang), jnp.sin(ang)

    def lanes(cos, sin, head_dim, half, n_rot_lanes):
        lane = np.arange(LANES)
        m = lane % head_dim
        first = (m < half) & (lane < n_rot_lanes)
        second = (m >= half) & (m < 2 * half) & (lane < n_rot_lanes)
        idx = np.where(first, m, np.where(second, m - half, 0))
        c = jnp.where(first | second, cos[:, idx], 1.0)
        sa = jnp.where(first, -sin[:, idx], 0.0)
        sb = jnp.where(second, sin[:, idx], 0.0)
        return [c, sa, sb]

    c64, s64 = cos_sin(ROT_64)
    c32, s32 = cos_sin(ROT_32)
    tabs = (lanes(c64, s64, HEAD_DIM, ROT_64 // 2, LANES) + lanes(c32, s32, D_IDX, ROT_32 // 2, LANES)
            + lanes(c32, s32, D_IDX, ROT_32 // 2, D_IDX))
    return jnp.stack(tabs).astype(F32)


def _proj_kernel(x_ref, w_ref, tab_ref, qa_ref, ka_ref, ka16_ref, va_ref, va16_ref, qi_ref, ki_ref, ki16_ref,
                 wi_ref, qb_ref, kb_ref, kb16_ref, vb_ref, vb16_ref, sga_ref, sgb_ref):
    xb = x_ref[...].astype(BF16)

    def mm(c0, n):
        return jnp.dot(xb, w_ref[:, c0:c0 + n], preferred_element_type=F32)

    def rope(z, k, sh):
        out = []
        for c in range(z.shape[1] // LANES):
            zc = z[:, c * LANES:(c + 1) * LANES]
            out.append(zc * tab_ref[3 * k] + pltpu.roll(zc, LANES - sh, 1) * tab_ref[3 * k + 1]
                       + pltpu.roll(zc, sh, 1) * tab_ref[3 * k + 2])
        return out[0] if len(out) == 1 else jnp.concatenate(out, axis=1)

    def heads_major(ref, z, width):
        for h in range(z.shape[1] // width):
            ref[h] = z[:, h * width:(h + 1) * width].astype(BF16)

    heads_major(qa_ref, rope(mm(C_QA, W_ATT), 0, ROT_64 // 2), HEAD_DIM)
    z = rope(mm(C_KA, W_ATT), 0, ROT_64 // 2)
    ka_ref[...] = z
    heads_major(ka16_ref, z, HEAD_DIM)
    z = mm(C_VA, W_ATT)
    va_ref[...] = z
    heads_major(va16_ref, z, HEAD_DIM)
    heads_major(qi_ref, rope(mm(C_QI, H_IDX * D_IDX), 1, ROT_32 // 2), D_IDX)
    r = rope(mm(C_KW, LANES), 2, ROT_32 // 2)
    ki_ref[...] = r[:, :D_IDX]
    ki16_ref[...] = r[:, :D_IDX].astype(BF16)
    wi_ref[...] = r[:, D_IDX:D_IDX + H_IDX]
    heads_major(qb_ref, mm(C_QB, W_ATT), HEAD_DIM)
    z = mm(C_KB, W_ATT)
    kb_ref[...] = z
    heads_major(kb16_ref, z, HEAD_DIM)
    z = mm(C_VB, W_ATT)
    vb_ref[...] = z
    heads_major(vb16_ref, z, HEAD_DIM)
    sga_ref[...] = jax.nn.sigmoid(mm(C_GA, D_MODEL)).astype(BF16)
    sgb_ref[...] = jax.nn.sigmoid(mm(C_GB, D_MODEL)).astype(BF16)


def _permute_w_in(w_in):
    qa, ka, va, qi, ki, wi, qb, kb, vb, ga, gb = jnp.split(w_in, np.cumsum(IN_SIZES)[:-1].tolist(), axis=1)
    scale = 1.0 / math.sqrt(HEAD_DIM)
    pad = jnp.zeros((D_MODEL, LANES - D_IDX - H_IDX), w_in.dtype)
    w = jnp.concatenate([qa * (scale * LOG2_E), ka, va, qi, ki, wi, pad, qb * scale, kb, vb, ga, gb], axis=1)
    assert w.shape[1] == N_COLS
    return w.astype(BF16)


def _project(x2d, w_perm, pos, tm):
    t = x2d.shape[0]
    tabs = _rope_tables(pos)
    rm = lambda n, dt: (jax.ShapeDtypeStruct((t, n), dt), pl.BlockSpec((tm, n), lambda i: (i, 0)))
    hm = lambda d: (jax.ShapeDtypeStruct((N_HEADS, t, d), BF16), pl.BlockSpec((N_HEADS, tm, d), lambda i: (0, i, 0)))
    outs = [hm(HEAD_DIM), rm(W_ATT, F32), hm(HEAD_DIM), rm(W_ATT, F32), hm(HEAD_DIM), hm(D_IDX), rm(D_IDX, F32),
            rm(D_IDX, BF16), rm(H_IDX, F32), hm(HEAD_DIM), rm(W_ATT, F32), hm(HEAD_DIM), rm(W_ATT, F32),
            hm(HEAD_DIM), rm(D_MODEL, BF16), rm(D_MODEL, BF16)]
    return pl.pallas_call(
        _proj_kernel,
        grid=(t // tm,),
        in_specs=[pl.BlockSpec((tm, D_MODEL), lambda i: (i, 0)), _resident((D_MODEL, N_COLS)),
                  pl.BlockSpec((9, tm, LANES), lambda i: (0, i, 0))],
        out_specs=[s for _, s in outs],
        out_shape=[s for s, _ in outs],
        compiler_params=_cparams(("parallel",)),
        name="proj_rope",
    )(x2d, w_perm, tabs)


def _sortable_key(score):
    b = lax.bitcast_convert_type(score, I32)
    return jnp.where(b < 0, b ^ jnp.int32(0x7FFFFFFF), b)


def _sortable_key_inv(key):
    return jnp.where(key < 0, key ^ jnp.int32(0x7FFFFFFF), key)


def _top_k_threshold(count_ge, k, n_keys, shape):
    kf = jnp.float32(k)
    c0 = count_ge(jnp.zeros(shape, I32))
    v0 = jnp.where(c0 >= kf, jnp.int32(0), jnp.int32(INT_MIN))
    cnt0 = jnp.where(c0 >= kf, c0, n_keys)

    def cond(st):
        t, _, cnt = st
        return jnp.logical_and(t < 31, jnp.max(jnp.abs(cnt - kf)) > 0.0)

    def body(st):
        t, v, cnt = st
        cand = v + lax.shift_left(jnp.int32(1), jnp.int32(30) - t)
        c = count_ge(cand)
        ok = c >= kf
        return t + 1, jnp.where(ok, cand, v), jnp.where(ok, c, cnt)

    _, v, cnt = lax.while_loop(cond, body, (jnp.int32(0), v0, cnt0))
    return v, cnt


def _bracket_threshold(count_ge, k, lo, hi, c_lo, c_hi):
    kf = jnp.float32(k)
    log_k = math.log(k - 0.5)

    def unfinished(lo, hi, c_lo):
        return jnp.logical_and(c_lo != kf, hi > lo + 1)

    def cond(st):
        t, lo, hi, c_lo, _ = st
        return jnp.logical_and(t < 100, jnp.max(jnp.where(unfinished(lo, hi, c_lo), 1.0, 0.0)) > 0.0)

    def body(st):
        t, lo, hi, c_lo, c_hi = st
        lo_f = lax.bitcast_convert_type(_sortable_key_inv(lo), F32)
        hi_f = lax.bitcast_convert_type(_sortable_key_inv(hi), F32)
        la, lb = jnp.log(c_lo), jnp.log(jnp.maximum(c_hi, 0.25))
        frac = (la - log_k) / (la - lb)
        mid_f = lo_f + (hi_f - lo_f) * frac
        cand_i = _sortable_key(jnp.where(mid_f == mid_f, mid_f, lo_f))
        cand_b = (lo >> 1) + (hi >> 1) + (lo & hi & 1)
        cand = jnp.where(t % 3 == 2, cand_b, cand_i)
        cand = jnp.minimum(jnp.maximum(cand, lo + 1), hi - 1)
        cand = jnp.where(unfinished(lo, hi, c_lo), cand, lo)
        c = count_ge(cand)
        up = c >= kf
        return (t + 1, jnp.where(up, cand, lo), jnp.where(up, hi, cand), jnp.where(up, c, c_lo),
                jnp.where(up, c_hi, c))

    _, lo, _, c_lo, c_hi = lax.while_loop(cond, body, (jnp.int32(0), lo, hi, c_lo, c_hi))
    return lo, c_lo, c_hi


def _tie_cut_search(count_tied_le, need, active, last_idx, n_tied):
    hi = jnp.broadcast_to(last_idx, need.shape).astype(I32)
    lo = jnp.where(active, jnp.int32(-1), hi - 1)

    def unfinished(lo, hi, h_hi):
        return jnp.logical_and(h_hi != need, hi > lo + 1)

    def cond(st):
        t, lo, hi, _, h_hi = st
        return jnp.logical_and(t < 100, jnp.max(jnp.where(unfinished(lo, hi, h_hi), 1.0, 0.0)) > 0.0)

    def body(st):
        t, lo, hi, h_lo, h_hi = st
        span = (hi - lo).astype(F32)
        step = jnp.ceil(span * (need - h_lo - 0.5) / jnp.maximum(h_hi - h_lo, 1.0)).astype(I32)
        cand = jnp.where(t % 3 == 2, lo + ((hi - lo) >> 1), lo + step)
        cand = jnp.minimum(jnp.maximum(cand, lo + 1), hi - 1)
        cand = jnp.where(unfinished(lo, hi, h_hi), cand, hi)
        c = count_tied_le(cand)
        dn = c >= need
        return (t + 1, jnp.where(dn, lo, cand), jnp.where(dn, cand, hi), jnp.where(dn, h_lo, c),
                jnp.where(dn, c, h_hi))

    _, _, hi, _, _ = lax.while_loop(cond, body, (jnp.int32(0), lo, hi, jnp.zeros_like(need), n_tied))
    return hi


def _tie_cut(count_eq_below, need, n_bits, shape):
    def bit_body(t, x):
        cand = x + lax.shift_left(jnp.int32(1), jnp.int32(n_bits - 1) - t)
        return jnp.where(count_eq_below(cand) < need, cand, x)

    return lax.fori_loop(0, n_bits, bit_body, jnp.zeros(shape, I32))


def _reduce_keys(x, op, width=64):
    n, w = x.shape
    if n % width == 0 and n > width:
        x = op(x.reshape(n // width, width, w), axis=0)
    return op(x, axis=0, keepdims=True)


def _selected(key, vt, jt, idx, on, off):
    return jnp.where(key > vt, on, jnp.where(key == vt, jnp.where(idx <= jt, on, off), off))


def _tri_pairs(t, tq, tk, reverse, j_window=None):
    qi, kj, first, last = [], [], [], []
    for i in range(t // tq):
        jd = ((i + 1) * tq - 1) // tk
        js = list(range(jd + 1))
        if j_window is not None:
            js = [j for j in js if j_window[0] <= jd - j < j_window[1]]
        if reverse:
            js = js[::-1]
        for n, j in enumerate(js):
            qi.append(i), kj.append(j), first.append(int(n == 0)), last.append(int(n == len(js) - 1))
    return tuple(np.asarray(a, np.int32) for a in (qi, kj, first, last))


def _dsa_prompt_kernel(qi_s, kj_s, first_s, last_s, qis_ref, wi_ref, ki_ref, qa_ref, ka_ref, va_ref, o_ref,
                       sc_ref, vthr_ref, jthr_ref, m_ref, l_ref, acc_ref, *, tq, tk, n_top):
    p = pl.program_id(0)
    i = qi_s[p]
    j = kj_s[p]
    q_idx = i * tq + lax.broadcasted_iota(I32, (tk, tq), 1)
    k_off = lax.broadcasted_iota(I32, (tk, tq), 0)

    @pl.when(first_s[p] == 1)
    def _():
        m_ref[...] = jnp.full(m_ref.shape, M_INIT, F32)
        l_ref[...] = jnp.zeros(l_ref.shape, F32)
        acc_ref[...] = jnp.zeros(acc_ref.shape, F32)
        n_chunks = ((i + 1) * tq + tk - 1) // tk
        w = wi_ref[...]

        cw = min(tk, 64)

        def score_chunk(c, carry):
            hi_s, lo_s = carry
            k_blk = ki_ref[pl.ds(pl.multiple_of(c * tk, tk), tk), :]
            score = jnp.zeros((tk, tq), F32)
            for h in range(H_IDX):
                s = jnp.dot(k_blk, qis_ref[h], preferred_element_type=F32)
                score = score + w[h:h + 1, :] * jnp.maximum(s, 0.0)
            score = jnp.where(c * tk + k_off <= q_idx, score, -jnp.inf)
            sc_ref[c] = _sortable_key(score)
            for r in range(tk // cw):
                blk = score[r * cw:(r + 1) * cw, :]
                hi_s = jnp.maximum(hi_s, blk)
                lo_s = jnp.minimum(lo_s, jnp.where(blk > -jnp.inf, blk, jnp.inf))
            return hi_s, lo_s

        hi_s, lo_s = lax.fori_loop(0, n_chunks, score_chunk,
                                   (jnp.full((cw, tq), -jnp.inf, F32), jnp.full((cw, tq), jnp.inf, F32)))
        k_max = _sortable_key(jnp.max(hi_s, axis=0, keepdims=True))
        k_min = _sortable_key(jnp.min(lo_s, axis=0, keepdims=True))

        def key_count(ones_fn):
            def body(c, acc):
                for r in range(tk // cw):
                    acc = acc + ones_fn(sc_ref[c, r * cw:(r + 1) * cw, :], c * tk + r * cw)
                return acc
            acc = lax.fori_loop(0, n_chunks, body, jnp.zeros((cw, tq), F32))
            return jnp.sum(acc, axis=0, keepdims=True)

        def count_ge(cand):
            cand_b = jnp.broadcast_to(cand, (cw, tq))
            return key_count(lambda blk, k0: jnp.where(blk >= cand_b, 1.0, 0.0))

        n_causal = (i * tq + lax.broadcasted_iota(I32, (1, tq), 1) + 1).astype(F32)
        few = n_causal <= n_top
        neg_inf_key = _sortable_key(jnp.full((1, tq), -jnp.inf, F32))
        lo0 = jnp.where(few, neg_inf_key, k_min)
        hi0 = jnp.where(few, neg_inf_key + 1, k_max + 1)
        c_lo0 = jnp.where(few, jnp.float32(n_top), n_causal)
        c_hi0 = jnp.zeros((1, tq), F32)
        for zk in (0, 1):
            c = count_ge(jnp.full((1, tq), zk, I32))
            up = jnp.logical_and(c >= n_top, lo0 < zk)
            dn = jnp.logical_and(c < n_top, hi0 > zk)
            lo0, c_lo0 = jnp.where(up, zk, lo0), jnp.where(up, c, c_lo0)
            hi0, c_hi0 = jnp.where(dn, zk, hi0), jnp.where(dn, c, c_hi0)
        v, cnt, above = _bracket_threshold(count_ge, n_top, lo0, hi0, c_lo0, c_hi0)
        vthr_ref[...] = jnp.broadcast_to(v, (SUBLANES, tq))
        jthr_ref[...] = jnp.full((SUBLANES, tq), NO_TIE_CUT, I32)

        @pl.when(jnp.max(cnt) > n_top)
        def _():
            need = n_top - above
            tied = cnt > n_top
            v_b = jnp.broadcast_to(v, (cw, tq))
            off = lax.broadcasted_iota(I32, (cw, tq), 0)

            def count_tied_le(x):
                x_b = jnp.broadcast_to(x, (cw, tq))
                return key_count(
                    lambda blk, k0: jnp.where(blk == v_b, jnp.where(k0 + off <= x_b, 1.0, 0.0), 0.0))
            cut = _tie_cut_search(count_tied_le, need, tied, n_chunks * tk - 1, cnt - above)
            jthr_ref[...] = jnp.broadcast_to(jnp.where(tied, cut, NO_TIE_CUT), (SUBLANES, tq))

    k_idx = j * tk + k_off
    sel = jnp.where(k_idx <= q_idx,
                    _selected(sc_ref[j], vthr_ref[0:1, :], jthr_ref[0:1, :], k_idx, 0.0, NEG_BIG), NEG_BIG)
    m_all = m_ref[...]
    l_all = l_ref[...]
    m_out, l_out = [], []

    def logits(h):
        return jnp.dot(ka_ref[h], qa_ref[h], preferred_element_type=F32) + sel

    ahead = 2
    pending = [logits(h) for h in range(ahead)]
    for h in range(N_HEADS):
        if h % ahead == 0 and h + ahead < N_HEADS:
            pending += [logits(h + ahead + a) for a in range(ahead)]
        s = pending.pop(0)
        m_prev = m_all[h:h + 1, :]
        m_new = jnp.maximum(m_prev, _reduce_keys(s, jnp.max))
        pr = jnp.exp2(s - m_new)
        alpha = jnp.exp2(m_prev - m_new)
        l_out.append(alpha * l_all[h:h + 1, :] + _reduce_keys(pr, jnp.sum))
        acc_ref[h] = alpha * acc_ref[h] + jnp.dot(va_ref[h], pr.astype(BF16), preferred_element_type=F32)
        m_out.append(m_new)
    m_ref[...] = jnp.concatenate(m_out, axis=0)
    l_ref[...] = jnp.concatenate(l_out, axis=0)

    @pl.when(last_s[p] == 1)
    def _():
        for h in range(N_HEADS):
            o_ref[h * HEAD_DIM:(h + 1) * HEAD_DIM, :] = (acc_ref[h] / l_ref[h:h + 1, :]).astype(o_ref.dtype)


def _dsa_prompt(qis_t, wi_t, ki16, qa_t, ka16, va_t, *, tq, tk):
    t = ki16.shape[0]
    n_top = min(TOPK_MAX, t // 4)
    pairs = _tri_pairs(t, tq, tk, reverse=False)
    kern = functools.partial(_dsa_prompt_kernel, tq=tq, tk=tk, n_top=n_top)
    qlane = lambda p, qi, kj, fi, la: (0, 0, qi[p])
    grid_spec = pltpu.PrefetchScalarGridSpec(
        num_scalar_prefetch=4,
        grid=(int(pairs[0].shape[0]),),
        in_specs=[
            pl.BlockSpec((H_IDX, D_IDX, tq), qlane),
            pl.BlockSpec((H_IDX, tq), lambda p, qi, kj, fi, la: (0, qi[p])),
            pl.BlockSpec((t, D_IDX), lambda p, qi, kj, fi, la: (0, 0)),
            pl.BlockSpec((N_HEADS, HEAD_DIM, tq), qlane),
            pl.BlockSpec((N_HEADS, tk, HEAD_DIM), lambda p, qi, kj, fi, la: (0, kj[p], 0)),
            pl.BlockSpec((N_HEADS, HEAD_DIM, tk), lambda p, qi, kj, fi, la: (0, 0, kj[p])),
        ],
        out_specs=pl.BlockSpec((W_ATT, tq), lambda p, qi, kj, fi, la: (0, qi[p])),
        scratch_shapes=[
            pltpu.VMEM((t // tk, tk, tq), I32),
            pltpu.VMEM((SUBLANES, tq), I32),
            pltpu.VMEM((SUBLANES, tq), I32),
            pltpu.VMEM((N_HEADS, tq), F32),
            pltpu.VMEM((N_HEADS, tq), F32),
            pltpu.VMEM((N_HEADS, HEAD_DIM, tq), F32),
        ],
    )
    return pl.pallas_call(
        kern, grid_spec=grid_spec, out_shape=jax.ShapeDtypeStruct((W_ATT, t), BF16),
        compiler_params=_cparams(("arbitrary",)), name="dsa_prompt",
    )(*pairs, qis_t, wi_t, ki16, qa_t, ka16, va_t)


def _log_sigmoid_neg(z):
    return -(jnp.maximum(z, 0.0) + jnp.log1p(jnp.exp(-jnp.abs(z))))


def _sum_of_later(lg, tri):
    hi = lg.astype(BF16)
    lo = (lg - hi.astype(F32)).astype(BF16)
    return jnp.dot(hi, tri, preferred_element_type=F32) + jnp.dot(lo, tri, preferred_element_type=F32)


def _later_matrix(n):
    r = np.arange(n)
    return jnp.asarray((r[:, None] > r[None, :]).astype(np.float32)).astype(BF16)


def _sb_prompt_kernel(qi_s, kj_s, first_s, last_s, q_ref, k_ref, v_ref, tri_ref, *rest, tq, tk, resume):
    if resume:
        acc_in_ref, carry_in_ref, acc_out_ref, carry_out_ref, carry_ref, acc_ref, done_ref = rest
    else:
        acc_out_ref, carry_out_ref, carry_ref, acc_ref, done_ref = rest
    p = pl.program_id(0)
    i = qi_s[p]
    j = kj_s[p]

    @pl.when(first_s[p] == 1)
    def _():
        if resume:
            worst = None
            for h in range(N_HEADS):
                acc_ref[h] = acc_in_ref[:, h * HEAD_DIM:(h + 1) * HEAD_DIM]
                c = carry_in_ref[:, h:h + 1]
                carry_ref[h] = c
                worst = c if worst is None else jnp.maximum(worst, c)
            done_ref[0] = (jnp.max(worst) < SB_DEAD).astype(I32)
        else:
            carry_ref[...] = jnp.zeros(carry_ref.shape, F32)
            acc_ref[...] = jnp.zeros(acc_ref.shape, F32)
            done_ref[0] = 0

    @pl.when(done_ref[0] == 0)
    def _():
        row = i * tq + lax.broadcasted_iota(I32, (tq, tk), 0)
        col = j * tk + lax.broadcasted_iota(I32, (tq, tk), 1)
        mask = col < row
        tri = tri_ref[...]
        zs = [lax.dot_general(q_ref[h], k_ref[h], (((1,), (1,)), ((), ())), preferred_element_type=F32)
              for h in range(N_HEADS)]
        lsms = [_log_sigmoid_neg(z) for z in zs]
        lgs = [jnp.where(mask, lsm, 0.0) for lsm in lsms]
        laters = [_sum_of_later(lg, tri) for lg in lgs]
        worst = None
        for h in range(N_HEADS):
            lg, later = lgs[h], laters[h]
            carry = carry_ref[h]
            a = jnp.where(mask, jnp.exp(lsms[h] + zs[h] + later + carry), 0.0)
            acc_ref[h] = acc_ref[h] + jnp.dot(a.astype(BF16), v_ref[h], preferred_element_type=F32)
            carry = carry + later[:, 0:1] + lg[:, 0:1]
            carry_ref[h] = carry
            worst = carry if worst is None else jnp.maximum(worst, carry)
        done_ref[0] = (jnp.max(worst) < SB_DEAD).astype(I32)

    @pl.when(last_s[p] == 1)
    def _():
        for h in range(N_HEADS):
            acc_out_ref[:, h * HEAD_DIM:(h + 1) * HEAD_DIM] = acc_ref[h]
            carry_out_ref[:, h:h + 1] = carry_ref[h]


def _sb_prompt_call(pairs, qb, kb16, vb16, tri, state, *, tq, tk):
    t = qb.shape[1]
    resume = state is not None
    qmap = lambda p, qi, kj, fi, la: (0, qi[p], 0)
    kmap = lambda p, qi, kj, fi, la: (0, kj[p], 0)
    rmap = lambda p, qi, kj, fi, la: (qi[p], 0)
    in_specs = [pl.BlockSpec((N_HEADS, tq, HEAD_DIM), qmap), pl.BlockSpec((N_HEADS, tk, HEAD_DIM), kmap),
                pl.BlockSpec((N_HEADS, tk, HEAD_DIM), kmap), pl.BlockSpec((tk, tk), lambda p, *_: (0, 0))]
    args = [qb, kb16, vb16, tri]
    if resume:
        in_specs += [pl.BlockSpec((tq, W_ATT), rmap), pl.BlockSpec((tq, N_HEADS), rmap)]
        args += list(state)
    grid_spec = pltpu.PrefetchScalarGridSpec(
        num_scalar_prefetch=4,
        grid=(int(pairs[0].shape[0]),),
        in_specs=in_specs,
        out_specs=[pl.BlockSpec((tq, W_ATT), rmap), pl.BlockSpec((tq, N_HEADS), rmap)],
        scratch_shapes=[pltpu.VMEM((N_HEADS, tq, 1), F32), pltpu.VMEM((N_HEADS, tq, HEAD_DIM), F32),
                        pltpu.SMEM((1,), I32)],
    )
    return pl.pallas_call(
        functools.partial(_sb_prompt_kernel, tq=tq, tk=tk, resume=resume),
        grid_spec=grid_spec,
        out_shape=[jax.ShapeDtypeStruct((t, W_ATT), F32), jax.ShapeDtypeStruct((t, N_HEADS), F32)],
        input_output_aliases=({4 + 4: 0, 4 + 5: 1} if resume else {}),
        compiler_params=_cparams(("arbitrary",)), name="sb_prompt_far" if resume else "sb_prompt_near",
    )(*pairs, *args)


def _sb_prompt(qb, kb16, vb16, *, tq, tk, near_blocks):
    t = qb.shape[1]
    tri = _later_matrix(tk)
    near = _tri_pairs(t, tq, tk, True, (0, near_blocks))
    far = _tri_pairs(t, tq, tk, True, (near_blocks, t))
    acc, carry = _sb_prompt_call(near, qb, kb16, vb16, tri, None, tq=tq, tk=tk)
    if far[0].shape[0] == 0:
        return acc
    first_row = int(far[0][0]) * tq
    alive = jnp.max(carry[first_row:]) >= SB_DEAD
    return lax.cond(
        alive,
        lambda a, c: _sb_prompt_call(far, qb, kb16, vb16, tri, (a, c), tq=tq, tk=tk)[0],
        lambda a, c: a, acc, carry)


def _dsa_sample_kernel(pt_s, qis_ref, w_ref, qa_ref, kin_ref, kan_ref, van_ref, *rest, g_pages, n_pages, n_new,
                       n_top, n_idx_bits):
    kip = rest[:g_pages]
    kap = rest[g_pages:2 * g_pages]
    vap = rest[2 * g_pages:3 * g_pages]
    o_ref, sc_ref, vthr_ref, jthr_ref, m_ref, l_ref, acc_ref = rest[3 * g_pages:]
    s = pl.program_id(1)
    n_steps = n_pages // g_pages
    n_blocks = n_pages + 1
    r_new = jnp.minimum(lax.broadcasted_iota(I32, (SAMPLE_ROWS, LANES), 0), n_new - 1)
    lane = lax.broadcasted_iota(I32, (SAMPLE_ROWS, LANES), 1)

    def scores(k_t):
        n = k_t.shape[1] // LANES
        s1 = jnp.dot(qis_ref[0], k_t, preferred_element_type=F32)
        wt = w_ref[0] if n == 1 else jnp.concatenate([w_ref[0]] * n, axis=1)
        t = wt * jnp.maximum(s1, 0.0)
        score = t[0:SAMPLE_ROWS]
        for h in range(1, H_IDX):
            score = score + t[h * SAMPLE_ROWS:(h + 1) * SAMPLE_ROWS]
        return score

    @pl.when(s < n_steps)
    def _():
        key = _sortable_key(scores(jnp.concatenate([r[0, 0].astype(BF16) for r in kip], axis=1)))
        for g in range(g_pages):
            sc_ref[s * g_pages + g] = key[:, g * LANES:(g + 1) * LANES]

    @pl.when(s == n_steps)
    def _():
        m_ref[...] = jnp.full(m_ref.shape, M_INIT, F32)
        l_ref[...] = jnp.zeros(l_ref.shape, F32)
        acc_ref[...] = jnp.zeros(acc_ref.shape, F32)
        key = _sortable_key(scores(kin_ref[0]))
        neg_inf_key = _sortable_key(jnp.full((SAMPLE_ROWS, LANES), -jnp.inf, F32))
        sc_ref[n_pages] = jnp.where(lane < n_new, jnp.where(lane <= r_new, key, neg_inf_key), jnp.int32(INT_MIN))
        keys = sc_ref[...]
        idx = (lax.broadcasted_iota(I32, (n_blocks, SAMPLE_ROWS, LANES), 0) * LANES
               + lax.broadcasted_iota(I32, (n_blocks, SAMPLE_ROWS, LANES), 2))

        def count(hit):
            ones = jnp.where(hit, 1.0, 0.0)
            parts = [jnp.sum(ones[b0:b0 + 16], axis=0) for b0 in range(0, n_blocks, 16)]
            while len(parts) > 1:
                parts = [sum(parts[k:k + 2]) for k in range(0, len(parts), 2)]
            return jnp.sum(parts[0], axis=1, keepdims=True)

        def count_ge(cand):
            return count(keys >= cand[None])

        v, cnt = _top_k_threshold(count_ge, n_top, jnp.float32(n_blocks * LANES), (SAMPLE_ROWS, 1))
        vthr_ref[...] = jnp.broadcast_to(v, (SAMPLE_ROWS, LANES))
        jthr_ref[...] = jnp.full((SAMPLE_ROWS, LANES), NO_TIE_CUT, I32)

        @pl.when(jnp.max(cnt) > n_top)
        def _():
            need = n_top - count_ge(v + 1)

            def count_eq_below(x):
                return count(jnp.logical_and(keys == v[None], idx < x[None]))
            jthr_ref[...] = jnp.broadcast_to(_tie_cut(count_eq_below, need, n_idx_bits, (SAMPLE_ROWS, 1)),
                                             (SAMPLE_ROWS, LANES))

    def chosen(blk_idx):
        return _selected(sc_ref[blk_idx], vthr_ref[...], jthr_ref[...], blk_idx * LANES + lane, 0.0, NEG_BIG)

    def attend(k_t, v_t, sel):
        q = qa_ref[0]
        sc = jnp.concatenate(
            [jnp.dot(q[h * SAMPLE_ROWS:(h + 1) * SAMPLE_ROWS], k_t[h], preferred_element_type=F32) + sel
             for h in range(N_HEADS)], axis=0)
        m_prev = m_ref[...]
        m_new = jnp.maximum(m_prev, jnp.max(sc, axis=1, keepdims=True))
        pr = jnp.exp2(sc - m_new)
        alpha = jnp.exp2(m_prev - m_new)
        l_ref[...] = alpha * l_ref[...] + jnp.sum(pr, axis=1, keepdims=True)
        pv = jnp.concatenate(
            [lax.dot_general(pr[h * SAMPLE_ROWS:(h + 1) * SAMPLE_ROWS].astype(BF16), v_t[h],
                             (((1,), (1,)), ((), ())), preferred_element_type=F32) for h in range(N_HEADS)], axis=0)
        acc_ref[...] = alpha * acc_ref[...] + pv
        m_ref[...] = m_new

    def head_pages(refs, h):
        return jnp.concatenate([r[0, 0, h].astype(BF16) for r in refs], axis=1)

    @pl.when(jnp.logical_and(s > n_steps, s <= 2 * n_steps))
    def _():
        p0 = (s - n_steps - 1) * g_pages
        attend([head_pages(kap, h) for h in range(N_HEADS)], [head_pages(vap, h) for h in range(N_HEADS)],
               jnp.concatenate([chosen(p0 + g) for g in range(g_pages)], axis=1))

    @pl.when(s == 2 * n_steps + 1)
    def _():
        sel = jnp.where(jnp.logical_and(lane <= r_new, lane < n_new), chosen(n_pages), NEG_BIG)
        attend([kan_ref[0, h] for h in range(N_HEADS)], [van_ref[0, h] for h in range(N_HEADS)], sel)
        o_ref[0] = (acc_ref[...] / l_ref[...]).astype(o_ref.dtype)


def _dsa_sample(page_table, layer, qis, w_rep, qa, ki_new, ka_new, va_new, cache_ik, cache_ak, cache_av, *,
                n_new, g_pages):
    b, n_pages = page_table.shape
    page = cache_ak.shape[-1]
    assert page == LANES and n_pages % g_pages == 0
    n_steps = n_pages // g_pages
    n_top = min(TOPK_MAX, (n_pages * page + n_new) // 4)
    kern = functools.partial(_dsa_sample_kernel, g_pages=g_pages, n_pages=n_pages, n_new=n_new, n_top=n_top,
                             n_idx_bits=max(1, int(math.ceil(math.log2((n_pages + 1) * page)))))
    seq3 = lambda i, s, pt: (i, 0, 0)
    seq4 = lambda i, s, pt: (i, 0, 0, 0)

    def score_page(g):
        return lambda i, s, pt: (layer, pt[i, jnp.minimum(s, n_steps - 1) * g_pages + g], 0, 0)

    def attend_page(g):
        return lambda i, s, pt: (layer, pt[i, jnp.clip(s - n_steps - 1, 0, n_steps - 1) * g_pages + g], 0, 0, 0)

    page_blk = (1, 1, N_HEADS, HEAD_DIM, page)
    grid_spec = pltpu.PrefetchScalarGridSpec(
        num_scalar_prefetch=1,
        grid=(b, 2 * n_steps + 2),
        in_specs=[pl.BlockSpec((1, STACK, D_IDX), seq3), pl.BlockSpec((1, STACK, LANES), seq3),
                  pl.BlockSpec((1, STACK, HEAD_DIM), seq3), pl.BlockSpec((1, D_IDX, page), seq3),
                  pl.BlockSpec((1, N_HEADS, HEAD_DIM, page), seq4), pl.BlockSpec((1, N_HEADS, HEAD_DIM, page), seq4)]
        + [pl.BlockSpec((1, 1, D_IDX, page), score_page(g)) for g in range(g_pages)]
        + [pl.BlockSpec(page_blk, attend_page(g)) for g in range(g_pages)] * 2,
        out_specs=pl.BlockSpec((1, STACK, HEAD_DIM), seq3),
        scratch_shapes=[
            pltpu.VMEM((n_pages + 1, SAMPLE_ROWS, LANES), I32),
            pltpu.VMEM((SAMPLE_ROWS, LANES), I32),
            pltpu.VMEM((SAMPLE_ROWS, LANES), I32),
            pltpu.VMEM((STACK, 1), F32),
            pltpu.VMEM((STACK, 1), F32),
            pltpu.VMEM((STACK, HEAD_DIM), F32),
        ],
    )
    return pl.pallas_call(
        kern, grid_spec=grid_spec, out_shape=jax.ShapeDtypeStruct((b, STACK, HEAD_DIM), BF16),
        compiler_params=_cparams(("arbitrary", "arbitrary")), name="dsa_sample",
    )(page_table, qis, w_rep, qa, ki_new, ka_new, va_new,
      *([cache_ik] * g_pages), *([cache_ak] * g_pages), *([cache_av] * g_pages))


def _sb_stack_block(q, k_t, v_t, valid, tri, carry, acc):
    z = jnp.concatenate([jnp.dot(q[h * SAMPLE_ROWS:(h + 1) * SAMPLE_ROWS], k_t[h], preferred_element_type=F32)
                         for h in range(N_HEADS)], axis=0)
    lsm = _log_sigmoid_neg(z)
    lg = lsm if valid is None else jnp.where(valid, lsm, 0.0)
    later = _sum_of_later(lg, tri)
    a = jnp.exp(lsm + z + later + carry)
    if valid is not None:
        a = jnp.where(valid, a, 0.0)
    pv = jnp.concatenate(
        [lax.dot_general(a[h * SAMPLE_ROWS:(h + 1) * SAMPLE_ROWS].astype(BF16), v_t[h],
                         (((1,), (1,)), ((), ())), preferred_element_type=F32) for h in range(N_HEADS)], axis=0)
    return carry + jnp.sum(lg, axis=1, keepdims=True), acc + pv


def _page_heads(ref):
    return [ref[0, 0, h].astype(BF16) for h in range(N_HEADS)]


def _sb_sample_near_kernel(pt_s, q_ref, kn_ref, vn_ref, tri_ref, *rest, n_near, n_new):
    kp = rest[:n_near]
    vp = rest[n_near:2 * n_near]
    acc_ref, carry_ref = rest[2 * n_near:]
    q = q_ref[0]
    tri = tri_ref[...]
    row = lax.broadcasted_iota(I32, (STACK, LANES), 0)
    lane = lax.broadcasted_iota(I32, (STACK, LANES), 1)
    valid = lane < jnp.minimum(row % SAMPLE_ROWS, n_new - 1)
    carry, acc = _sb_stack_block(q, [kn_ref[0, h] for h in range(N_HEADS)], [vn_ref[0, h] for h in range(N_HEADS)],
                                 valid, tri, jnp.zeros((STACK, 1), F32), jnp.zeros((STACK, HEAD_DIM), F32))
    for g in range(n_near):
        carry, acc = _sb_stack_block(q, _page_heads(kp[g]), _page_heads(vp[g]), None, tri, carry, acc)
    acc_ref[0] = acc
    carry_ref[0] = jnp.broadcast_to(carry, (STACK, LANES))


def _sb_sample_far_kernel(pt_s, q_ref, tri_ref, kp_ref, vp_ref, acc_in_ref, carry_in_ref, acc_ref, carry_ref,
                          done_ref):
    s = pl.program_id(1)

    @pl.when(s == 0)
    def _():
        acc_ref[...] = acc_in_ref[...]
        carry_ref[...] = carry_in_ref[...]
        done_ref[0] = (jnp.max(carry_in_ref[...]) < SB_DEAD).astype(I32)

    @pl.when(done_ref[0] == 0)
    def _():
        carry, acc = _sb_stack_block(q_ref[0], _page_heads(kp_ref), _page_heads(vp_ref), None, tri_ref[...],
                                     carry_ref[0][:, 0:1], acc_ref[0])
        acc_ref[0] = acc
        carry_ref[0] = jnp.broadcast_to(carry, (STACK, LANES))
        done_ref[0] = (jnp.max(carry) < SB_DEAD).astype(I32)


def _sb_sample(page_table, layer, qb, kb_new, vb_new, cache_bk, cache_bv, *, n_new, n_near):
    b, n_pages = page_table.shape
    page = cache_bk.shape[-1]
    n_near = min(n_near, n_pages)
    tri = _later_matrix(page)
    seq3 = lambda i, *_: (i, 0, 0)
    seq4 = lambda i, *_: (i, 0, 0, 0)
    const2 = lambda *_: (0, 0)
    page_blk = (1, 1, N_HEADS, HEAD_DIM, page)
    new_blk = (1, N_HEADS, HEAD_DIM, page)
    state_shapes = [jax.ShapeDtypeStruct((b, STACK, HEAD_DIM), F32), jax.ShapeDtypeStruct((b, STACK, LANES), F32)]
    state_specs = [pl.BlockSpec((1, STACK, HEAD_DIM), seq3), pl.BlockSpec((1, STACK, LANES), seq3)]

    def near_page(g):
        return lambda i, pt: (layer, pt[i, n_pages - 1 - g], 0, 0, 0)

    near_spec = pltpu.PrefetchScalarGridSpec(
        num_scalar_prefetch=1, grid=(b,),
        in_specs=[pl.BlockSpec((1, STACK, HEAD_DIM), seq3), pl.BlockSpec(new_blk, seq4), pl.BlockSpec(new_blk, seq4),
                  pl.BlockSpec((page, page), const2)]
        + [pl.BlockSpec(page_blk, near_page(g)) for g in range(n_near)] * 2,
        out_specs=state_specs)
    acc, carry = pl.pallas_call(
        functools.partial(_sb_sample_near_kernel, n_near=n_near, n_new=n_new),
        grid_spec=near_spec, out_shape=state_shapes,
        compiler_params=_cparams(("arbitrary",)), name="sb_sample_near",
    )(page_table, qb, kb_new, vb_new, tri, *([cache_bk] * n_near), *([cache_bv] * n_near))
    n_far = n_pages - n_near
    if n_far == 0:
        return acc

    far_page = lambda i, s, pt: (layer, pt[i, n_far - 1 - s], 0, 0, 0)
    far_spec = pltpu.PrefetchScalarGridSpec(
        num_scalar_prefetch=1, grid=(b, n_far),
        in_specs=[pl.BlockSpec((1, STACK, HEAD_DIM), seq3), pl.BlockSpec((page, page), const2),
                  pl.BlockSpec(page_blk, far_page), pl.BlockSpec(page_blk, far_page)] + state_specs,
        out_specs=state_specs, scratch_shapes=[pltpu.SMEM((1,), I32)])

    def far(a, c):
        return pl.pallas_call(
            _sb_sample_far_kernel, grid_spec=far_spec, out_shape=state_shapes,
            compiler_params=_cparams(("arbitrary", "arbitrary")), name="sb_sample_far",
        )(page_table, qb, tri, cache_bk, cache_bv, a, c)[0]

    return lax.cond(jnp.max(carry) >= SB_DEAD, far, lambda a, c: a, acc, carry)


def _layer_norm(y, g, b):
    mu = jnp.mean(y, axis=1, keepdims=True)
    d = y - mu
    var = jnp.mean(d * d, axis=1, keepdims=True)
    return d * lax.rsqrt(var + LN_EPS) * g + b


def _merge_kernel(oa_ref, ob_ref, sga_ref, sgb_ref, x_ref, wba_ref, wbb_ref, wo_ref, g_ref, b_ref, h_ref, h16_ref,
                  *, alpha):
    oa = jnp.dot(oa_ref[...].astype(BF16), wba_ref[...], preferred_element_type=F32)
    ob = jnp.dot(ob_ref[...].astype(BF16), wbb_ref[...], preferred_element_type=F32)
    mix = sga_ref[...].astype(F32) * oa + sgb_ref[...].astype(F32) * ob
    y = alpha * x_ref[...] + jnp.dot(mix.astype(BF16), wo_ref[...], preferred_element_type=F32)
    h = _layer_norm(y, g_ref[...], b_ref[...])
    h_ref[...] = h
    h16_ref[...] = h.astype(BF16)


def _merge(oa, ob, sga, sgb, x2d, wba, wbb, wo, g, b, *, alpha, tm):
    t = x2d.shape[0]
    rspec = lambda n: pl.BlockSpec((tm, n), lambda i: (i, 0))
    return pl.pallas_call(
        functools.partial(_merge_kernel, alpha=alpha),
        grid=(t // tm,),
        in_specs=[rspec(W_ATT), rspec(W_ATT), rspec(D_MODEL), rspec(D_MODEL), rspec(D_MODEL),
                  _resident((W_ATT, D_MODEL)), _resident((W_ATT, D_MODEL)), _resident((D_MODEL, D_MODEL)),
                  _resident((1, D_MODEL)), _resident((1, D_MODEL))],
        out_specs=[rspec(D_MODEL), rspec(D_MODEL)],
        out_shape=[jax.ShapeDtypeStruct((t, D_MODEL), F32), jax.ShapeDtypeStruct((t, D_MODEL), BF16)],
        compiler_params=_cparams(("parallel",)), name="merge_ln1",
    )(oa, ob, sga, sgb, x2d, wba, wbb, wo, g, b)


def _top_rows(x, n):
    out = []
    rank = jnp.full(x.shape, float(n), F32)
    for r in range(n):
        m = jnp.max(x, axis=0, keepdims=True)
        out.append(m)
        hit = x == m
        rank = jnp.where(hit, float(r), rank)
        x = jnp.where(hit, -jnp.inf, x)
    return out, rank


def _peer_keys_kernel(ht_ref, wq_ref, wk_ref, cnt_ref, e1_ref, rank2_ref, e2_ref):
    qt = jnp.dot(wq_ref[...], ht_ref[...], preferred_element_type=F32)
    st = jnp.dot(wk_ref[...], qt.astype(BF16), preferred_element_type=F32)
    for h in range(PEER_HEADS):
        s1 = st[h * 2 * PEER_NKEYS:h * 2 * PEER_NKEYS + PEER_NKEYS]
        s2 = st[h * 2 * PEER_NKEYS + PEER_NKEYS:(h + 1) * 2 * PEER_NKEYS]
        assert PEER_TOPK == 16
        t1, _ = _top_rows(s1, PEER_TOPK)
        t2_rows, rank2 = _top_rows(s2, PEER_TOPK)
        t2 = jnp.concatenate(t2_rows, axis=0)
        cand = jnp.concatenate(
            [t1[0] + t2, t1[1] + t2[0:8], t1[2] + t2[0:8], t1[3] + t2[0:8],
             t1[4] + t2[0:4], t1[5] + t2[0:4], t1[6] + t2[0:4], t1[7] + t2[0:4],
             jnp.concatenate(t1[8:], axis=0) + t2[0:1]], axis=0)
        cmax = t1[0] + t2[0:1]
        cur = cand
        tot = jnp.zeros_like(cmax)
        thr = cmax
        for _ in range(PEER_TOPK):
            m = jnp.max(cur, axis=0, keepdims=True)
            hit = cur == m
            tot_new = tot + jnp.sum(jnp.where(hit, 1.0, 0.0), axis=0, keepdims=True)
            thr = jnp.where(jnp.logical_and(tot < PEER_TOPK, tot_new >= PEER_TOPK), m, thr)
            tot = tot_new
            cur = jnp.where(hit, -jnp.inf, cur)
        z = jnp.sum(jnp.where(cand >= thr, jnp.exp(cand - cmax), 0.0), axis=0, keepdims=True)
        cnt = jnp.zeros_like(s1)
        for b in range(PEER_TOPK):
            cnt = cnt + jnp.where(s1 + t2_rows[b] >= thr, 1.0, 0.0)
        cnt_ref[h] = cnt
        e1_ref[h] = jnp.exp(s1 - t1[0]) / z
        rank2_ref[h] = rank2.astype(BF16)
        e2_ref[h] = jnp.exp(s2 - t2[0:1]).astype(BF16)


def _peer_keys(ht16, wq_t, wk_t, *, tn):
    t = ht16.shape[1]
    shape = lambda dt: jax.ShapeDtypeStruct((PEER_HEADS, PEER_NKEYS, t), dt)
    bspec = pl.BlockSpec((PEER_HEADS, PEER_NKEYS, tn), lambda i: (0, 0, i))
    return pl.pallas_call(
        _peer_keys_kernel,
        grid=(t // tn,),
        in_specs=[pl.BlockSpec((D_MODEL, tn), lambda i: (0, i)), _resident(wq_t.shape), _resident(wk_t.shape)],
        out_specs=[bspec, bspec, bspec, bspec],
        out_shape=[shape(F32), shape(F32), shape(BF16), shape(BF16)],
        compiler_params=_cparams(("parallel",)), name="peer_keys",
    )(ht16, wq_t, wk_t)


def _gelu_tanh(x):
    return 0.5 * x * (1.0 + jnp.tanh(math.sqrt(2.0 / math.pi) * (x + 0.044715 * (x * x * x))))


def _peer_mix_kernel(ht_ref, u_ref, vt_ref, cnt_ref, e1_ref, rank2_ref, e2_ref, ft_ref, g_ref, *, i_per_chunk):
    c = pl.program_id(1)
    tn = ht_ref.shape[1]
    pack = 2 * SUBLANES

    @pl.when(c == 0)
    def _():
        ft_ref[...] = jnp.zeros(ft_ref.shape, F32)

    def row_tile(ref, h, i):
        one = jnp.broadcast_to(ref[h, pl.ds(i, 1), :], (pack, tn)).astype(BF16)
        return jnp.concatenate([one] * (PEER_NKEYS // pack), axis=0)

    n_piece = 4
    ipp = i_per_chunk // n_piece
    rows = ipp * PEER_NKEYS
    ht = ht_ref[...]
    pre = [jnp.dot(u_ref[q * rows:(q + 1) * rows, :], ht, preferred_element_type=F32) for q in range(n_piece)]
    for q in range(n_piece):
        for ii in range(ipp):
            i = c * i_per_chunk + q * ipp + ii
            w = None
            for h in range(PEER_HEADS):
                gate = jnp.where(rank2_ref[h] < row_tile(cnt_ref, h, i), e2_ref[h] * row_tile(e1_ref, h, i),
                                 jnp.zeros((), BF16))
                w = gate if w is None else w + gate
            act = _gelu_tanh(pre[q][ii * PEER_NKEYS:(ii + 1) * PEER_NKEYS, :]).astype(BF16)
            g_ref[q * rows + ii * PEER_NKEYS:q * rows + (ii + 1) * PEER_NKEYS, :] = w * act
        ft_ref[...] += jnp.dot(vt_ref[:, q * rows:(q + 1) * rows], g_ref[q * rows:(q + 1) * rows, :],
                               preferred_element_type=F32)


def _peer_mix(ht16, u16, vt16, cnt, e1, rank2, e2, *, tn, i_per_chunk):
    t = ht16.shape[1]
    ce = i_per_chunk * PEER_NKEYS
    n_exp = u16.shape[0]
    bspec = pl.BlockSpec((PEER_HEADS, PEER_NKEYS, tn), lambda i, c: (0, 0, i))
    return pl.pallas_call(
        functools.partial(_peer_mix_kernel, i_per_chunk=i_per_chunk),
        grid=(t // tn, n_exp // ce),
        in_specs=[pl.BlockSpec((D_MODEL, tn), lambda i, c: (0, i)),
                  pl.BlockSpec((ce, D_MODEL), lambda i, c: (c, 0)),
                  pl.BlockSpec((D_MODEL, ce), lambda i, c: (0, c)),
                  bspec, bspec, bspec, bspec],
        out_specs=pl.BlockSpec((D_MODEL, tn), lambda i, c: (0, i)),
        out_shape=jax.ShapeDtypeStruct((D_MODEL, t), F32),
        scratch_shapes=[pltpu.VMEM((ce, tn), BF16)],
        compiler_params=_cparams(("parallel", "arbitrary")), name="peer_mix",
    )(ht16, u16, vt16, cnt, e1, rank2, e2)


def _final_kernel(h_ref, h16_ref, f_ref, p_ref, wg_ref, wp_ref, g_ref, b_ref, o_ref, *, alpha):
    gate = jax.nn.sigmoid(jnp.dot(h16_ref[...], wg_ref[...], preferred_element_type=F32))
    e = gate * jnp.dot(p_ref[...].astype(BF16), wp_ref[...], preferred_element_type=F32)
    o_ref[...] = _layer_norm(alpha * h_ref[...] + f_ref[...] + e, g_ref[...], b_ref[...])


def _final(h, h16, f, p2d, wg, wp, g, b, *, alpha, tm):
    t = h.shape[0]
    rspec = lambda n: pl.BlockSpec((tm, n), lambda i: (i, 0))
    return pl.pallas_call(
        functools.partial(_final_kernel, alpha=alpha),
        grid=(t // tm,),
        in_specs=[rspec(D_MODEL), rspec(D_MODEL), rspec(D_MODEL), rspec(PLE_DIM),
                  _resident((D_MODEL, D_MODEL)), _resident((PLE_DIM, D_MODEL)),
                  _resident((1, D_MODEL)), _resident((1, D_MODEL))],
        out_specs=rspec(D_MODEL),
        out_shape=jax.ShapeDtypeStruct((t, D_MODEL), F32),
        compiler_params=_cparams(("parallel",)), name="ple_ln2",
    )(h, h16, f, p2d, wg, wp, g, b)


def _pick(n, prefs):
    for c in prefs:
        if n % c == 0:
            return c
    return n


def _tiles(t):
    return dict(proj=_pick(t, (512, 256, 128)), tq=_pick(t, (256, 128)), tk=_pick(t, (512, 256, 128)),
                sb=_pick(t, (256, 128)), rows=_pick(t, (256, 128)), peer=_pick(t, (512, 256, 128)))


def _layer_weights(l, w_in, w_branch, w_out, ln1_g, ln1_b, w_pq, peer_sub_keys, peer_u, peer_v, ln2_g, ln2_b,
                   w_ple_gate, w_ple_proj):
    sk = peer_sub_keys[l]
    half = PEER_DQ // 2
    blk = jnp.zeros((2 * PEER_NKEYS, PEER_DQ), F32)
    blk = blk.at[:PEER_NKEYS, :half].set(sk[0]).at[PEER_NKEYS:, half:].set(sk[1])
    return dict(
        w_perm=_permute_w_in(w_in[l]),
        wba=w_branch[l, :W_ATT].astype(BF16), wbb=w_branch[l, W_ATT:].astype(BF16), wo=w_out[l].astype(BF16),
        g1=ln1_g[l][None], b1=ln1_b[l][None], g2=ln2_g[l][None], b2=ln2_b[l][None],
        wq_t=w_pq[l].T.astype(BF16),
        wk_t=jnp.kron(jnp.eye(PEER_HEADS, dtype=F32), blk).astype(BF16),
        u16=peer_u[l].astype(BF16), vt16=peer_v[l].T.astype(BF16),
        wg=w_ple_gate[l].astype(BF16), wp=w_ple_proj[l].astype(BF16),
    )


def _token_tail(x2d, p2d, oa, ob, sga, sgb, lw, alpha):
    tl = _tiles(x2d.shape[0])
    h, h16 = _merge(oa, ob, sga, sgb, x2d, lw["wba"], lw["wbb"], lw["wo"], lw["g1"], lw["b1"], alpha=alpha,
                    tm=tl["rows"])
    ht16 = h16.T
    cnt, e1, rank2, e2 = _peer_keys(ht16, lw["wq_t"], lw["wk_t"], tn=tl["peer"])
    ft = _peer_mix(ht16, lw["u16"], lw["vt16"], cnt, e1, rank2, e2, tn=tl["peer"], i_per_chunk=8)
    return _final(h, h16, ft.T, p2d, lw["wg"], lw["wp"], lw["g2"], lw["b2"], alpha=alpha, tm=tl["rows"])


def _stack_heads(a_hm, dec_b):
    h, _, d = a_hm.shape
    return a_hm.reshape(h, dec_b, SAMPLE_ROWS, d).transpose(1, 0, 2, 3).reshape(dec_b, h * SAMPLE_ROWS, d)


def _unstack_heads(o, dec_b):
    return o.reshape(dec_b, N_HEADS, SAMPLE_ROWS, HEAD_DIM).transpose(0, 2, 1, 3).reshape(dec_b * SAMPLE_ROWS, W_ATT)


def _new_page(a_hm, dec_b, page):
    h, _, d = a_hm.shape
    a = a_hm.reshape(h, dec_b, SAMPLE_ROWS, d).transpose(1, 0, 3, 2)
    return jnp.pad(a, ((0, 0), (0, 0), (0, 0), (0, page - SAMPLE_ROWS)))


def kernel(x_prompt, x_sample, p_prompt, p_sample, cache_a_k, cache_a_v, cache_idx_k, cache_b_k, cache_b_v,
           page_table, w_in, w_branch, w_out, ln1_g, ln1_b, w_pq, peer_sub_keys, peer_u, peer_v, ln2_g, ln2_b,
           w_ple_gate, w_ple_proj):
    depth = w_in.shape[0]
    alpha = (2.0 * depth) ** 0.25
    n_batch, seq, _ = x_prompt.shape
    dec_b, dec_t, _ = x_sample.shape
    page = cache_a_k.shape[2]
    n_pages = page_table.shape[1]
    n_past = n_pages * page
    rows = SAMPLE_ROWS
    assert dec_t <= rows and page == LANES and cache_a_k.shape[3:] == (N_HEADS, HEAD_DIM)

    slot_minor = lambda c: jnp.transpose(c, (0, 1, 3, 4, 2))
    cak, cav, cbk, cbv = (slot_minor(c) for c in (cache_a_k, cache_a_v, cache_b_k, cache_b_v))
    cik = jnp.transpose(cache_idx_k, (0, 1, 3, 2))

    pos_p = jnp.arange(seq, dtype=I32)
    pos_row = n_past + jnp.minimum(jnp.arange(rows, dtype=I32), dec_t - 1)
    pos_s = jnp.tile(pos_row, dec_b)
    g_pages = _pick(n_pages, (16, 8, 4, 2, 1))

    xp = x_prompt
    xs = jnp.pad(x_sample, ((0, 0), (0, rows - dec_t), (0, 0))).reshape(dec_b * rows, D_MODEL)
    shapes = ((N_HEADS, HEAD_DIM), (N_HEADS, HEAD_DIM), (D_IDX,), (N_HEADS, HEAD_DIM), (N_HEADS, HEAD_DIM))
    outs_p = [[] for _ in range(5)]
    outs_s = [[] for _ in range(5)]
    for l in range(depth):
        lw = _layer_weights(l, w_in, w_branch, w_out, ln1_g, ln1_b, w_pq, peer_sub_keys, peer_u, peer_v, ln2_g,
                            ln2_b, w_ple_gate, w_ple_proj)
        tl = _tiles(seq)
        new_xp = []
        per_b = [[] for _ in range(5)]
        for b in range(n_batch):
            x2d = xp[b]
            (qa, ka, ka16, va, va16, qis, ki, ki16, wi, qb, kb, kb16, vb, vb16, sga, sgb) = _project(
                x2d, lw["w_perm"], pos_p, tl["proj"])
            oa_t = _dsa_prompt(jnp.swapaxes(qis, 1, 2), wi.T, ki16, jnp.swapaxes(qa, 1, 2), ka16,
                               jnp.swapaxes(va16, 1, 2), tq=tl["tq"], tk=tl["tk"])
            ob = _sb_prompt(qb, kb16, vb16, tq=tl["sb"], tk=tl["sb"], near_blocks=3)
            new_xp.append(_token_tail(x2d, p_prompt[l, b], oa_t.T, ob, sga, sgb, lw, alpha))
            for dst, a in zip(per_b, (ka, va, ki, kb, vb)):
                dst.append(a)
        xp = jnp.stack(new_xp)
        for dst, a, sh in zip(outs_p, per_b, shapes):
            dst.append(jnp.stack(a).reshape((n_batch, seq) + sh))

        (qa, ka, ka16, va, va16, qis, ki, ki16, wi, qb, kb, kb16, vb, vb16, sga, sgb) = _project(
            xs, lw["w_perm"], pos_s, _tiles(dec_b * rows)["proj"])
        w_rep = jnp.broadcast_to(
            wi.reshape(dec_b, rows, H_IDX).transpose(0, 2, 1).reshape(dec_b, STACK, 1), (dec_b, STACK, LANES))
        ki_new = jnp.pad(ki16.reshape(dec_b, rows, D_IDX).transpose(0, 2, 1), ((0, 0), (0, 0), (0, page - rows)))
        oa = _dsa_sample(page_table, l, _stack_heads(qis, dec_b), w_rep, _stack_heads(qa, dec_b), ki_new,
                         _new_page(ka16, dec_b, page), _new_page(va16, dec_b, page), cik, cak, cav, n_new=dec_t,
                         g_pages=g_pages)
        ob = _sb_sample(page_table, l, _stack_heads(qb, dec_b), _new_page(kb16, dec_b, page),
                        _new_page(vb16, dec_b, page), cbk, cbv, n_new=dec_t, n_near=4)
        ps = jnp.pad(p_sample[l], ((0, 0), (0, rows - dec_t), (0, 0))).reshape(dec_b * rows, PLE_DIM)
        xs = _token_tail(xs, ps, _unstack_heads(oa, dec_b), _unstack_heads(ob, dec_b), sga, sgb, lw, alpha)
        for dst, a, sh in zip(outs_s, (ka, va, ki, kb, vb), shapes):
            dst.append(a.reshape((dec_b, rows) + sh)[:, :dec_t])

    y_sample = xs.reshape(dec_b, rows, D_MODEL)[:, :dec_t]
    return (xp, y_sample) + tuple(jnp.stack(o) for o in outs_p) + tuple(jnp.stack(o) for o in outs_s)
```

```python
import functools
import math

import numpy as np
import jax
import jax.numpy as jnp
from jax import lax
from jax.experimental import pallas as pl
from jax.experimental.pallas import tpu as pltpu

F32 = jnp.float32
BF16 = jnp.bfloat16
I32 = jnp.int32

D_MODEL = 1024
HEAD_DIM = 64
N_HEADS = 8
W_ATT = N_HEADS * HEAD_DIM
ROPE_THETA = 500000.0
ROT_64 = 16
ROT_32 = 8
H_IDX = 8
D_IDX = 32
TOPK_MAX = 256
PEER_HEADS = 8
PEER_NKEYS = 128
PEER_DQ = 128
PEER_TOPK = 16
PLE_DIM = 256
LN_EPS = 1e-5
IN_SIZES = (W_ATT, W_ATT, W_ATT, H_IDX * D_IDX, D_IDX, H_IDX, W_ATT, W_ATT, W_ATT, D_MODEL, D_MODEL)

LANES = 128
SUBLANES = 8
VMEM_LIMIT_BYTES = 56 * 1024 * 1024
INT_MIN = -(2 ** 31)
NO_TIE_CUT = 2 ** 30
NEG_BIG = -1e30
M_INIT = -1e20
SB_DEAD = -104.0
LOG2_E = 1.4426950408889634

C_QA, C_KA, C_VA, C_QI, C_KW = 0, 512, 1024, 1536, 1792
C_QB, C_KB, C_VB, C_GA, C_GB, N_COLS = 1920, 2432, 2944, 3456, 4480, 5504

SAMPLE_ROWS = SUBLANES
STACK = N_HEADS * SAMPLE_ROWS


def _cparams(sem):
    return pltpu.CompilerParams(dimension_semantics=sem, vmem_limit_bytes=VMEM_LIMIT_BYTES)


def _resident(shape):
    zeros = (0,) * len(shape)
    return pl.BlockSpec(shape, lambda *_: zeros, pipeline_mode=pl.Buffered(1))


def _rope_tables(pos):
    posf = pos.astype(F32)

    def cos_sin(r):
        half = r // 2
        inv = ROPE_THETA ** (-(jnp.arange(half, dtype=F32) * 2.0 / r))
        ang = posf[:, None] * inv[None, :]
        return jnp.cos(ang), jnp.sin(ang)

    def lanes(cos, sin, head_dim, half, n_rot_lanes):
        lane = np.arange(LANES)
        m = lane % head_dim
        first = (m < half) & (lane < n_rot_lanes)
        second = (m >= half) & (m < 2 * half) & (lane < n_rot_lanes)
        idx = np.where(first, m, np.where(second, m - half, 0))
        c = jnp.where(first | second, cos[:, idx], 1.0)
        sa = jnp.where(first, -sin[:, idx], 0.0)
        sb = jnp.where(second, sin[:, idx], 0.0)
        return [c, sa, sb]

    c64, s64 = cos_sin(ROT_64)
    c32, s32 = cos_sin(ROT_32)
    tabs = (lanes(c64, s64, HEAD_DIM, ROT_64 // 2, LANES) + lanes(c32, s32, D_IDX, ROT_32 // 2, LANES)
            + lanes(c32, s32, D_IDX, ROT_32 // 2, D_IDX))
    return jnp.stack(tabs).astype(F32)


def _proj_kernel(x_ref, w_ref, tab_ref, qa_ref, ka_ref, ka16_ref, va_ref, va16_ref, qi_ref, ki_ref, ki16_ref,
                 wi_ref, qb_ref, kb_ref, kb16_ref, vb_ref, vb16_ref, sga_ref, sgb_ref):
    xb = x_ref[...].astype(BF16)

    def mm(c0, n):
        return jnp.dot(xb, w_ref[:, c0:c0 + n], preferred_element_type=F32)

    def rope(z, k, sh):
        out = []
        for c in range(z.shape[1] // LANES):
            zc = z[:, c * LANES:(c + 1) * LANES]
            out.append(zc * tab_ref[3 * k] + pltpu.roll(zc, LANES - sh, 1) * tab_ref[3 * k + 1]
                       + pltpu.roll(zc, sh, 1) * tab_ref[3 * k + 2])
        return out[0] if len(out) == 1 else jnp.concatenate(out, axis=1)

    def heads_major(ref, z, width):
        for h in range(z.shape[1] // width):
            ref[h] = z[:, h * width:(h + 1) * width].astype(BF16)

    heads_major(qa_ref, rope(mm(C_QA, W_ATT), 0, ROT_64 // 2), HEAD_DIM)
    z = rope(mm(C_KA, W_ATT), 0, ROT_64 // 2)
    ka_ref[...] = z
    heads_major(ka16_ref, z, HEAD_DIM)
    z = mm(C_VA, W_ATT)
    va_ref[...] = z
    heads_major(va16_ref, z, HEAD_DIM)
    heads_major(qi_ref, rope(mm(C_QI, H_IDX * D_IDX), 1, ROT_32 // 2), D_IDX)
    r = rope(mm(C_KW, LANES), 2, ROT_32 // 2)
    ki_ref[...] = r[:, :D_IDX]
    ki16_ref[...] = r[:, :D_IDX].astype(BF16)
    wi_ref[...] = r[:, D_IDX:D_IDX + H_IDX]
    heads_major(qb_ref, mm(C_QB, W_ATT), HEAD_DIM)
    z = mm(C_KB, W_ATT)
    kb_ref[...] = z
    heads_major(kb16_ref, z, HEAD_DIM)
    z = mm(C_VB, W_ATT)
    vb_ref[...] = z
    heads_major(vb16_ref, z, HEAD_DIM)
    sga_ref[...] = jax.nn.sigmoid(mm(C_GA, D_MODEL)).astype(BF16)
    sgb_ref[...] = jax.nn.sigmoid(mm(C_GB, D_MODEL)).astype(BF16)


def _permute_w_in(w_in):
    qa, ka, va, qi, ki, wi, qb, kb, vb, ga, gb = jnp.split(w_in, np.cumsum(IN_SIZES)[:-1].tolist(), axis=1)
    scale = 1.0 / math.sqrt(HEAD_DIM)
    pad = jnp.zeros((D_MODEL, LANES - D_IDX - H_IDX), w_in.dtype)
    w = jnp.concatenate([qa * (scale * LOG2_E), ka, va, qi, ki, wi, pad, qb * scale, kb, vb, ga, gb], axis=1)
    assert w.shape[1] == N_COLS
    return w.astype(BF16)


def _project(x2d, w_perm, pos, tm):
    t = x2d.shape[0]
    tabs = _rope_tables(pos)
    rm = lambda n, dt: (jax.ShapeDtypeStruct((t, n), dt), pl.BlockSpec((tm, n), lambda i: (i, 0)))
    hm = lambda d: (jax.ShapeDtypeStruct((N_HEADS, t, d), BF16), pl.BlockSpec((N_HEADS, tm, d), lambda i: (0, i, 0)))
    outs = [hm(HEAD_DIM), rm(W_ATT, F32), hm(HEAD_DIM), rm(W_ATT, F32), hm(HEAD_DIM), hm(D_IDX), rm(D_IDX, F32),
            rm(D_IDX, BF16), rm(H_IDX, F32), hm(HEAD_DIM), rm(W_ATT, F32), hm(HEAD_DIM), rm(W_ATT, F32),
            hm(HEAD_DIM), rm(D_MODEL, BF16), rm(D_MODEL, BF16)]
    return pl.pallas_call(
        _proj_kernel,
        grid=(t // tm,),
        in_specs=[pl.BlockSpec((tm, D_MODEL), lambda i: (i, 0)), _resident((D_MODEL, N_COLS)),
                  pl.BlockSpec((9, tm, LANES), lambda i: (0, i, 0))],
        out_specs=[s for _, s in outs],
        out_shape=[s for s, _ in outs],
        compiler_params=_cparams(("parallel",)),
        name="proj_rope",
    )(x2d, w_perm, tabs)


def _sortable_key(score):
    b = lax.bitcast_convert_type(score, I32)
    return jnp.where(b < 0, b ^ jnp.int32(0x7FFFFFFF), b)


def _sortable_key_inv(key):
    return jnp.where(key < 0, key ^ jnp.int32(0x7FFFFFFF), key)


def _bracket_start(count_ge, k, k_min, k_max, n_finite, shape):
    few = n_finite <= k
    neg_inf_key = _sortable_key(jnp.full(shape, -jnp.inf, F32))
    lo = jnp.where(few, neg_inf_key, k_min)
    hi = jnp.where(few, neg_inf_key + 1, k_max + 1)
    c_lo = jnp.where(few, jnp.float32(k), n_finite)
    c_hi = jnp.zeros(shape, F32)
    for zk in (0, 1):
        c = count_ge(jnp.full(shape, zk, I32))
        up = jnp.logical_and(c >= k, lo < zk)
        dn = jnp.logical_and(c < k, hi > zk)
        lo, c_lo = jnp.where(up, zk, lo), jnp.where(up, c, c_lo)
        hi, c_hi = jnp.where(dn, zk, hi), jnp.where(dn, c, c_hi)
    return lo, hi, c_lo, c_hi


def _bracket_threshold(count_ge, k, lo, hi, c_lo, c_hi):
    kf = jnp.float32(k)
    log_k = math.log(k - 0.5)

    def unfinished(lo, hi, c_lo):
        return jnp.logical_and(c_lo != kf, hi > lo + 1)

    def cond(st):
        t, lo, hi, c_lo, _ = st
        return jnp.logical_and(t < 100, jnp.max(jnp.where(unfinished(lo, hi, c_lo), 1.0, 0.0)) > 0.0)

    def body(st):
        t, lo, hi, c_lo, c_hi = st
        lo_f = lax.bitcast_convert_type(_sortable_key_inv(lo), F32)
        hi_f = lax.bitcast_convert_type(_sortable_key_inv(hi), F32)
        la, lb = jnp.log(c_lo), jnp.log(jnp.maximum(c_hi, 0.25))
        frac = (la - log_k) / (la - lb)
        mid_f = lo_f + (hi_f - lo_f) * frac
        cand_i = _sortable_key(jnp.where(mid_f == mid_f, mid_f, lo_f))
        cand_b = (lo >> 1) + (hi >> 1) + (lo & hi & 1)
        cand = jnp.where(t % 3 == 2, cand_b, cand_i)
        cand = jnp.minimum(jnp.maximum(cand, lo + 1), hi - 1)
        cand = jnp.where(unfinished(lo, hi, c_lo), cand, lo)
        c = count_ge(cand)
        up = c >= kf
        return (t + 1, jnp.where(up, cand, lo), jnp.where(up, hi, cand), jnp.where(up, c, c_lo),
                jnp.where(up, c_hi, c))

    _, lo, _, c_lo, c_hi = lax.while_loop(cond, body, (jnp.int32(0), lo, hi, c_lo, c_hi))
    return lo, c_lo, c_hi


def _tie_cut_search(count_tied_le, need, active, last_idx, n_tied):
    hi = jnp.broadcast_to(last_idx, need.shape).astype(I32)
    lo = jnp.where(active, jnp.int32(-1), hi - 1)

    def unfinished(lo, hi, h_hi):
        return jnp.logical_and(h_hi != need, hi > lo + 1)

    def cond(st):
        t, lo, hi, _, h_hi = st
        return jnp.logical_and(t < 100, jnp.max(jnp.where(unfinished(lo, hi, h_hi), 1.0, 0.0)) > 0.0)

    def body(st):
        t, lo, hi, h_lo, h_hi = st
        span = (hi - lo).astype(F32)
        step = jnp.ceil(span * (need - h_lo - 0.5) / jnp.maximum(h_hi - h_lo, 1.0)).astype(I32)
        cand = jnp.where(t % 3 == 2, lo + ((hi - lo) >> 1), lo + step)
        cand = jnp.minimum(jnp.maximum(cand, lo + 1), hi - 1)
        cand = jnp.where(unfinished(lo, hi, h_hi), cand, hi)
        c = count_tied_le(cand)
        dn = c >= need
        return (t + 1, jnp.where(dn, lo, cand), jnp.where(dn, cand, hi), jnp.where(dn, h_lo, c),
                jnp.where(dn, c, h_hi))

    _, _, hi, _, _ = lax.while_loop(cond, body, (jnp.int32(0), lo, hi, jnp.zeros_like(need), n_tied))
    return hi


def _reduce_keys(x, op, width=64):
    n, w = x.shape
    if n % width == 0 and n > width:
        x = op(x.reshape(n // width, width, w), axis=0)
    return op(x, axis=0, keepdims=True)


def _selected(key, vt, jt, idx, on, off):
    return jnp.where(key > vt, on, jnp.where(key == vt, jnp.where(idx <= jt, on, off), off))


def _tri_pairs(t, tq, tk, reverse, j_window=None):
    qi, kj, first, last = [], [], [], []
    for i in range(t // tq):
        jd = ((i + 1) * tq - 1) // tk
        js = list(range(jd + 1))
        if j_window is not None:
            js = [j for j in js if j_window[0] <= jd - j < j_window[1]]
        if reverse:
            js = js[::-1]
        for n, j in enumerate(js):
            qi.append(i), kj.append(j), first.append(int(n == 0)), last.append(int(n == len(js) - 1))
    return tuple(np.asarray(a, np.int32) for a in (qi, kj, first, last))


def _dsa_prompt_kernel(qi_s, kj_s, first_s, last_s, qis_ref, wi_ref, ki_ref, qa_ref, ka_ref, va_ref, o_ref,
                       sc_ref, vthr_ref, jthr_ref, m_ref, l_ref, acc_ref, *, tq, tk, n_top):
    p = pl.program_id(0)
    i = qi_s[p]
    j = kj_s[p]
    q_idx = i * tq + lax.broadcasted_iota(I32, (tk, tq), 1)
    k_off = lax.broadcasted_iota(I32, (tk, tq), 0)

    @pl.when(first_s[p] == 1)
    def _():
        m_ref[...] = jnp.full(m_ref.shape, M_INIT, F32)
        l_ref[...] = jnp.zeros(l_ref.shape, F32)
        acc_ref[...] = jnp.zeros(acc_ref.shape, F32)
        n_chunks = ((i + 1) * tq + tk - 1) // tk
        w = wi_ref[...]

        cw = min(tk, 64)

        def score_chunk(c, carry):
            hi_s, lo_s = carry
            k_blk = ki_ref[pl.ds(pl.multiple_of(c * tk, tk), tk), :]
            score = jnp.zeros((tk, tq), F32)
            for h in range(H_IDX):
                s = jnp.dot(k_blk, qis_ref[h], preferred_element_type=F32)
                score = score + w[h:h + 1, :] * jnp.maximum(s, 0.0)
            score = jnp.where(c * tk + k_off <= q_idx, score, -jnp.inf)
            sc_ref[c] = _sortable_key(score)
            for r in range(tk // cw):
                blk = score[r * cw:(r + 1) * cw, :]
                hi_s = jnp.maximum(hi_s, blk)
                lo_s = jnp.minimum(lo_s, jnp.where(blk > -jnp.inf, blk, jnp.inf))
            return hi_s, lo_s

        hi_s, lo_s = lax.fori_loop(0, n_chunks, score_chunk,
                                   (jnp.full((cw, tq), -jnp.inf, F32), jnp.full((cw, tq), jnp.inf, F32)))
        k_max = _sortable_key(jnp.max(hi_s, axis=0, keepdims=True))
        k_min = _sortable_key(jnp.min(lo_s, axis=0, keepdims=True))

        def key_count(ones_fn):
            def body(c, acc):
                for r in range(tk // cw):
                    acc = acc + ones_fn(sc_ref[c, r * cw:(r + 1) * cw, :], c * tk + r * cw)
                return acc
            acc = lax.fori_loop(0, n_chunks, body, jnp.zeros((cw, tq), F32))
            return jnp.sum(acc, axis=0, keepdims=True)

        def count_ge(cand):
            cand_b = jnp.broadcast_to(cand, (cw, tq))
            return key_count(lambda blk, k0: jnp.where(blk >= cand_b, 1.0, 0.0))

        n_causal = (i * tq + lax.broadcasted_iota(I32, (1, tq), 1) + 1).astype(F32)
        lo0, hi0, c_lo0, c_hi0 = _bracket_start(count_ge, n_top, k_min, k_max, n_causal, (1, tq))
        v, cnt, above = _bracket_threshold(count_ge, n_top, lo0, hi0, c_lo0, c_hi0)
        vthr_ref[...] = jnp.broadcast_to(v, (SUBLANES, tq))
        jthr_ref[...] = jnp.full((SUBLANES, tq), NO_TIE_CUT, I32)

        @pl.when(jnp.max(cnt) > n_top)
        def _():
            need = n_top - above
            tied = cnt > n_top
            v_b = jnp.broadcast_to(v, (cw, tq))
            off = lax.broadcasted_iota(I32, (cw, tq), 0)

            def count_tied_le(x):
                x_b = jnp.broadcast_to(x, (cw, tq))
                return key_count(
                    lambda blk, k0: jnp.where(blk == v_b, jnp.where(k0 + off <= x_b, 1.0, 0.0), 0.0))
            cut = _tie_cut_search(count_tied_le, need, tied, n_chunks * tk - 1, cnt - above)
            jthr_ref[...] = jnp.broadcast_to(jnp.where(tied, cut, NO_TIE_CUT), (SUBLANES, tq))

    k_idx = j * tk + k_off
    sel = jnp.where(k_idx <= q_idx,
                    _selected(sc_ref[j], vthr_ref[0:1, :], jthr_ref[0:1, :], k_idx, 0.0, NEG_BIG), NEG_BIG)
    m_all = m_ref[...]
    l_all = l_ref[...]
    m_out, l_out = [], []

    def logits(h):
        return jnp.dot(ka_ref[h], qa_ref[h], preferred_element_type=F32) + sel

    ahead = 2
    pending = [logits(h) for h in range(ahead)]
    for h in range(N_HEADS):
        if h % ahead == 0 and h + ahead < N_HEADS:
            pending += [logits(h + ahead + a) for a in range(ahead)]
        s = pending.pop(0)
        m_prev = m_all[h:h + 1, :]
        m_new = jnp.maximum(m_prev, _reduce_keys(s, jnp.max))
        pr = jnp.exp2(s - m_new)
        alpha = jnp.exp2(m_prev - m_new)
        l_out.append(alpha * l_all[h:h + 1, :] + _reduce_keys(pr, jnp.sum))
        acc_ref[h] = alpha * acc_ref[h] + jnp.dot(va_ref[h], pr.astype(BF16), preferred_element_type=F32)
        m_out.append(m_new)
    m_ref[...] = jnp.concatenate(m_out, axis=0)
    l_ref[...] = jnp.concatenate(l_out, axis=0)

    @pl.when(last_s[p] == 1)
    def _():
        for h in range(N_HEADS):
            o_ref[h * HEAD_DIM:(h + 1) * HEAD_DIM, :] = (acc_ref[h] / l_ref[h:h + 1, :]).astype(o_ref.dtype)


def _dsa_prompt(qis_t, wi_t, ki16, qa_t, ka16, va_t, *, tq, tk):
    t = ki16.shape[0]
    n_top = min(TOPK_MAX, t // 4)
    pairs = _tri_pairs(t, tq, tk, reverse=False)
    kern = functools.partial(_dsa_prompt_kernel, tq=tq, tk=tk, n_top=n_top)
    qlane = lambda p, qi, kj, fi, la: (0, 0, qi[p])
    grid_spec = pltpu.PrefetchScalarGridSpec(
        num_scalar_prefetch=4,
        grid=(int(pairs[0].shape[0]),),
        in_specs=[
            pl.BlockSpec((H_IDX, D_IDX, tq), qlane),
            pl.BlockSpec((H_IDX, tq), lambda p, qi, kj, fi, la: (0, qi[p])),
            pl.BlockSpec((t, D_IDX), lambda p, qi, kj, fi, la: (0, 0)),
            pl.BlockSpec((N_HEADS, HEAD_DIM, tq), qlane),
            pl.BlockSpec((N_HEADS, tk, HEAD_DIM), lambda p, qi, kj, fi, la: (0, kj[p], 0)),
            pl.BlockSpec((N_HEADS, HEAD_DIM, tk), lambda p, qi, kj, fi, la: (0, 0, kj[p])),
        ],
        out_specs=pl.BlockSpec((W_ATT, tq), lambda p, qi, kj, fi, la: (0, qi[p])),
        scratch_shapes=[
            pltpu.VMEM((t // tk, tk, tq), I32),
            pltpu.VMEM((SUBLANES, tq), I32),
            pltpu.VMEM((SUBLANES, tq), I32),
            pltpu.VMEM((N_HEADS, tq), F32),
            pltpu.VMEM((N_HEADS, tq), F32),
            pltpu.VMEM((N_HEADS, HEAD_DIM, tq), F32),
        ],
    )
    return pl.pallas_call(
        kern, grid_spec=grid_spec, out_shape=jax.ShapeDtypeStruct((W_ATT, t), BF16),
        compiler_params=_cparams(("arbitrary",)), name="dsa_prompt",
    )(*pairs, qis_t, wi_t, ki16, qa_t, ka16, va_t)


def _log_sigmoid_neg(z):
    return -(jnp.maximum(z, 0.0) + jnp.log1p(jnp.exp(-jnp.abs(z))))


def _sum_of_later(lg, tri):
    hi = lg.astype(BF16)
    lo = (lg - hi.astype(F32)).astype(BF16)
    return jnp.dot(hi, tri, preferred_element_type=F32) + jnp.dot(lo, tri, preferred_element_type=F32)


def _later_matrix(n):
    r = np.arange(n)
    return jnp.asarray((r[:, None] > r[None, :]).astype(np.float32)).astype(BF16)


def _sb_prompt_kernel(qi_s, kj_s, first_s, last_s, q_ref, k_ref, v_ref, tri_ref, *rest, tq, tk, resume):
    if resume:
        acc_in_ref, carry_in_ref, acc_out_ref, carry_out_ref, carry_ref, acc_ref, done_ref = rest
    else:
        acc_out_ref, carry_out_ref, carry_ref, acc_ref, done_ref = rest
    p = pl.program_id(0)
    i = qi_s[p]
    j = kj_s[p]

    @pl.when(first_s[p] == 1)
    def _():
        if resume:
            worst = None
            for h in range(N_HEADS):
                acc_ref[h] = acc_in_ref[:, h * HEAD_DIM:(h + 1) * HEAD_DIM]
                c = carry_in_ref[:, h:h + 1]
                carry_ref[h] = c
                worst = c if worst is None else jnp.maximum(worst, c)
            done_ref[0] = (jnp.max(worst) < SB_DEAD).astype(I32)
        else:
            carry_ref[...] = jnp.zeros(carry_ref.shape, F32)
            acc_ref[...] = jnp.zeros(acc_ref.shape, F32)
            done_ref[0] = 0

    @pl.when(done_ref[0] == 0)
    def _():
        row = i * tq + lax.broadcasted_iota(I32, (tq, tk), 0)
        col = j * tk + lax.broadcasted_iota(I32, (tq, tk), 1)
        mask = col < row
        tri = tri_ref[...]
        zs = [lax.dot_general(q_ref[h], k_ref[h], (((1,), (1,)), ((), ())), preferred_element_type=F32)
              for h in range(N_HEADS)]
        lsms = [_log_sigmoid_neg(z) for z in zs]
        lgs = [jnp.where(mask, lsm, 0.0) for lsm in lsms]
        laters = [_sum_of_later(lg, tri) for lg in lgs]
        worst = None
        for h in range(N_HEADS):
            lg, later = lgs[h], laters[h]
            carry = carry_ref[h]
            a = jnp.where(mask, jnp.exp(lsms[h] + zs[h] + later + carry), 0.0)
            acc_ref[h] = acc_ref[h] + jnp.dot(a.astype(BF16), v_ref[h], preferred_element_type=F32)
            carry = carry + later[:, 0:1] + lg[:, 0:1]
            carry_ref[h] = carry
            worst = carry if worst is None else jnp.maximum(worst, carry)
        done_ref[0] = (jnp.max(worst) < SB_DEAD).astype(I32)

    @pl.when(last_s[p] == 1)
    def _():
        for h in range(N_HEADS):
            acc_out_ref[:, h * HEAD_DIM:(h + 1) * HEAD_DIM] = acc_ref[h]
            carry_out_ref[:, h:h + 1] = carry_ref[h]


def _sb_prompt_call(pairs, qb, kb16, vb16, tri, state, *, tq, tk):
    t = qb.shape[1]
    resume = state is not None
    qmap = lambda p, qi, kj, fi, la: (0, qi[p], 0)
    kmap = lambda p, qi, kj, fi, la: (0, kj[p], 0)
    rmap = lambda p, qi, kj, fi, la: (qi[p], 0)
    in_specs = [pl.BlockSpec((N_HEADS, tq, HEAD_DIM), qmap), pl.BlockSpec((N_HEADS, tk, HEAD_DIM), kmap),
                pl.BlockSpec((N_HEADS, tk, HEAD_DIM), kmap), pl.BlockSpec((tk, tk), lambda p, *_: (0, 0))]
    args = [qb, kb16, vb16, tri]
    if resume:
        in_specs += [pl.BlockSpec((tq, W_ATT), rmap), pl.BlockSpec((tq, N_HEADS), rmap)]
        args += list(state)
    grid_spec = pltpu.PrefetchScalarGridSpec(
        num_scalar_prefetch=4,
        grid=(int(pairs[0].shape[0]),),
        in_specs=in_specs,
        out_specs=[pl.BlockSpec((tq, W_ATT), rmap), pl.BlockSpec((tq, N_HEADS), rmap)],
        scratch_shapes=[pltpu.VMEM((N_HEADS, tq, 1), F32), pltpu.VMEM((N_HEADS, tq, HEAD_DIM), F32),
                        pltpu.SMEM((1,), I32)],
    )
    return pl.pallas_call(
        functools.partial(_sb_prompt_kernel, tq=tq, tk=tk, resume=resume),
        grid_spec=grid_spec,
        out_shape=[jax.ShapeDtypeStruct((t, W_ATT), F32), jax.ShapeDtypeStruct((t, N_HEADS), F32)],
        input_output_aliases=({4 + 4: 0, 4 + 5: 1} if resume else {}),
        compiler_params=_cparams(("arbitrary",)), name="sb_prompt_far" if resume else "sb_prompt_near",
    )(*pairs, *args)


def _sb_prompt(qb, kb16, vb16, *, tq, tk, near_blocks):
    t = qb.shape[1]
    tri = _later_matrix(tk)
    near = _tri_pairs(t, tq, tk, True, (0, near_blocks))
    far = _tri_pairs(t, tq, tk, True, (near_blocks, t))
    acc, carry = _sb_prompt_call(near, qb, kb16, vb16, tri, None, tq=tq, tk=tk)
    if far[0].shape[0] == 0:
        return acc
    first_row = int(far[0][0]) * tq
    alive = jnp.max(carry[first_row:]) >= SB_DEAD
    return lax.cond(
        alive,
        lambda a, c: _sb_prompt_call(far, qb, kb16, vb16, tri, (a, c), tq=tq, tk=tk)[0],
        lambda a, c: a, acc, carry)


def _dsa_sample_kernel(pt_s, qis_ref, w_ref, qa_ref, kin_ref, kan_ref, van_ref, *rest, g_pages, n_pages, n_new,
                       n_top):
    kip = rest[:g_pages]
    kap = rest[g_pages:2 * g_pages]
    vap = rest[2 * g_pages:3 * g_pages]
    o_ref, sc_ref, vthr_ref, jthr_ref, m_ref, l_ref, acc_ref = rest[3 * g_pages:]
    s = pl.program_id(1)
    n_steps = n_pages // g_pages
    n_blocks = n_pages + 1
    r_new = jnp.minimum(lax.broadcasted_iota(I32, (SAMPLE_ROWS, LANES), 0), n_new - 1)
    lane = lax.broadcasted_iota(I32, (SAMPLE_ROWS, LANES), 1)

    def scores(k_t):
        n = k_t.shape[1] // LANES
        s1 = jnp.dot(qis_ref[0], k_t, preferred_element_type=F32)
        wt = w_ref[0] if n == 1 else jnp.concatenate([w_ref[0]] * n, axis=1)
        t = wt * jnp.maximum(s1, 0.0)
        score = t[0:SAMPLE_ROWS]
        for h in range(1, H_IDX):
            score = score + t[h * SAMPLE_ROWS:(h + 1) * SAMPLE_ROWS]
        return score

    @pl.when(s < n_steps)
    def _():
        key = _sortable_key(scores(jnp.concatenate([r[0, 0].astype(BF16) for r in kip], axis=1)))
        for g in range(g_pages):
            sc_ref[s * g_pages + g] = key[:, g * LANES:(g + 1) * LANES]

    @pl.when(s == n_steps)
    def _():
        m_ref[...] = jnp.full(m_ref.shape, M_INIT, F32)
        l_ref[...] = jnp.zeros(l_ref.shape, F32)
        acc_ref[...] = jnp.zeros(acc_ref.shape, F32)
        key = _sortable_key(scores(kin_ref[0]))
        neg_inf_key = _sortable_key(jnp.full((SAMPLE_ROWS, LANES), -jnp.inf, F32))
        sc_ref[n_pages] = jnp.where(lane < n_new, jnp.where(lane <= r_new, key, neg_inf_key), jnp.int32(INT_MIN))
        keys = sc_ref[...]
        idx = (lax.broadcasted_iota(I32, (n_blocks, SAMPLE_ROWS, LANES), 0) * LANES
               + lax.broadcasted_iota(I32, (n_blocks, SAMPLE_ROWS, LANES), 2))

        def count(hit):
            ones = jnp.where(hit, 1.0, 0.0)
            parts = [jnp.sum(ones[b0:b0 + 16], axis=0) for b0 in range(0, n_blocks, 16)]
            while len(parts) > 1:
                parts = [sum(parts[k:k + 2]) for k in range(0, len(parts), 2)]
            return jnp.sum(parts[0], axis=1, keepdims=True)

        def count_ge(cand):
            return count(keys >= cand[None])

        scores_back = lax.bitcast_convert_type(_sortable_key_inv(keys), F32)
        finite = keys > neg_inf_key[None]

        def over_keys(x, op):
            return op(op(x, axis=0), axis=1, keepdims=True)

        k_max = _sortable_key(over_keys(jnp.where(finite, scores_back, -jnp.inf), jnp.max))
        k_min = _sortable_key(over_keys(jnp.where(finite, scores_back, jnp.inf), jnp.min))
        shape = (SAMPLE_ROWS, 1)
        lo0, hi0, c_lo0, c_hi0 = _bracket_start(count_ge, n_top, k_min, k_max, count(finite), shape)
        v, cnt, above = _bracket_threshold(count_ge, n_top, lo0, hi0, c_lo0, c_hi0)
        vthr_ref[...] = jnp.broadcast_to(v, (SAMPLE_ROWS, LANES))
        jthr_ref[...] = jnp.full((SAMPLE_ROWS, LANES), NO_TIE_CUT, I32)

        @pl.when(jnp.max(cnt) > n_top)
        def _():
            tied = cnt > n_top

            def count_tied_le(x):
                return count(jnp.logical_and(keys == v[None], idx <= x[None]))
            cut = _tie_cut_search(count_tied_le, n_top - above, tied, n_blocks * LANES - 1, cnt - above)
            jthr_ref[...] = jnp.broadcast_to(jnp.where(tied, cut, NO_TIE_CUT), (SAMPLE_ROWS, LANES))

    def chosen(blk_idx):
        return _selected(sc_ref[blk_idx], vthr_ref[...], jthr_ref[...], blk_idx * LANES + lane, 0.0, NEG_BIG)

    def attend(k_t, v_t, sel):
        q = qa_ref[0]
        sc = jnp.concatenate(
            [jnp.dot(q[h * SAMPLE_ROWS:(h + 1) * SAMPLE_ROWS], k_t[h], preferred_element_type=F32) + sel
             for h in range(N_HEADS)], axis=0)
        m_prev = m_ref[...]
        m_new = jnp.maximum(m_prev, jnp.max(sc, axis=1, keepdims=True))
        pr = jnp.exp2(sc - m_new)
        alpha = jnp.exp2(m_prev - m_new)
        l_ref[...] = alpha * l_ref[...] + jnp.sum(pr, axis=1, keepdims=True)
        pv = jnp.concatenate(
            [lax.dot_general(pr[h * SAMPLE_ROWS:(h + 1) * SAMPLE_ROWS].astype(BF16), v_t[h],
                             (((1,), (1,)), ((), ())), preferred_element_type=F32) for h in range(N_HEADS)], axis=0)
        acc_ref[...] = alpha * acc_ref[...] + pv
        m_ref[...] = m_new

    def head_pages(refs, h):
        return jnp.concatenate([r[0, 0, h].astype(BF16) for r in refs], axis=1)

    @pl.when(jnp.logical_and(s > n_steps, s <= 2 * n_steps))
    def _():
        p0 = (s - n_steps - 1) * g_pages
        attend([head_pages(kap, h) for h in range(N_HEADS)], [head_pages(vap, h) for h in range(N_HEADS)],
               jnp.concatenate([chosen(p0 + g) for g in range(g_pages)], axis=1))

    @pl.when(s == 2 * n_steps + 1)
    def _():
        sel = jnp.where(jnp.logical_and(lane <= r_new, lane < n_new), chosen(n_pages), NEG_BIG)
        attend([kan_ref[0, h] for h in range(N_HEADS)], [van_ref[0, h] for h in range(N_HEADS)], sel)
        o_ref[0] = (acc_ref[...] / l_ref[...]).astype(o_ref.dtype)


def _dsa_sample(page_table, layer, qis, w_rep, qa, ki_new, ka_new, va_new, cache_ik, cache_ak, cache_av, *,
                n_new, g_pages):
    b, n_pages = page_table.shape
    page = cache_ak.shape[-1]
    assert page == LANES and n_pages % g_pages == 0
    n_steps = n_pages // g_pages
    n_top = min(TOPK_MAX, (n_pages * page + n_new) // 4)
    kern = functools.partial(_dsa_sample_kernel, g_pages=g_pages, n_pages=n_pages, n_new=n_new, n_top=n_top)
    seq3 = lambda i, s, pt: (i, 0, 0)
    seq4 = lambda i, s, pt: (i, 0, 0, 0)

    def score_page(g):
        return lambda i, s, pt: (layer, pt[i, jnp.minimum(s, n_steps - 1) * g_pages + g], 0, 0)

    def attend_page(g):
        return lambda i, s, pt: (layer, pt[i, jnp.clip(s - n_steps - 1, 0, n_steps - 1) * g_pages + g], 0, 0, 0)

    page_blk = (1, 1, N_HEADS, HEAD_DIM, page)
    grid_spec = pltpu.PrefetchScalarGridSpec(
        num_scalar_prefetch=1,
        grid=(b, 2 * n_steps + 2),
        in_specs=[pl.BlockSpec((1, STACK, D_IDX), seq3), pl.BlockSpec((1, STACK, LANES), seq3),
                  pl.BlockSpec((1, STACK, HEAD_DIM), seq3), pl.BlockSpec((1, D_IDX, page), seq3),
                  pl.BlockSpec((1, N_HEADS, HEAD_DIM, page), seq4), pl.BlockSpec((1, N_HEADS, HEAD_DIM, page), seq4)]
        + [pl.BlockSpec((1, 1, D_IDX, page), score_page(g)) for g in range(g_pages)]
        + [pl.BlockSpec(page_blk, attend_page(g)) for g in range(g_pages)] * 2,
        out_specs=pl.BlockSpec((1, STACK, HEAD_DIM), seq3),
        scratch_shapes=[
            pltpu.VMEM((n_pages + 1, SAMPLE_ROWS, LANES), I32),
            pltpu.VMEM((SAMPLE_ROWS, LANES), I32),
            pltpu.VMEM((SAMPLE_ROWS, LANES), I32),
            pltpu.VMEM((STACK, 1), F32),
            pltpu.VMEM((STACK, 1), F32),
            pltpu.VMEM((STACK, HEAD_DIM), F32),
        ],
    )
    return pl.pallas_call(
        kern, grid_spec=grid_spec, out_shape=jax.ShapeDtypeStruct((b, STACK, HEAD_DIM), BF16),
        compiler_params=_cparams(("arbitrary", "arbitrary")), name="dsa_sample",
    )(page_table, qis, w_rep, qa, ki_new, ka_new, va_new,
      *([cache_ik] * g_pages), *([cache_ak] * g_pages), *([cache_av] * g_pages))


def _sb_stack_block(q, k_t, v_t, valid, tri, carry, acc):
    z = jnp.concatenate([jnp.dot(q[h * SAMPLE_ROWS:(h + 1) * SAMPLE_ROWS], k_t[h], preferred_element_type=F32)
                         for h in range(N_HEADS)], axis=0)
    lsm = _log_sigmoid_neg(z)
    lg = lsm if valid is None else jnp.where(valid, lsm, 0.0)
    later = _sum_of_later(lg, tri)
    a = jnp.exp(lsm + z + later + carry)
    if valid is not None:
        a = jnp.where(valid, a, 0.0)
    pv = jnp.concatenate(
        [lax.dot_general(a[h * SAMPLE_ROWS:(h + 1) * SAMPLE_ROWS].astype(BF16), v_t[h],
                         (((1,), (1,)), ((), ())), preferred_element_type=F32) for h in range(N_HEADS)], axis=0)
    return carry + jnp.sum(lg, axis=1, keepdims=True), acc + pv


def _page_heads(ref):
    return [ref[0, 0, h].astype(BF16) for h in range(N_HEADS)]


def _sb_sample_near_kernel(pt_s, q_ref, kn_ref, vn_ref, tri_ref, *rest, n_near, n_new):
    kp = rest[:n_near]
    vp = rest[n_near:2 * n_near]
    acc_ref, carry_ref = rest[2 * n_near:]
    q = q_ref[0]
    tri = tri_ref[...]
    row = lax.broadcasted_iota(I32, (STACK, LANES), 0)
    lane = lax.broadcasted_iota(I32, (STACK, LANES), 1)
    valid = lane < jnp.minimum(row % SAMPLE_ROWS, n_new - 1)
    carry, acc = _sb_stack_block(q, [kn_ref[0, h] for h in range(N_HEADS)], [vn_ref[0, h] for h in range(N_HEADS)],
                                 valid, tri, jnp.zeros((STACK, 1), F32), jnp.zeros((STACK, HEAD_DIM), F32))
    for g in range(n_near):
        carry, acc = _sb_stack_block(q, _page_heads(kp[g]), _page_heads(vp[g]), None, tri, carry, acc)
    acc_ref[0] = acc
    carry_ref[0] = jnp.broadcast_to(carry, (STACK, LANES))


def _sb_sample_far_kernel(pt_s, q_ref, tri_ref, kp_ref, vp_ref, acc_in_ref, carry_in_ref, acc_ref, carry_ref,
                          done_ref):
    s = pl.program_id(1)

    @pl.when(s == 0)
    def _():
        acc_ref[...] = acc_in_ref[...]
        carry_ref[...] = carry_in_ref[...]
        done_ref[0] = (jnp.max(carry_in_ref[...]) < SB_DEAD).astype(I32)

    @pl.when(done_ref[0] == 0)
    def _():
        carry, acc = _sb_stack_block(q_ref[0], _page_heads(kp_ref), _page_heads(vp_ref), None, tri_ref[...],
                                     carry_ref[0][:, 0:1], acc_ref[0])
        acc_ref[0] = acc
        carry_ref[0] = jnp.broadcast_to(carry, (STACK, LANES))
        done_ref[0] = (jnp.max(carry) < SB_DEAD).astype(I32)


def _sb_sample(page_table, layer, qb, kb_new, vb_new, cache_bk, cache_bv, *, n_new, n_near):
    b, n_pages = page_table.shape
    page = cache_bk.shape[-1]
    n_near = min(n_near, n_pages)
    tri = _later_matrix(page)
    seq3 = lambda i, *_: (i, 0, 0)
    seq4 = lambda i, *_: (i, 0, 0, 0)
    const2 = lambda *_: (0, 0)
    page_blk = (1, 1, N_HEADS, HEAD_DIM, page)
    new_blk = (1, N_HEADS, HEAD_DIM, page)
    state_shapes = [jax.ShapeDtypeStruct((b, STACK, HEAD_DIM), F32), jax.ShapeDtypeStruct((b, STACK, LANES), F32)]
    state_specs = [pl.BlockSpec((1, STACK, HEAD_DIM), seq3), pl.BlockSpec((1, STACK, LANES), seq3)]

    def near_page(g):
        return lambda i, pt: (layer, pt[i, n_pages - 1 - g], 0, 0, 0)

    near_spec = pltpu.PrefetchScalarGridSpec(
        num_scalar_prefetch=1, grid=(b,),
        in_specs=[pl.BlockSpec((1, STACK, HEAD_DIM), seq3), pl.BlockSpec(new_blk, seq4), pl.BlockSpec(new_blk, seq4),
                  pl.BlockSpec((page, page), const2)]
        + [pl.BlockSpec(page_blk, near_page(g)) for g in range(n_near)] * 2,
        out_specs=state_specs)
    acc, carry = pl.pallas_call(
        functools.partial(_sb_sample_near_kernel, n_near=n_near, n_new=n_new),
        grid_spec=near_spec, out_shape=state_shapes,
        compiler_params=_cparams(("arbitrary",)), name="sb_sample_near",
    )(page_table, qb, kb_new, vb_new, tri, *([cache_bk] * n_near), *([cache_bv] * n_near))
    n_far = n_pages - n_near
    if n_far == 0:
        return acc

    far_page = lambda i, s, pt: (layer, pt[i, n_far - 1 - s], 0, 0, 0)
    far_spec = pltpu.PrefetchScalarGridSpec(
        num_scalar_prefetch=1, grid=(b, n_far),
        in_specs=[pl.BlockSpec((1, STACK, HEAD_DIM), seq3), pl.BlockSpec((page, page), const2),
                  pl.BlockSpec(page_blk, far_page), pl.BlockSpec(page_blk, far_page)] + state_specs,
        out_specs=state_specs, scratch_shapes=[pltpu.SMEM((1,), I32)])

    def far(a, c):
        return pl.pallas_call(
            _sb_sample_far_kernel, grid_spec=far_spec, out_shape=state_shapes,
            compiler_params=_cparams(("arbitrary", "arbitrary")), name="sb_sample_far",
        )(page_table, qb, tri, cache_bk, cache_bv, a, c)[0]

    return lax.cond(jnp.max(carry) >= SB_DEAD, far, lambda a, c: a, acc, carry)


def _layer_norm(y, g, b):
    mu = jnp.mean(y, axis=1, keepdims=True)
    d = y - mu
    var = jnp.mean(d * d, axis=1, keepdims=True)
    return d * lax.rsqrt(var + LN_EPS) * g + b


def _merge_kernel(oa_ref, ob_ref, sga_ref, sgb_ref, x_ref, wba_ref, wbb_ref, wo_ref, g_ref, b_ref, h_ref, h16_ref,
                  *, alpha):
    oa = jnp.dot(oa_ref[...].astype(BF16), wba_ref[...], preferred_element_type=F32)
    ob = jnp.dot(ob_ref[...].astype(BF16), wbb_ref[...], preferred_element_type=F32)
    mix = sga_ref[...].astype(F32) * oa + sgb_ref[...].astype(F32) * ob
    y = alpha * x_ref[...] + jnp.dot(mix.astype(BF16), wo_ref[...], preferred_element_type=F32)
    h = _layer_norm(y, g_ref[...], b_ref[...])
    h_ref[...] = h
    h16_ref[...] = h.astype(BF16)


def _merge(oa, ob, sga, sgb, x2d, wba, wbb, wo, g, b, *, alpha, tm):
    t = x2d.shape[0]
    rspec = lambda n: pl.BlockSpec((tm, n), lambda i: (i, 0))
    return pl.pallas_call(
        functools.partial(_merge_kernel, alpha=alpha),
        grid=(t // tm,),
        in_specs=[rspec(W_ATT), rspec(W_ATT), rspec(D_MODEL), rspec(D_MODEL), rspec(D_MODEL),
                  _resident((W_ATT, D_MODEL)), _resident((W_ATT, D_MODEL)), _resident((D_MODEL, D_MODEL)),
                  _resident((1, D_MODEL)), _resident((1, D_MODEL))],
        out_specs=[rspec(D_MODEL), rspec(D_MODEL)],
        out_shape=[jax.ShapeDtypeStruct((t, D_MODEL), F32), jax.ShapeDtypeStruct((t, D_MODEL), BF16)],
        compiler_params=_cparams(("parallel",)), name="merge_ln1",
    )(oa, ob, sga, sgb, x2d, wba, wbb, wo, g, b)


def _top_rows(x, n):
    out = []
    rank = jnp.full(x.shape, float(n), F32)
    for r in range(n):
        m = jnp.max(x, axis=0, keepdims=True)
        out.append(m)
        hit = x == m
        rank = jnp.where(hit, float(r), rank)
        x = jnp.where(hit, -jnp.inf, x)
    return out, rank


def _peer_keys_kernel(ht_ref, wq_ref, wk_ref, cnt_ref, e1_ref, rank2_ref, e2_ref, st_ref):
    qt = jnp.dot(wq_ref[...], ht_ref[...], preferred_element_type=F32)
    st_ref[...] = jnp.dot(wk_ref[...], qt.astype(BF16), preferred_element_type=F32)

    def head(h, carry):
        base = pl.multiple_of(h * 2 * PEER_NKEYS, 2 * PEER_NKEYS)
        s1 = st_ref[pl.ds(base, PEER_NKEYS), :]
        s2 = st_ref[pl.ds(base + PEER_NKEYS, PEER_NKEYS), :]
        assert PEER_TOPK == 16
        t1, _ = _top_rows(s1, PEER_TOPK)
        t2_rows, rank2 = _top_rows(s2, PEER_TOPK)
        t2 = jnp.concatenate(t2_rows, axis=0)
        cand = jnp.concatenate(
            [t1[0] + t2, t1[1] + t2[0:8], t1[2] + t2[0:8], t1[3] + t2[0:8],
             t1[4] + t2[0:4], t1[5] + t2[0:4], t1[6] + t2[0:4], t1[7] + t2[0:4],
             jnp.concatenate(t1[8:], axis=0) + t2[0:1]], axis=0)
        cmax = t1[0] + t2[0:1]
        cur = cand
        tot = jnp.zeros_like(cmax)
        thr = cmax
        for _ in range(PEER_TOPK):
            m = jnp.max(cur, axis=0, keepdims=True)
            hit = cur == m
            tot_new = tot + jnp.sum(jnp.where(hit, 1.0, 0.0), axis=0, keepdims=True)
            thr = jnp.where(jnp.logical_and(tot < PEER_TOPK, tot_new >= PEER_TOPK), m, thr)
            tot = tot_new
            cur = jnp.where(hit, -jnp.inf, cur)
        z = jnp.sum(jnp.where(cand >= thr, jnp.exp(cand - cmax), 0.0), axis=0, keepdims=True)
        cnt = jnp.zeros_like(s1)
        for b in range(PEER_TOPK):
            cnt = cnt + jnp.where(s1 + t2_rows[b] >= thr, 1.0, 0.0)
        cnt_ref[h] = cnt
        e1_ref[h] = jnp.exp(s1 - t1[0]) / z
        rank2_ref[h] = rank2.astype(BF16)
        e2_ref[h] = jnp.exp(s2 - t2[0:1]).astype(BF16)
        return carry

    lax.fori_loop(0, PEER_HEADS, head, 0)


def _peer_keys(ht16, wq_t, wk_t, *, tn):
    t = ht16.shape[1]
    shape = lambda dt: jax.ShapeDtypeStruct((PEER_HEADS, PEER_NKEYS, t), dt)
    bspec = pl.BlockSpec((PEER_HEADS, PEER_NKEYS, tn), lambda i: (0, 0, i))
    return pl.pallas_call(
        _peer_keys_kernel,
        grid=(t // tn,),
        in_specs=[pl.BlockSpec((D_MODEL, tn), lambda i: (0, i)), _resident(wq_t.shape), _resident(wk_t.shape)],
        out_specs=[bspec, bspec, bspec, bspec],
        out_shape=[shape(F32), shape(F32), shape(BF16), shape(BF16)],
        scratch_shapes=[pltpu.VMEM((PEER_HEADS * 2 * PEER_NKEYS, tn), F32)],
        compiler_params=_cparams(("parallel",)), name="peer_keys",
    )(ht16, wq_t, wk_t)


def _gelu_tanh(x):
    return 0.5 * x * (1.0 + jnp.tanh(math.sqrt(2.0 / math.pi) * (x + 0.044715 * (x * x * x))))


def _peer_mix_kernel(ht_ref, u_ref, vt_ref, cnt_ref, e1_ref, rank2_ref, e2_ref, ft_ref, g_ref, *, i_per_chunk):
    c = pl.program_id(1)
    tn = ht_ref.shape[1]
    pack = 2 * SUBLANES

    @pl.when(c == 0)
    def _():
        ft_ref[...] = jnp.zeros(ft_ref.shape, F32)

    def row_tile(ref, h, i):
        one = jnp.broadcast_to(ref[h, pl.ds(i, 1), :], (pack, tn)).astype(BF16)
        return jnp.concatenate([one] * (PEER_NKEYS // pack), axis=0)

    n_piece = 4
    ipp = i_per_chunk // n_piece
    rows = ipp * PEER_NKEYS
    ht = ht_ref[...]
    pre = [jnp.dot(u_ref[q * rows:(q + 1) * rows, :], ht, preferred_element_type=F32) for q in range(n_piece)]
    for q in range(n_piece):
        for ii in range(ipp):
            i = c * i_per_chunk + q * ipp + ii
            w = None
            for h in range(PEER_HEADS):
                gate = jnp.where(rank2_ref[h] < row_tile(cnt_ref, h, i), e2_ref[h] * row_tile(e1_ref, h, i),
                                 jnp.zeros((), BF16))
                w = gate if w is None else w + gate
            act = _gelu_tanh(pre[q][ii * PEER_NKEYS:(ii + 1) * PEER_NKEYS, :]).astype(BF16)
            g_ref[q * rows + ii * PEER_NKEYS:q * rows + (ii + 1) * PEER_NKEYS, :] = w * act
        ft_ref[...] += jnp.dot(vt_ref[:, q * rows:(q + 1) * rows], g_ref[q * rows:(q + 1) * rows, :],
                               preferred_element_type=F32)


def _peer_mix(ht16, u16, vt16, cnt, e1, rank2, e2, *, tn, i_per_chunk):
    t = ht16.shape[1]
    ce = i_per_chunk * PEER_NKEYS
    n_exp = u16.shape[0]
    bspec = pl.BlockSpec((PEER_HEADS, PEER_NKEYS, tn), lambda i, c: (0, 0, i))
    return pl.pallas_call(
        functools.partial(_peer_mix_kernel, i_per_chunk=i_per_chunk),
        grid=(t // tn, n_exp // ce),
        in_specs=[pl.BlockSpec((D_MODEL, tn), lambda i, c: (0, i)),
                  pl.BlockSpec((ce, D_MODEL), lambda i, c: (c, 0)),
                  pl.BlockSpec((D_MODEL, ce), lambda i, c: (0, c)),
                  bspec, bspec, bspec, bspec],
        out_specs=pl.BlockSpec((D_MODEL, tn), lambda i, c: (0, i)),
        out_shape=jax.ShapeDtypeStruct((D_MODEL, t), F32),
        scratch_shapes=[pltpu.VMEM((ce, tn), BF16)],
        compiler_params=_cparams(("parallel", "arbitrary")), name="peer_mix",
    )(ht16, u16, vt16, cnt, e1, rank2, e2)


def _final_kernel(h_ref, h16_ref, f_ref, p_ref, wg_ref, wp_ref, g_ref, b_ref, o_ref, *, alpha):
    gate = jax.nn.sigmoid(jnp.dot(h16_ref[...], wg_ref[...], preferred_element_type=F32))
    e = gate * jnp.dot(p_ref[...].astype(BF16), wp_ref[...], preferred_element_type=F32)
    o_ref[...] = _layer_norm(alpha * h_ref[...] + f_ref[...] + e, g_ref[...], b_ref[...])


def _final(h, h16, f, p2d, wg, wp, g, b, *, alpha, tm):
    t = h.shape[0]
    rspec = lambda n: pl.BlockSpec((tm, n), lambda i: (i, 0))
    return pl.pallas_call(
        functools.partial(_final_kernel, alpha=alpha),
        grid=(t // tm,),
        in_specs=[rspec(D_MODEL), rspec(D_MODEL), rspec(D_MODEL), rspec(PLE_DIM),
                  _resident((D_MODEL, D_MODEL)), _resident((PLE_DIM, D_MODEL)),
                  _resident((1, D_MODEL)), _resident((1, D_MODEL))],
        out_specs=rspec(D_MODEL),
        out_shape=jax.ShapeDtypeStruct((t, D_MODEL), F32),
        compiler_params=_cparams(("parallel",)), name="ple_ln2",
    )(h, h16, f, p2d, wg, wp, g, b)


def _pick(n, prefs):
    for c in prefs:
        if n % c == 0:
            return c
    return n


def _tiles(t):
    return dict(proj=_pick(t, (512, 256, 128)), tq=_pick(t, (256, 128)), tk=_pick(t, (512, 256, 128)),
                sb=_pick(t, (256, 128)), rows=_pick(t, (256, 128)), peer=_pick(t, (512, 256, 128)))


def _layer_weights(l, w_in, w_branch, w_out, ln1_g, ln1_b, w_pq, peer_sub_keys, peer_u, peer_v, ln2_g, ln2_b,
                   w_ple_gate, w_ple_proj):
    sk = peer_sub_keys[l]
    half = PEER_DQ // 2
    blk = jnp.zeros((2 * PEER_NKEYS, PEER_DQ), F32)
    blk = blk.at[:PEER_NKEYS, :half].set(sk[0]).at[PEER_NKEYS:, half:].set(sk[1])
    return dict(
        w_perm=_permute_w_in(w_in[l]),
        wba=w_branch[l, :W_ATT].astype(BF16), wbb=w_branch[l, W_ATT:].astype(BF16), wo=w_out[l].astype(BF16),
        g1=ln1_g[l][None], b1=ln1_b[l][None], g2=ln2_g[l][None], b2=ln2_b[l][None],
        wq_t=w_pq[l].T.astype(BF16),
        wk_t=jnp.kron(jnp.eye(PEER_HEADS, dtype=F32), blk).astype(BF16),
        u16=peer_u[l].astype(BF16), vt16=peer_v[l].T.astype(BF16),
        wg=w_ple_gate[l].astype(BF16), wp=w_ple_proj[l].astype(BF16),
    )


def _token_tail(x2d, p2d, oa, ob, sga, sgb, lw, alpha):
    tl = _tiles(x2d.shape[0])
    h, h16 = _merge(oa, ob, sga, sgb, x2d, lw["wba"], lw["wbb"], lw["wo"], lw["g1"], lw["b1"], alpha=alpha,
                    tm=tl["rows"])
    ht16 = h16.T
    cnt, e1, rank2, e2 = _peer_keys(ht16, lw["wq_t"], lw["wk_t"], tn=tl["peer"])
    ft = _peer_mix(ht16, lw["u16"], lw["vt16"], cnt, e1, rank2, e2, tn=tl["peer"], i_per_chunk=8)
    return _final(h, h16, ft.T, p2d, lw["wg"], lw["wp"], lw["g2"], lw["b2"], alpha=alpha, tm=tl["rows"])


def _stack_heads(a_hm, dec_b):
    h, _, d = a_hm.shape
    return a_hm.reshape(h, dec_b, SAMPLE_ROWS, d).transpose(1, 0, 2, 3).reshape(dec_b, h * SAMPLE_ROWS, d)


def _unstack_heads(o, dec_b):
    return o.reshape(dec_b, N_HEADS, SAMPLE_ROWS, HEAD_DIM).transpose(0, 2, 1, 3).reshape(dec_b * SAMPLE_ROWS, W_ATT)


def _new_page(a_hm, dec_b, page):
    h, _, d = a_hm.shape
    a = a_hm.reshape(h, dec_b, SAMPLE_ROWS, d).transpose(1, 0, 3, 2)
    return jnp.pad(a, ((0, 0), (0, 0), (0, 0), (0, page - SAMPLE_ROWS)))


def kernel(x_prompt, x_sample, p_prompt, p_sample, cache_a_k, cache_a_v, cache_idx_k, cache_b_k, cache_b_v,
           page_table, w_in, w_branch, w_out, ln1_g, ln1_b, w_pq, peer_sub_keys, peer_u, peer_v, ln2_g, ln2_b,
           w_ple_gate, w_ple_proj):
    depth = w_in.shape[0]
    alpha = (2.0 * depth) ** 0.25
    n_batch, seq, _ = x_prompt.shape
    dec_b, dec_t, _ = x_sample.shape
    page = cache_a_k.shape[2]
    n_pages = page_table.shape[1]
    n_past = n_pages * page
    rows = SAMPLE_ROWS
    assert dec_t <= rows and page == LANES and cache_a_k.shape[3:] == (N_HEADS, HEAD_DIM)

    slot_minor = lambda c: jnp.transpose(c, (0, 1, 3, 4, 2))
    cak, cav, cbk, cbv = (slot_minor(c) for c in (cache_a_k, cache_a_v, cache_b_k, cache_b_v))
    cik = jnp.transpose(cache_idx_k, (0, 1, 3, 2))

    pos_p = jnp.arange(seq, dtype=I32)
    pos_row = n_past + jnp.minimum(jnp.arange(rows, dtype=I32), dec_t - 1)
    pos_s = jnp.tile(pos_row, dec_b)
    g_pages = _pick(n_pages, (16, 8, 4, 2, 1))

    xp = x_prompt
    xs = jnp.pad(x_sample, ((0, 0), (0, rows - dec_t), (0, 0)), mode="edge").reshape(dec_b * rows, D_MODEL)
    shapes = ((N_HEADS, HEAD_DIM), (N_HEADS, HEAD_DIM), (D_IDX,), (N_HEADS, HEAD_DIM), (N_HEADS, HEAD_DIM))
    outs_p = [[] for _ in range(5)]
    outs_s = [[] for _ in range(5)]
    for l in range(depth):
        lw = _layer_weights(l, w_in, w_branch, w_out, ln1_g, ln1_b, w_pq, peer_sub_keys, peer_u, peer_v, ln2_g,
                            ln2_b, w_ple_gate, w_ple_proj)
        tl = _tiles(seq)
        new_xp = []
        per_b = [[] for _ in range(5)]
        for b in range(n_batch):
            x2d = xp[b]
            (qa, ka, ka16, va, va16, qis, ki, ki16, wi, qb, kb, kb16, vb, vb16, sga, sgb) = _project(
                x2d, lw["w_perm"], pos_p, tl["proj"])
            oa_t = _dsa_prompt(jnp.swapaxes(qis, 1, 2), wi.T, ki16, jnp.swapaxes(qa, 1, 2), ka16,
                               jnp.swapaxes(va16, 1, 2), tq=tl["tq"], tk=tl["tk"])
            ob = _sb_prompt(qb, kb16, vb16, tq=tl["sb"], tk=tl["sb"], near_blocks=3)
            new_xp.append(_token_tail(x2d, p_prompt[l, b], oa_t.T, ob, sga, sgb, lw, alpha))
            for dst, a in zip(per_b, (ka, va, ki, kb, vb)):
                dst.append(a)
        xp = jnp.stack(new_xp)
        for dst, a, sh in zip(outs_p, per_b, shapes):
            dst.append(jnp.stack(a).reshape((n_batch, seq) + sh))

        (qa, ka, ka16, va, va16, qis, ki, ki16, wi, qb, kb, kb16, vb, vb16, sga, sgb) = _project(
            xs, lw["w_perm"], pos_s, _tiles(dec_b * rows)["proj"])
        w_rep = jnp.broadcast_to(
            wi.reshape(dec_b, rows, H_IDX).transpose(0, 2, 1).reshape(dec_b, STACK, 1), (dec_b, STACK, LANES))
        ki_new = jnp.pad(ki16.reshape(dec_b, rows, D_IDX).transpose(0, 2, 1), ((0, 0), (0, 0), (0, page - rows)))
        oa = _dsa_sample(page_table, l, _stack_heads(qis, dec_b), w_rep, _stack_heads(qa, dec_b), ki_new,
                         _new_page(ka16, dec_b, page), _new_page(va16, dec_b, page), cik, cak, cav, n_new=dec_t,
                         g_pages=g_pages)
        ob = _sb_sample(page_table, l, _stack_heads(qb, dec_b), _new_page(kb16, dec_b, page),
                        _new_page(vb16, dec_b, page), cbk, cbv, n_new=dec_t, n_near=4)
        ps = jnp.pad(p_sample[l], ((0, 0), (0, rows - dec_t), (0, 0)), mode="edge").reshape(dec_b * rows, PLE_DIM)
        xs = _token_tail(xs, ps, _unstack_heads(oa, dec_b), _unstack_heads(ob, dec_b), sga, sgb, lw, alpha)
        for dst, a, sh in zip(outs_s, (ka, va, ki, kb, vb), shapes):
            dst.append(a.reshape((dec_b, rows) + sh)[:, :dec_t])

    y_sample = xs.reshape(dec_b, rows, D_MODEL)[:, :dec_t]
    return (xp, y_sample) + tuple(jnp.stack(o) for o in outs_p) + tuple(jnp.stack(o) for o in outs_s)
```

```python
import functools
import math

import numpy as np
import jax
import jax.numpy as jnp
from jax import lax
from jax.experimental import pallas as pl
from jax.experimental.pallas import tpu as pltpu

F32 = jnp.float32
BF16 = jnp.bfloat16
I32 = jnp.int32

D_MODEL = 1024
HEAD_DIM = 64
N_HEADS = 8
W_ATT = N_HEADS * HEAD_DIM
ROPE_THETA = 500000.0
ROT_64 = 16
ROT_32 = 8
H_IDX = 8
D_IDX = 32
TOPK_MAX = 256
PEER_HEADS = 8
PEER_NKEYS = 128
PEER_DQ = 128
PEER_TOPK = 16
PLE_DIM = 256
LN_EPS = 1e-5
IN_SIZES = (W_ATT, W_ATT, W_ATT, H_IDX * D_IDX, D_IDX, H_IDX, W_ATT, W_ATT, W_ATT, D_MODEL, D_MODEL)

LANES = 128
SUBLANES = 8
VMEM_LIMIT_BYTES = 56 * 1024 * 1024
INT_MIN = -(2 ** 31)
NO_TIE_CUT = 2 ** 30
NEG_BIG = -1e30
M_INIT = -1e20
SB_DEAD = -104.0
LOG2_E = 1.4426950408889634

C_QA, C_KA, C_VA, C_QI, C_KW = 0, 512, 1024, 1536, 1792
C_QB, C_KB, C_VB, C_GA, C_GB, N_COLS = 1920, 2432, 2944, 3456, 4480, 5504

SAMPLE_ROWS = SUBLANES
STACK = N_HEADS * SAMPLE_ROWS


def _cparams(sem):
    return pltpu.CompilerParams(dimension_semantics=sem, vmem_limit_bytes=VMEM_LIMIT_BYTES)


def _resident(shape):
    zeros = (0,) * len(shape)
    return pl.BlockSpec(shape, lambda *_: zeros, pipeline_mode=pl.Buffered(1))


def _rope_tables(pos):
    posf = pos.astype(F32)

    def cos_sin(r):
        half = r // 2
        inv = ROPE_THETA ** (-(jnp.arange(half, dtype=F32) * 2.0 / r))
        ang = posf[:, None] * inv[None, :]
        return jnp.cos(ang), jnp.sin(ang)

    def lanes(cos, sin, head_dim, half, n_rot_lanes):
        lane = np.arange(LANES)
        m = lane % head_dim
        first = (m < half) & (lane < n_rot_lanes)
        second = (m >= half) & (m < 2 * half) & (lane < n_rot_lanes)
        idx = np.where(first, m, np.where(second, m - half, 0))
        c = jnp.where(first | second, cos[:, idx], 1.0)
        sa = jnp.where(first, -sin[:, idx], 0.0)
        sb = jnp.where(second, sin[:, idx], 0.0)
        return [c, sa, sb]

    c64, s64 = cos_sin(ROT_64)
    c32, s32 = cos_sin(ROT_32)
    tabs = (lanes(c64, s64, HEAD_DIM, ROT_64 // 2, LANES) + lanes(c32, s32, D_IDX, ROT_32 // 2, LANES)
            + lanes(c32, s32, D_IDX, ROT_32 // 2, D_IDX))
    return jnp.stack(tabs).astype(F32)


def _proj_kernel(x_ref, w_ref, tab_ref, qa_ref, ka_ref, ka16_ref, va_ref, va16_ref, qi_ref, ki_ref, ki16_ref,
                 wi_ref, qb_ref, kb_ref, kb16_ref, vb_ref, vb16_ref, sga_ref, sgb_ref):
    xb = x_ref[...].astype(BF16)

    def mm(c0, n):
        return jnp.dot(xb, w_ref[:, c0:c0 + n], preferred_element_type=F32)

    def rope(z, k, sh):
        out = []
        for c in range(z.shape[1] // LANES):
            zc = z[:, c * LANES:(c + 1) * LANES]
            out.append(zc * tab_ref[3 * k] + pltpu.roll(zc, LANES - sh, 1) * tab_ref[3 * k + 1]
                       + pltpu.roll(zc, sh, 1) * tab_ref[3 * k + 2])
        return out[0] if len(out) == 1 else jnp.concatenate(out, axis=1)

    def heads_major(ref, z, width):
        for h in range(z.shape[1] // width):
            ref[h] = z[:, h * width:(h + 1) * width].astype(BF16)

    heads_major(qa_ref, rope(mm(C_QA, W_ATT), 0, ROT_64 // 2), HEAD_DIM)
    z = rope(mm(C_KA, W_ATT), 0, ROT_64 // 2)
    ka_ref[...] = z
    heads_major(ka16_ref, z, HEAD_DIM)
    z = mm(C_VA, W_ATT)
    va_ref[...] = z
    heads_major(va16_ref, z, HEAD_DIM)
    heads_major(qi_ref, rope(mm(C_QI, H_IDX * D_IDX), 1, ROT_32 // 2), D_IDX)
    r = rope(mm(C_KW, LANES), 2, ROT_32 // 2)
    ki_ref[...] = r[:, :D_IDX]
    ki16_ref[...] = r[:, :D_IDX].astype(BF16)
    wi_ref[...] = r[:, D_IDX:D_IDX + H_IDX]
    heads_major(qb_ref, mm(C_QB, W_ATT), HEAD_DIM)
    z = mm(C_KB, W_ATT)
    kb_ref[...] = z
    heads_major(kb16_ref, z, HEAD_DIM)
    z = mm(C_VB, W_ATT)
    vb_ref[...] = z
    heads_major(vb16_ref, z, HEAD_DIM)
    sga_ref[...] = jax.nn.sigmoid(mm(C_GA, D_MODEL)).astype(BF16)
    sgb_ref[...] = jax.nn.sigmoid(mm(C_GB, D_MODEL)).astype(BF16)


def _permute_w_in(w_in):
    qa, ka, va, qi, ki, wi, qb, kb, vb, ga, gb = jnp.split(w_in, np.cumsum(IN_SIZES)[:-1].tolist(), axis=1)
    scale = 1.0 / math.sqrt(HEAD_DIM)
    pad = jnp.zeros((D_MODEL, LANES - D_IDX - H_IDX), w_in.dtype)
    w = jnp.concatenate([qa * (scale * LOG2_E), ka, va, qi, ki, wi, pad, qb * scale, kb, vb, ga, gb], axis=1)
    assert w.shape[1] == N_COLS
    return w.astype(BF16)


def _project(x2d, w_perm, pos, tm):
    t = x2d.shape[0]
    tabs = _rope_tables(pos)
    rm = lambda n, dt: (jax.ShapeDtypeStruct((t, n), dt), pl.BlockSpec((tm, n), lambda i: (i, 0)))
    hm = lambda d: (jax.ShapeDtypeStruct((N_HEADS, t, d), BF16), pl.BlockSpec((N_HEADS, tm, d), lambda i: (0, i, 0)))
    outs = [hm(HEAD_DIM), rm(W_ATT, F32), hm(HEAD_DIM), rm(W_ATT, F32), hm(HEAD_DIM), hm(D_IDX), rm(D_IDX, F32),
            rm(D_IDX, BF16), rm(H_IDX, F32), hm(HEAD_DIM), rm(W_ATT, F32), hm(HEAD_DIM), rm(W_ATT, F32),
            hm(HEAD_DIM), rm(D_MODEL, BF16), rm(D_MODEL, BF16)]
    return pl.pallas_call(
        _proj_kernel,
        grid=(t // tm,),
        in_specs=[pl.BlockSpec((tm, D_MODEL), lambda i: (i, 0)), _resident((D_MODEL, N_COLS)),
                  pl.BlockSpec((9, tm, LANES), lambda i: (0, i, 0))],
        out_specs=[s for _, s in outs],
        out_shape=[s for s, _ in outs],
        compiler_params=_cparams(("parallel",)),
        name="proj_rope",
    )(x2d, w_perm, tabs)


def _sortable_key(score):
    b = lax.bitcast_convert_type(score, I32)
    return jnp.where(b < 0, b ^ jnp.int32(0x7FFFFFFF), b)


def _sortable_key_inv(key):
    return jnp.where(key < 0, key ^ jnp.int32(0x7FFFFFFF), key)


def _bracket_start(count_ge, k, k_min, k_max, n_finite, shape):
    few = n_finite <= k
    neg_inf_key = _sortable_key(jnp.full(shape, -jnp.inf, F32))
    lo = jnp.where(few, neg_inf_key, k_min)
    hi = jnp.where(few, neg_inf_key + 1, k_max + 1)
    c_lo = jnp.where(few, jnp.float32(k), n_finite)
    c_hi = jnp.zeros(shape, F32)
    for zk in (0, 1):
        c = count_ge(jnp.full(shape, zk, I32))
        up = jnp.logical_and(c >= k, lo < zk)
        dn = jnp.logical_and(c < k, hi > zk)
        lo, c_lo = jnp.where(up, zk, lo), jnp.where(up, c, c_lo)
        hi, c_hi = jnp.where(dn, zk, hi), jnp.where(dn, c, c_hi)
    return lo, hi, c_lo, c_hi


def _bracket_threshold(count_ge, k, lo, hi, c_lo, c_hi):
    kf = jnp.float32(k)
    log_k = math.log(k - 0.5)

    def unfinished(lo, hi, c_lo):
        return jnp.logical_and(c_lo != kf, hi > lo + 1)

    def cond(st):
        t, lo, hi, c_lo, _ = st
        return jnp.logical_and(t < 100, jnp.max(jnp.where(unfinished(lo, hi, c_lo), 1.0, 0.0)) > 0.0)

    def body(st):
        t, lo, hi, c_lo, c_hi = st
        lo_f = lax.bitcast_convert_type(_sortable_key_inv(lo), F32)
        hi_f = lax.bitcast_convert_type(_sortable_key_inv(hi), F32)
        la, lb = jnp.log(c_lo), jnp.log(jnp.maximum(c_hi, 0.25))
        frac = (la - log_k) / (la - lb)
        mid_f = lo_f + (hi_f - lo_f) * frac
        cand_i = _sortable_key(jnp.where(mid_f == mid_f, mid_f, lo_f))
        cand_b = (lo >> 1) + (hi >> 1) + (lo & hi & 1)
        cand = jnp.where(t % 3 == 2, cand_b, cand_i)
        cand = jnp.minimum(jnp.maximum(cand, lo + 1), hi - 1)
        cand = jnp.where(unfinished(lo, hi, c_lo), cand, lo)
        c = count_ge(cand)
        up = c >= kf
        return (t + 1, jnp.where(up, cand, lo), jnp.where(up, hi, cand), jnp.where(up, c, c_lo),
                jnp.where(up, c_hi, c))

    _, lo, _, c_lo, c_hi = lax.while_loop(cond, body, (jnp.int32(0), lo, hi, c_lo, c_hi))
    return lo, c_lo, c_hi


def _tie_cut_search(count_tied_le, need, active, last_idx, n_tied):
    hi = jnp.broadcast_to(last_idx, need.shape).astype(I32)
    lo = jnp.where(active, jnp.int32(-1), hi - 1)

    def unfinished(lo, hi, h_hi):
        return jnp.logical_and(h_hi != need, hi > lo + 1)

    def cond(st):
        t, lo, hi, _, h_hi = st
        return jnp.logical_and(t < 100, jnp.max(jnp.where(unfinished(lo, hi, h_hi), 1.0, 0.0)) > 0.0)

    def body(st):
        t, lo, hi, h_lo, h_hi = st
        span = (hi - lo).astype(F32)
        step = jnp.ceil(span * (need - h_lo - 0.5) / jnp.maximum(h_hi - h_lo, 1.0)).astype(I32)
        cand = jnp.where(t % 3 == 2, lo + ((hi - lo) >> 1), lo + step)
        cand = jnp.minimum(jnp.maximum(cand, lo + 1), hi - 1)
        cand = jnp.where(unfinished(lo, hi, h_hi), cand, hi)
        c = count_tied_le(cand)
        dn = c >= need
        return (t + 1, jnp.where(dn, lo, cand), jnp.where(dn, cand, hi), jnp.where(dn, h_lo, c),
                jnp.where(dn, c, h_hi))

    _, _, hi, _, _ = lax.while_loop(cond, body, (jnp.int32(0), lo, hi, jnp.zeros_like(need), n_tied))
    return hi


def _reduce_keys(x, op, width=64):
    n, w = x.shape
    if n % width == 0 and n > width:
        x = op(x.reshape(n // width, width, w), axis=0)
    return op(x, axis=0, keepdims=True)


def _selected(key, vt, jt, idx, on, off):
    return jnp.where(key > vt, on, jnp.where(key == vt, jnp.where(idx <= jt, on, off), off))


def _tri_pairs(t, tq, tk, reverse, j_window=None):
    qi, kj, first, last = [], [], [], []
    for i in range(t // tq):
        jd = ((i + 1) * tq - 1) // tk
        js = list(range(jd + 1))
        if j_window is not None:
            js = [j for j in js if j_window[0] <= jd - j < j_window[1]]
        if reverse:
            js = js[::-1]
        for n, j in enumerate(js):
            qi.append(i), kj.append(j), first.append(int(n == 0)), last.append(int(n == len(js) - 1))
    return tuple(np.asarray(a, np.int32) for a in (qi, kj, first, last))


def _dsa_prompt_kernel(qi_s, kj_s, first_s, last_s, qis_ref, wi_ref, ki_ref, qa_ref, ka_ref, va_ref, o_ref,
                       sc_ref, vthr_ref, jthr_ref, m_ref, l_ref, acc_ref, *, tq, tk, n_top):
    p = pl.program_id(0)
    i = qi_s[p]
    j = kj_s[p]
    q_idx = i * tq + lax.broadcasted_iota(I32, (tk, tq), 1)
    k_off = lax.broadcasted_iota(I32, (tk, tq), 0)

    @pl.when(first_s[p] == 1)
    def _():
        m_ref[...] = jnp.full(m_ref.shape, M_INIT, F32)
        l_ref[...] = jnp.zeros(l_ref.shape, F32)
        acc_ref[...] = jnp.zeros(acc_ref.shape, F32)
        n_chunks = ((i + 1) * tq + tk - 1) // tk
        w = wi_ref[...]

        cw = min(tk, 64)

        def score_chunk(c, carry):
            hi_s, lo_s = carry
            k_blk = ki_ref[pl.ds(pl.multiple_of(c * tk, tk), tk), :]
            score = jnp.zeros((tk, tq), F32)
            for h in range(H_IDX):
                s = jnp.dot(k_blk, qis_ref[h], preferred_element_type=F32)
                score = score + w[h:h + 1, :] * jnp.maximum(s, 0.0)
            score = jnp.where(c * tk + k_off <= q_idx, score, -jnp.inf)
            sc_ref[c] = _sortable_key(score)
            for r in range(tk // cw):
                blk = score[r * cw:(r + 1) * cw, :]
                hi_s = jnp.maximum(hi_s, blk)
                lo_s = jnp.minimum(lo_s, jnp.where(blk > -jnp.inf, blk, jnp.inf))
            return hi_s, lo_s

        hi_s, lo_s = lax.fori_loop(0, n_chunks, score_chunk,
                                   (jnp.full((cw, tq), -jnp.inf, F32), jnp.full((cw, tq), jnp.inf, F32)))
        k_max = _sortable_key(jnp.max(hi_s, axis=0, keepdims=True))
        k_min = _sortable_key(jnp.min(lo_s, axis=0, keepdims=True))

        def key_count(ones_fn):
            def body(c, acc):
                for r in range(tk // cw):
                    acc = acc + ones_fn(sc_ref[c, r * cw:(r + 1) * cw, :], c * tk + r * cw)
                return acc
            acc = lax.fori_loop(0, n_chunks, body, jnp.zeros((cw, tq), F32))
            return jnp.sum(acc, axis=0, keepdims=True)

        def count_ge(cand):
            cand_b = jnp.broadcast_to(cand, (cw, tq))
            return key_count(lambda blk, k0: jnp.where(blk >= cand_b, 1.0, 0.0))

        n_causal = (i * tq + lax.broadcasted_iota(I32, (1, tq), 1) + 1).astype(F32)
        lo0, hi0, c_lo0, c_hi0 = _bracket_start(count_ge, n_top, k_min, k_max, n_causal, (1, tq))
        v, cnt, above = _bracket_threshold(count_ge, n_top, lo0, hi0, c_lo0, c_hi0)
        vthr_ref[...] = jnp.broadcast_to(v, (SUBLANES, tq))
        jthr_ref[...] = jnp.full((SUBLANES, tq), NO_TIE_CUT, I32)

        @pl.when(jnp.max(cnt) > n_top)
        def _():
            need = n_top - above
            tied = cnt > n_top
            v_b = jnp.broadcast_to(v, (cw, tq))
            off = lax.broadcasted_iota(I32, (cw, tq), 0)

            def count_tied_le(x):
                x_b = jnp.broadcast_to(x, (cw, tq))
                return key_count(
                    lambda blk, k0: jnp.where(blk == v_b, jnp.where(k0 + off <= x_b, 1.0, 0.0), 0.0))
            cut = _tie_cut_search(count_tied_le, need, tied, n_chunks * tk - 1, cnt - above)
            jthr_ref[...] = jnp.broadcast_to(jnp.where(tied, cut, NO_TIE_CUT), (SUBLANES, tq))

    k_idx = j * tk + k_off
    sel = jnp.where(k_idx <= q_idx,
                    _selected(sc_ref[j], vthr_ref[0:1, :], jthr_ref[0:1, :], k_idx, 0.0, NEG_BIG), NEG_BIG)
    m_all = m_ref[...]
    l_all = l_ref[...]
    m_out, l_out = [], []

    def logits(h):
        return jnp.dot(ka_ref[h], qa_ref[h], preferred_element_type=F32) + sel

    ahead = 2
    pending = [logits(h) for h in range(ahead)]
    for h in range(N_HEADS):
        if h % ahead == 0 and h + ahead < N_HEADS:
            pending += [logits(h + ahead + a) for a in range(ahead)]
        s = pending.pop(0)
        m_prev = m_all[h:h + 1, :]
        m_new = jnp.maximum(m_prev, _reduce_keys(s, jnp.max))
        pr = jnp.exp2(s - m_new)
        alpha = jnp.exp2(m_prev - m_new)
        l_out.append(alpha * l_all[h:h + 1, :] + _reduce_keys(pr, jnp.sum))
        acc_ref[h] = alpha * acc_ref[h] + jnp.dot(va_ref[h], pr.astype(BF16), preferred_element_type=F32)
        m_out.append(m_new)
    m_ref[...] = jnp.concatenate(m_out, axis=0)
    l_ref[...] = jnp.concatenate(l_out, axis=0)

    @pl.when(last_s[p] == 1)
    def _():
        for h in range(N_HEADS):
            o_ref[h * HEAD_DIM:(h + 1) * HEAD_DIM, :] = (acc_ref[h] / l_ref[h:h + 1, :]).astype(o_ref.dtype)


def _dsa_prompt(qis_t, wi_t, ki16, qa_t, ka16, va_t, *, tq, tk):
    t = ki16.shape[0]
    n_top = min(TOPK_MAX, t // 4)
    pairs = _tri_pairs(t, tq, tk, reverse=False)
    kern = functools.partial(_dsa_prompt_kernel, tq=tq, tk=tk, n_top=n_top)
    qlane = lambda p, qi, kj, fi, la: (0, 0, qi[p])
    grid_spec = pltpu.PrefetchScalarGridSpec(
        num_scalar_prefetch=4,
        grid=(int(pairs[0].shape[0]),),
        in_specs=[
            pl.BlockSpec((H_IDX, D_IDX, tq), qlane),
            pl.BlockSpec((H_IDX, tq), lambda p, qi, kj, fi, la: (0, qi[p])),
            pl.BlockSpec((t, D_IDX), lambda p, qi, kj, fi, la: (0, 0)),
            pl.BlockSpec((N_HEADS, HEAD_DIM, tq), qlane),
            pl.BlockSpec((N_HEADS, tk, HEAD_DIM), lambda p, qi, kj, fi, la: (0, kj[p], 0)),
            pl.BlockSpec((N_HEADS, HEAD_DIM, tk), lambda p, qi, kj, fi, la: (0, 0, kj[p])),
        ],
        out_specs=pl.BlockSpec((W_ATT, tq), lambda p, qi, kj, fi, la: (0, qi[p])),
        scratch_shapes=[
            pltpu.VMEM((t // tk, tk, tq), I32),
            pltpu.VMEM((SUBLANES, tq), I32),
            pltpu.VMEM((SUBLANES, tq), I32),
            pltpu.VMEM((N_HEADS, tq), F32),
            pltpu.VMEM((N_HEADS, tq), F32),
            pltpu.VMEM((N_HEADS, HEAD_DIM, tq), F32),
        ],
    )
    return pl.pallas_call(
        kern, grid_spec=grid_spec, out_shape=jax.ShapeDtypeStruct((W_ATT, t), BF16),
        compiler_params=_cparams(("arbitrary",)), name="dsa_prompt",
    )(*pairs, qis_t, wi_t, ki16, qa_t, ka16, va_t)


def _log_sigmoid_neg(z):
    return -(jnp.maximum(z, 0.0) + jnp.log1p(jnp.exp(-jnp.abs(z))))


def _sum_of_later(lg, tri):
    hi = lg.astype(BF16)
    lo = (lg - hi.astype(F32)).astype(BF16)
    return jnp.dot(hi, tri, preferred_element_type=F32) + jnp.dot(lo, tri, preferred_element_type=F32)


def _later_matrix(n):
    r = np.arange(n)
    return jnp.asarray((r[:, None] > r[None, :]).astype(np.float32)).astype(BF16)


def _sb_prompt_kernel(qi_s, kj_s, first_s, last_s, q_ref, k_ref, v_ref, tri_ref, *rest, tq, tk, resume):
    if resume:
        acc_in_ref, carry_in_ref, acc_out_ref, carry_out_ref, carry_ref, acc_ref, done_ref = rest
    else:
        acc_out_ref, carry_out_ref, carry_ref, acc_ref, done_ref = rest
    p = pl.program_id(0)
    i = qi_s[p]
    j = kj_s[p]

    @pl.when(first_s[p] == 1)
    def _():
        if resume:
            worst = None
            for h in range(N_HEADS):
                acc_ref[h] = acc_in_ref[:, h * HEAD_DIM:(h + 1) * HEAD_DIM]
                c = carry_in_ref[:, h:h + 1]
                carry_ref[h] = c
                worst = c if worst is None else jnp.maximum(worst, c)
            done_ref[0] = (jnp.max(worst) < SB_DEAD).astype(I32)
        else:
            carry_ref[...] = jnp.zeros(carry_ref.shape, F32)
            acc_ref[...] = jnp.zeros(acc_ref.shape, F32)
            done_ref[0] = 0

    @pl.when(done_ref[0] == 0)
    def _():
        row = i * tq + lax.broadcasted_iota(I32, (tq, tk), 0)
        col = j * tk + lax.broadcasted_iota(I32, (tq, tk), 1)
        mask = col < row
        tri = tri_ref[...]
        zs = [lax.dot_general(q_ref[h], k_ref[h], (((1,), (1,)), ((), ())), preferred_element_type=F32)
              for h in range(N_HEADS)]
        lsms = [_log_sigmoid_neg(z) for z in zs]
        lgs = [jnp.where(mask, lsm, 0.0) for lsm in lsms]
        laters = [_sum_of_later(lg, tri) for lg in lgs]
        worst = None
        for h in range(N_HEADS):
            lg, later = lgs[h], laters[h]
            carry = carry_ref[h]
            a = jnp.where(mask, jnp.exp(lsms[h] + zs[h] + later + carry), 0.0)
            acc_ref[h] = acc_ref[h] + jnp.dot(a.astype(BF16), v_ref[h], preferred_element_type=F32)
            carry = carry + later[:, 0:1] + lg[:, 0:1]
            carry_ref[h] = carry
            worst = carry if worst is None else jnp.maximum(worst, carry)
        done_ref[0] = (jnp.max(worst) < SB_DEAD).astype(I32)

    @pl.when(last_s[p] == 1)
    def _():
        for h in range(N_HEADS):
            acc_out_ref[:, h * HEAD_DIM:(h + 1) * HEAD_DIM] = acc_ref[h]
            carry_out_ref[:, h:h + 1] = carry_ref[h]


def _sb_prompt_call(pairs, qb, kb16, vb16, tri, state, *, tq, tk):
    t = qb.shape[1]
    resume = state is not None
    qmap = lambda p, qi, kj, fi, la: (0, qi[p], 0)
    kmap = lambda p, qi, kj, fi, la: (0, kj[p], 0)
    rmap = lambda p, qi, kj, fi, la: (qi[p], 0)
    in_specs = [pl.BlockSpec((N_HEADS, tq, HEAD_DIM), qmap), pl.BlockSpec((N_HEADS, tk, HEAD_DIM), kmap),
                pl.BlockSpec((N_HEADS, tk, HEAD_DIM), kmap), pl.BlockSpec((tk, tk), lambda p, *_: (0, 0))]
    args = [qb, kb16, vb16, tri]
    if resume:
        in_specs += [pl.BlockSpec((tq, W_ATT), rmap), pl.BlockSpec((tq, N_HEADS), rmap)]
        args += list(state)
    grid_spec = pltpu.PrefetchScalarGridSpec(
        num_scalar_prefetch=4,
        grid=(int(pairs[0].shape[0]),),
        in_specs=in_specs,
        out_specs=[pl.BlockSpec((tq, W_ATT), rmap), pl.BlockSpec((tq, N_HEADS), rmap)],
        scratch_shapes=[pltpu.VMEM((N_HEADS, tq, 1), F32), pltpu.VMEM((N_HEADS, tq, HEAD_DIM), F32),
                        pltpu.SMEM((1,), I32)],
    )
    return pl.pallas_call(
        functools.partial(_sb_prompt_kernel, tq=tq, tk=tk, resume=resume),
        grid_spec=grid_spec,
        out_shape=[jax.ShapeDtypeStruct((t, W_ATT), F32), jax.ShapeDtypeStruct((t, N_HEADS), F32)],
        input_output_aliases=({4 + 4: 0, 4 + 5: 1} if resume else {}),
        compiler_params=_cparams(("arbitrary",)), name="sb_prompt_far" if resume else "sb_prompt_near",
    )(*pairs, *args)


def _sb_prompt(qb, kb16, vb16, *, tq, tk, near_blocks):
    t = qb.shape[1]
    tri = _later_matrix(tk)
    near = _tri_pairs(t, tq, tk, True, (0, near_blocks))
    far = _tri_pairs(t, tq, tk, True, (near_blocks, t))
    acc, carry = _sb_prompt_call(near, qb, kb16, vb16, tri, None, tq=tq, tk=tk)
    if far[0].shape[0] == 0:
        return acc
    first_row = int(far[0][0]) * tq
    alive = jnp.max(carry[first_row:]) >= SB_DEAD
    return lax.cond(
        alive,
        lambda a, c: _sb_prompt_call(far, qb, kb16, vb16, tri, (a, c), tq=tq, tk=tk)[0],
        lambda a, c: a, acc, carry)


def _dsa_sample_kernel(pt_s, qis_ref, w_ref, qa_ref, kin_ref, kan_ref, van_ref, *rest, g_pages, n_pages, n_new,
                       n_top):
    kip = rest[:g_pages]
    kap = rest[g_pages:2 * g_pages]
    vap = rest[2 * g_pages:3 * g_pages]
    o_ref, sc_ref, vthr_ref, jthr_ref, m_ref, l_ref, acc_ref = rest[3 * g_pages:]
    s = pl.program_id(1)
    n_steps = n_pages // g_pages
    n_blocks = n_pages + 1
    r_new = jnp.minimum(lax.broadcasted_iota(I32, (SAMPLE_ROWS, LANES), 0), n_new - 1)
    lane = lax.broadcasted_iota(I32, (SAMPLE_ROWS, LANES), 1)

    def scores(k_t):
        n = k_t.shape[1] // LANES
        s1 = jnp.dot(qis_ref[0], k_t, preferred_element_type=F32)
        wt = w_ref[0] if n == 1 else jnp.concatenate([w_ref[0]] * n, axis=1)
        t = wt * jnp.maximum(s1, 0.0)
        score = t[0:SAMPLE_ROWS]
        for h in range(1, H_IDX):
            score = score + t[h * SAMPLE_ROWS:(h + 1) * SAMPLE_ROWS]
        return score

    @pl.when(s < n_steps)
    def _():
        key = _sortable_key(scores(jnp.concatenate([r[0, 0].astype(BF16) for r in kip], axis=1)))
        for g in range(g_pages):
            sc_ref[s * g_pages + g] = key[:, g * LANES:(g + 1) * LANES]

    @pl.when(s == n_steps)
    def _():
        m_ref[...] = jnp.full(m_ref.shape, M_INIT, F32)
        l_ref[...] = jnp.zeros(l_ref.shape, F32)
        acc_ref[...] = jnp.zeros(acc_ref.shape, F32)
        key = _sortable_key(scores(kin_ref[0]))
        neg_inf_key = _sortable_key(jnp.full((SAMPLE_ROWS, LANES), -jnp.inf, F32))
        sc_ref[n_pages] = jnp.where(lane < n_new, jnp.where(lane <= r_new, key, neg_inf_key), jnp.int32(INT_MIN))
        keys = sc_ref[...]
        idx = (lax.broadcasted_iota(I32, (n_blocks, SAMPLE_ROWS, LANES), 0) * LANES
               + lax.broadcasted_iota(I32, (n_blocks, SAMPLE_ROWS, LANES), 2))

        def count(hit):
            ones = jnp.where(hit, 1.0, 0.0)
            parts = [jnp.sum(ones[b0:b0 + 16], axis=0) for b0 in range(0, n_blocks, 16)]
            while len(parts) > 1:
                parts = [sum(parts[k:k + 2]) for k in range(0, len(parts), 2)]
            return jnp.sum(parts[0], axis=1, keepdims=True)

        def count_ge(cand):
            return count(keys >= cand[None])

        scores_back = lax.bitcast_convert_type(_sortable_key_inv(keys), F32)
        finite = keys > neg_inf_key[None]

        def over_keys(x, op):
            return op(op(x, axis=0), axis=1, keepdims=True)

        k_max = _sortable_key(over_keys(jnp.where(finite, scores_back, -jnp.inf), jnp.max))
        k_min = _sortable_key(over_keys(jnp.where(finite, scores_back, jnp.inf), jnp.min))
        shape = (SAMPLE_ROWS, 1)
        lo0, hi0, c_lo0, c_hi0 = _bracket_start(count_ge, n_top, k_min, k_max, count(finite), shape)
        v, cnt, above = _bracket_threshold(count_ge, n_top, lo0, hi0, c_lo0, c_hi0)
        vthr_ref[...] = jnp.broadcast_to(v, (SAMPLE_ROWS, LANES))
        jthr_ref[...] = jnp.full((SAMPLE_ROWS, LANES), NO_TIE_CUT, I32)

        @pl.when(jnp.max(cnt) > n_top)
        def _():
            tied = cnt > n_top

            def count_tied_le(x):
                return count(jnp.logical_and(keys == v[None], idx <= x[None]))
            cut = _tie_cut_search(count_tied_le, n_top - above, tied, n_blocks * LANES - 1, cnt - above)
            jthr_ref[...] = jnp.broadcast_to(jnp.where(tied, cut, NO_TIE_CUT), (SAMPLE_ROWS, LANES))

    def chosen(blk_idx):
        return _selected(sc_ref[blk_idx], vthr_ref[...], jthr_ref[...], blk_idx * LANES + lane, 0.0, NEG_BIG)

    def attend(k_t, v_t, sel):
        q = qa_ref[0]
        sc = jnp.concatenate(
            [jnp.dot(q[h * SAMPLE_ROWS:(h + 1) * SAMPLE_ROWS], k_t[h], preferred_element_type=F32) + sel
             for h in range(N_HEADS)], axis=0)
        m_prev = m_ref[...]
        m_new = jnp.maximum(m_prev, jnp.max(sc, axis=1, keepdims=True))
        pr = jnp.exp2(sc - m_new)
        alpha = jnp.exp2(m_prev - m_new)
        l_ref[...] = alpha * l_ref[...] + jnp.sum(pr, axis=1, keepdims=True)
        pv = jnp.concatenate(
            [lax.dot_general(pr[h * SAMPLE_ROWS:(h + 1) * SAMPLE_ROWS].astype(BF16), v_t[h],
                             (((1,), (1,)), ((), ())), preferred_element_type=F32) for h in range(N_HEADS)], axis=0)
        acc_ref[...] = alpha * acc_ref[...] + pv
        m_ref[...] = m_new

    def head_pages(refs, h):
        return jnp.concatenate([r[0, 0, h].astype(BF16) for r in refs], axis=1)

    @pl.when(jnp.logical_and(s > n_steps, s <= 2 * n_steps))
    def _():
        p0 = (s - n_steps - 1) * g_pages
        attend([head_pages(kap, h) for h in range(N_HEADS)], [head_pages(vap, h) for h in range(N_HEADS)],
               jnp.concatenate([chosen(p0 + g) for g in range(g_pages)], axis=1))

    @pl.when(s == 2 * n_steps + 1)
    def _():
        sel = jnp.where(jnp.logical_and(lane <= r_new, lane < n_new), chosen(n_pages), NEG_BIG)
        attend([kan_ref[0, h] for h in range(N_HEADS)], [van_ref[0, h] for h in range(N_HEADS)], sel)
        o_ref[0] = (acc_ref[...] / l_ref[...]).astype(o_ref.dtype)


def _dsa_sample(page_table, layer, qis, w_rep, qa, ki_new, ka_new, va_new, cache_ik, cache_ak, cache_av, *,
                n_new, g_pages):
    b, n_pages = page_table.shape
    page = cache_ak.shape[-1]
    assert page == LANES and n_pages % g_pages == 0
    n_steps = n_pages // g_pages
    n_top = min(TOPK_MAX, (n_pages * page + n_new) // 4)
    kern = functools.partial(_dsa_sample_kernel, g_pages=g_pages, n_pages=n_pages, n_new=n_new, n_top=n_top)
    seq3 = lambda i, s, pt: (i, 0, 0)
    seq4 = lambda i, s, pt: (i, 0, 0, 0)

    def score_page(g):
        return lambda i, s, pt: (layer, pt[i, jnp.minimum(s, n_steps - 1) * g_pages + g], 0, 0)

    def attend_page(g):
        return lambda i, s, pt: (layer, pt[i, jnp.clip(s - n_steps - 1, 0, n_steps - 1) * g_pages + g], 0, 0, 0)

    page_blk = (1, 1, N_HEADS, HEAD_DIM, page)
    grid_spec = pltpu.PrefetchScalarGridSpec(
        num_scalar_prefetch=1,
        grid=(b, 2 * n_steps + 2),
        in_specs=[pl.BlockSpec((1, STACK, D_IDX), seq3), pl.BlockSpec((1, STACK, LANES), seq3),
                  pl.BlockSpec((1, STACK, HEAD_DIM), seq3), pl.BlockSpec((1, D_IDX, page), seq3),
                  pl.BlockSpec((1, N_HEADS, HEAD_DIM, page), seq4), pl.BlockSpec((1, N_HEADS, HEAD_DIM, page), seq4)]
        + [pl.BlockSpec((1, 1, D_IDX, page), score_page(g)) for g in range(g_pages)]
        + [pl.BlockSpec(page_blk, attend_page(g)) for g in range(g_pages)] * 2,
        out_specs=pl.BlockSpec((1, STACK, HEAD_DIM), seq3),
        scratch_shapes=[
            pltpu.VMEM((n_pages + 1, SAMPLE_ROWS, LANES), I32),
            pltpu.VMEM((SAMPLE_ROWS, LANES), I32),
            pltpu.VMEM((SAMPLE_ROWS, LANES), I32),
            pltpu.VMEM((STACK, 1), F32),
            pltpu.VMEM((STACK, 1), F32),
            pltpu.VMEM((STACK, HEAD_DIM), F32),
        ],
    )
    return pl.pallas_call(
        kern, grid_spec=grid_spec, out_shape=jax.ShapeDtypeStruct((b, STACK, HEAD_DIM), BF16),
        compiler_params=_cparams(("arbitrary", "arbitrary")), name="dsa_sample",
    )(page_table, qis, w_rep, qa, ki_new, ka_new, va_new,
      *([cache_ik] * g_pages), *([cache_ak] * g_pages), *([cache_av] * g_pages))


def _sb_stack_block(q, k_t, v_t, valid, tri, carry, acc):
    z = jnp.concatenate([jnp.dot(q[h * SAMPLE_ROWS:(h + 1) * SAMPLE_ROWS], k_t[h], preferred_element_type=F32)
                         for h in range(N_HEADS)], axis=0)
    lsm = _log_sigmoid_neg(z)
    lg = lsm if valid is None else jnp.where(valid, lsm, 0.0)
    later = _sum_of_later(lg, tri)
    a = jnp.exp(lsm + z + later + carry)
    if valid is not None:
        a = jnp.where(valid, a, 0.0)
    pv = jnp.concatenate(
        [lax.dot_general(a[h * SAMPLE_ROWS:(h + 1) * SAMPLE_ROWS].astype(BF16), v_t[h],
                         (((1,), (1,)), ((), ())), preferred_element_type=F32) for h in range(N_HEADS)], axis=0)
    return carry + jnp.sum(lg, axis=1, keepdims=True), acc + pv


def _page_heads(ref):
    return [ref[0, 0, h].astype(BF16) for h in range(N_HEADS)]


def _sb_sample_near_kernel(pt_s, q_ref, kn_ref, vn_ref, tri_ref, *rest, n_near, n_new):
    kp = rest[:n_near]
    vp = rest[n_near:2 * n_near]
    acc_ref, carry_ref = rest[2 * n_near:]
    q = q_ref[0]
    tri = tri_ref[...]
    row = lax.broadcasted_iota(I32, (STACK, LANES), 0)
    lane = lax.broadcasted_iota(I32, (STACK, LANES), 1)
    valid = lane < jnp.minimum(row % SAMPLE_ROWS, n_new - 1)
    carry, acc = _sb_stack_block(q, [kn_ref[0, h] for h in range(N_HEADS)], [vn_ref[0, h] for h in range(N_HEADS)],
                                 valid, tri, jnp.zeros((STACK, 1), F32), jnp.zeros((STACK, HEAD_DIM), F32))
    for g in range(n_near):
        carry, acc = _sb_stack_block(q, _page_heads(kp[g]), _page_heads(vp[g]), None, tri, carry, acc)
    acc_ref[0] = acc
    carry_ref[0] = jnp.broadcast_to(carry, (STACK, LANES))


def _sb_sample_far_kernel(pt_s, q_ref, tri_ref, kp_ref, vp_ref, acc_in_ref, carry_in_ref, acc_ref, carry_ref,
                          done_ref):
    s = pl.program_id(1)

    @pl.when(s == 0)
    def _():
        acc_ref[...] = acc_in_ref[...]
        carry_ref[...] = carry_in_ref[...]
        done_ref[0] = (jnp.max(carry_in_ref[...]) < SB_DEAD).astype(I32)

    @pl.when(done_ref[0] == 0)
    def _():
        carry, acc = _sb_stack_block(q_ref[0], _page_heads(kp_ref), _page_heads(vp_ref), None, tri_ref[...],
                                     carry_ref[0][:, 0:1], acc_ref[0])
        acc_ref[0] = acc
        carry_ref[0] = jnp.broadcast_to(carry, (STACK, LANES))
        done_ref[0] = (jnp.max(carry) < SB_DEAD).astype(I32)


def _sb_sample(page_table, layer, qb, kb_new, vb_new, cache_bk, cache_bv, *, n_new, n_near):
    b, n_pages = page_table.shape
    page = cache_bk.shape[-1]
    n_near = min(n_near, n_pages)
    tri = _later_matrix(page)
    seq3 = lambda i, *_: (i, 0, 0)
    seq4 = lambda i, *_: (i, 0, 0, 0)
    const2 = lambda *_: (0, 0)
    page_blk = (1, 1, N_HEADS, HEAD_DIM, page)
    new_blk = (1, N_HEADS, HEAD_DIM, page)
    state_shapes = [jax.ShapeDtypeStruct((b, STACK, HEAD_DIM), F32), jax.ShapeDtypeStruct((b, STACK, LANES), F32)]
    state_specs = [pl.BlockSpec((1, STACK, HEAD_DIM), seq3), pl.BlockSpec((1, STACK, LANES), seq3)]

    def near_page(g):
        return lambda i, pt: (layer, pt[i, n_pages - 1 - g], 0, 0, 0)

    near_spec = pltpu.PrefetchScalarGridSpec(
        num_scalar_prefetch=1, grid=(b,),
        in_specs=[pl.BlockSpec((1, STACK, HEAD_DIM), seq3), pl.BlockSpec(new_blk, seq4), pl.BlockSpec(new_blk, seq4),
                  pl.BlockSpec((page, page), const2)]
        + [pl.BlockSpec(page_blk, near_page(g)) for g in range(n_near)] * 2,
        out_specs=state_specs)
    acc, carry = pl.pallas_call(
        functools.partial(_sb_sample_near_kernel, n_near=n_near, n_new=n_new),
        grid_spec=near_spec, out_shape=state_shapes,
        compiler_params=_cparams(("arbitrary",)), name="sb_sample_near",
    )(page_table, qb, kb_new, vb_new, tri, *([cache_bk] * n_near), *([cache_bv] * n_near))
    n_far = n_pages - n_near
    if n_far == 0:
        return acc

    far_page = lambda i, s, pt: (layer, pt[i, n_far - 1 - s], 0, 0, 0)
    far_spec = pltpu.PrefetchScalarGridSpec(
        num_scalar_prefetch=1, grid=(b, n_far),
        in_specs=[pl.BlockSpec((1, STACK, HEAD_DIM), seq3), pl.BlockSpec((page, page), const2),
                  pl.BlockSpec(page_blk, far_page), pl.BlockSpec(page_blk, far_page)] + state_specs,
        out_specs=state_specs, scratch_shapes=[pltpu.SMEM((1,), I32)])

    def far(a, c):
        return pl.pallas_call(
            _sb_sample_far_kernel, grid_spec=far_spec, out_shape=state_shapes,
            compiler_params=_cparams(("arbitrary", "arbitrary")), name="sb_sample_far",
        )(page_table, qb, tri, cache_bk, cache_bv, a, c)[0]

    return lax.cond(jnp.max(carry) >= SB_DEAD, far, lambda a, c: a, acc, carry)


def _layer_norm(y, g, b):
    mu = jnp.mean(y, axis=1, keepdims=True)
    d = y - mu
    var = jnp.mean(d * d, axis=1, keepdims=True)
    return d * lax.rsqrt(var + LN_EPS) * g + b


def _merge_kernel(oa_ref, ob_ref, sga_ref, sgb_ref, x_ref, wba_ref, wbb_ref, wo_ref, g_ref, b_ref, h_ref, h16_ref,
                  *, alpha):
    oa = jnp.dot(oa_ref[...].astype(BF16), wba_ref[...], preferred_element_type=F32)
    ob = jnp.dot(ob_ref[...].astype(BF16), wbb_ref[...], preferred_element_type=F32)
    mix = sga_ref[...].astype(F32) * oa + sgb_ref[...].astype(F32) * ob
    y = alpha * x_ref[...] + jnp.dot(mix.astype(BF16), wo_ref[...], preferred_element_type=F32)
    h = _layer_norm(y, g_ref[...], b_ref[...])
    h_ref[...] = h
    h16_ref[...] = h.astype(BF16)


def _merge(oa, ob, sga, sgb, x2d, wba, wbb, wo, g, b, *, alpha, tm):
    t = x2d.shape[0]
    rspec = lambda n: pl.BlockSpec((tm, n), lambda i: (i, 0))
    return pl.pallas_call(
        functools.partial(_merge_kernel, alpha=alpha),
        grid=(t // tm,),
        in_specs=[rspec(W_ATT), rspec(W_ATT), rspec(D_MODEL), rspec(D_MODEL), rspec(D_MODEL),
                  _resident((W_ATT, D_MODEL)), _resident((W_ATT, D_MODEL)), _resident((D_MODEL, D_MODEL)),
                  _resident((1, D_MODEL)), _resident((1, D_MODEL))],
        out_specs=[rspec(D_MODEL), rspec(D_MODEL)],
        out_shape=[jax.ShapeDtypeStruct((t, D_MODEL), F32), jax.ShapeDtypeStruct((t, D_MODEL), BF16)],
        compiler_params=_cparams(("parallel",)), name="merge_ln1",
    )(oa, ob, sga, sgb, x2d, wba, wbb, wo, g, b)


def _top_rows(x, n):
    out = []
    rank = jnp.full(x.shape, float(n), F32)
    for r in range(n):
        m = jnp.max(x, axis=0, keepdims=True)
        out.append(m)
        hit = x == m
        rank = jnp.where(hit, float(r), rank)
        x = jnp.where(hit, -jnp.inf, x)
    return out, rank


def _peer_keys_kernel(ht_ref, wq_ref, wk_ref, cnt_ref, e1_ref, rank2_ref, e2_ref, st_ref):
    qt = jnp.dot(wq_ref[...], ht_ref[...], preferred_element_type=F32)
    st_ref[...] = jnp.dot(wk_ref[...], qt.astype(BF16), preferred_element_type=F32)

    def head(h, carry):
        base = pl.multiple_of(h * 2 * PEER_NKEYS, 2 * PEER_NKEYS)
        s1 = st_ref[pl.ds(base, PEER_NKEYS), :]
        s2 = st_ref[pl.ds(base + PEER_NKEYS, PEER_NKEYS), :]
        assert PEER_TOPK == 16
        t1, _ = _top_rows(s1, PEER_TOPK)
        t2_rows, rank2 = _top_rows(s2, PEER_TOPK)
        t2 = jnp.concatenate(t2_rows, axis=0)
        cand = jnp.concatenate(
            [t1[0] + t2, t1[1] + t2[0:8], t1[2] + t2[0:8], t1[3] + t2[0:8],
             t1[4] + t2[0:4], t1[5] + t2[0:4], t1[6] + t2[0:4], t1[7] + t2[0:4],
             jnp.concatenate(t1[8:], axis=0) + t2[0:1]], axis=0)
        cmax = t1[0] + t2[0:1]
        cur = cand
        tot = jnp.zeros_like(cmax)
        thr = cmax
        for _ in range(PEER_TOPK):
            m = jnp.max(cur, axis=0, keepdims=True)
            hit = cur == m
            tot_new = tot + jnp.sum(jnp.where(hit, 1.0, 0.0), axis=0, keepdims=True)
            thr = jnp.where(jnp.logical_and(tot < PEER_TOPK, tot_new >= PEER_TOPK), m, thr)
            tot = tot_new
            cur = jnp.where(hit, -jnp.inf, cur)
        z = jnp.sum(jnp.where(cand >= thr, jnp.exp(cand - cmax), 0.0), axis=0, keepdims=True)
        cnt = jnp.zeros_like(s1)
        for b in range(PEER_TOPK):
            cnt = cnt + jnp.where(s1 + t2_rows[b] >= thr, 1.0, 0.0)
        cnt_ref[h] = cnt
        e1_ref[h] = jnp.exp(s1 - t1[0]) / z
        rank2_ref[h] = rank2.astype(BF16)
        e2_ref[h] = jnp.exp(s2 - t2[0:1]).astype(BF16)
        return carry

    lax.fori_loop(0, PEER_HEADS, head, 0)


def _peer_keys(ht16, wq_t, wk_t, *, tn):
    t = ht16.shape[1]
    shape = lambda dt: jax.ShapeDtypeStruct((PEER_HEADS, PEER_NKEYS, t), dt)
    bspec = pl.BlockSpec((PEER_HEADS, PEER_NKEYS, tn), lambda i: (0, 0, i))
    return pl.pallas_call(
        _peer_keys_kernel,
        grid=(t // tn,),
        in_specs=[pl.BlockSpec((D_MODEL, tn), lambda i: (0, i)), _resident(wq_t.shape), _resident(wk_t.shape)],
        out_specs=[bspec, bspec, bspec, bspec],
        out_shape=[shape(F32), shape(F32), shape(BF16), shape(BF16)],
        scratch_shapes=[pltpu.VMEM((PEER_HEADS * 2 * PEER_NKEYS, tn), F32)],
        compiler_params=_cparams(("parallel",)), name="peer_keys",
    )(ht16, wq_t, wk_t)


def _gelu_tanh(x):
    return 0.5 * x * (1.0 + jnp.tanh(math.sqrt(2.0 / math.pi) * (x + 0.044715 * (x * x * x))))


def _peer_mix_kernel(ht_ref, u_ref, vt_ref, cnt_ref, e1_ref, rank2_ref, e2_ref, ft_ref, g_ref, *, i_per_chunk):
    c = pl.program_id(1)
    tn = ht_ref.shape[1]
    pack = 2 * SUBLANES

    @pl.when(c == 0)
    def _():
        ft_ref[...] = jnp.zeros(ft_ref.shape, F32)

    def row_tile(ref, h, i):
        one = jnp.broadcast_to(ref[h, pl.ds(i, 1), :], (pack, tn)).astype(BF16)
        return jnp.concatenate([one] * (PEER_NKEYS // pack), axis=0)

    n_piece = 4
    ipp = i_per_chunk // n_piece
    rows = ipp * PEER_NKEYS
    ht = ht_ref[...]
    pre = [jnp.dot(u_ref[q * rows:(q + 1) * rows, :], ht, preferred_element_type=F32) for q in range(n_piece)]
    for q in range(n_piece):
        for ii in range(ipp):
            i = c * i_per_chunk + q * ipp + ii
            w = None
            for h in range(PEER_HEADS):
                gate = jnp.where(rank2_ref[h] < row_tile(cnt_ref, h, i), e2_ref[h] * row_tile(e1_ref, h, i),
                                 jnp.zeros((), BF16))
                w = gate if w is None else w + gate
            act = _gelu_tanh(pre[q][ii * PEER_NKEYS:(ii + 1) * PEER_NKEYS, :]).astype(BF16)
            g_ref[q * rows + ii * PEER_NKEYS:q * rows + (ii + 1) * PEER_NKEYS, :] = w * act
        ft_ref[...] += jnp.dot(vt_ref[:, q * rows:(q + 1) * rows], g_ref[q * rows:(q + 1) * rows, :],
                               preferred_element_type=F32)


def _peer_mix(ht16, u16, vt16, cnt, e1, rank2, e2, *, tn, i_per_chunk):
    t = ht16.shape[1]
    ce = i_per_chunk * PEER_NKEYS
    n_exp = u16.shape[0]
    bspec = pl.BlockSpec((PEER_HEADS, PEER_NKEYS, tn), lambda i, c: (0, 0, i))
    return pl.pallas_call(
        functools.partial(_peer_mix_kernel, i_per_chunk=i_per_chunk),
        grid=(t // tn, n_exp // ce),
        in_specs=[pl.BlockSpec((D_MODEL, tn), lambda i, c: (0, i)),
                  pl.BlockSpec((ce, D_MODEL), lambda i, c: (c, 0)),
                  pl.BlockSpec((D_MODEL, ce), lambda i, c: (0, c)),
                  bspec, bspec, bspec, bspec],
        out_specs=pl.BlockSpec((D_MODEL, tn), lambda i, c: (0, i)),
        out_shape=jax.ShapeDtypeStruct((D_MODEL, t), F32),
        scratch_shapes=[pltpu.VMEM((ce, tn), BF16)],
        compiler_params=_cparams(("parallel", "arbitrary")), name="peer_mix",
    )(ht16, u16, vt16, cnt, e1, rank2, e2)


def _final_kernel(h_ref, h16_ref, f_ref, p_ref, wg_ref, wp_ref, g_ref, b_ref, o_ref, *, alpha):
    gate = jax.nn.sigmoid(jnp.dot(h16_ref[...], wg_ref[...], preferred_element_type=F32))
    e = gate * jnp.dot(p_ref[...].astype(BF16), wp_ref[...], preferred_element_type=F32)
    o_ref[...] = _layer_norm(alpha * h_ref[...] + f_ref[...] + e, g_ref[...], b_ref[...])


def _final(h, h16, f, p2d, wg, wp, g, b, *, alpha, tm):
    t = h.shape[0]
    rspec = lambda n: pl.BlockSpec((tm, n), lambda i: (i, 0))
    return pl.pallas_call(
        functools.partial(_final_kernel, alpha=alpha),
        grid=(t // tm,),
        in_specs=[rspec(D_MODEL), rspec(D_MODEL), rspec(D_MODEL), rspec(PLE_DIM),
                  _resident((D_MODEL, D_MODEL)), _resident((PLE_DIM, D_MODEL)),
                  _resident((1, D_MODEL)), _resident((1, D_MODEL))],
        out_specs=rspec(D_MODEL),
        out_shape=jax.ShapeDtypeStruct((t, D_MODEL), F32),
        compiler_params=_cparams(("parallel",)), name="ple_ln2",
    )(h, h16, f, p2d, wg, wp, g, b)


def _pick(n, prefs):
    for c in prefs:
        if n % c == 0:
            return c
    return n


def _tiles(t):
    return dict(proj=_pick(t, (512, 256, 128)), tq=_pick(t, (256, 128)), tk=_pick(t, (1024, 512, 256, 128)),
                sb=_pick(t, (256, 128)), rows=_pick(t, (256, 128)), peer=_pick(t, (512, 256, 128)))


def _layer_weights(l, w_in, w_branch, w_out, ln1_g, ln1_b, w_pq, peer_sub_keys, peer_u, peer_v, ln2_g, ln2_b,
                   w_ple_gate, w_ple_proj):
    sk = peer_sub_keys[l]
    half = PEER_DQ // 2
    blk = jnp.zeros((2 * PEER_NKEYS, PEER_DQ), F32)
    blk = blk.at[:PEER_NKEYS, :half].set(sk[0]).at[PEER_NKEYS:, half:].set(sk[1])
    return dict(
        w_perm=_permute_w_in(w_in[l]),
        wba=w_branch[l, :W_ATT].astype(BF16), wbb=w_branch[l, W_ATT:].astype(BF16), wo=w_out[l].astype(BF16),
        g1=ln1_g[l][None], b1=ln1_b[l][None], g2=ln2_g[l][None], b2=ln2_b[l][None],
        wq_t=w_pq[l].T.astype(BF16),
        wk_t=jnp.kron(jnp.eye(PEER_HEADS, dtype=F32), blk).astype(BF16),
        u16=peer_u[l].astype(BF16), vt16=peer_v[l].T.astype(BF16),
        wg=w_ple_gate[l].astype(BF16), wp=w_ple_proj[l].astype(BF16),
    )


def _token_tail(x2d, p2d, oa, ob, sga, sgb, lw, alpha):
    tl = _tiles(x2d.shape[0])
    h, h16 = _merge(oa, ob, sga, sgb, x2d, lw["wba"], lw["wbb"], lw["wo"], lw["g1"], lw["b1"], alpha=alpha,
                    tm=tl["rows"])
    ht16 = h16.T
    cnt, e1, rank2, e2 = _peer_keys(ht16, lw["wq_t"], lw["wk_t"], tn=tl["peer"])
    ft = _peer_mix(ht16, lw["u16"], lw["vt16"], cnt, e1, rank2, e2, tn=tl["peer"], i_per_chunk=8)
    return _final(h, h16, ft.T, p2d, lw["wg"], lw["wp"], lw["g2"], lw["b2"], alpha=alpha, tm=tl["rows"])


def _stack_heads(a_hm, dec_b):
    h, _, d = a_hm.shape
    return a_hm.reshape(h, dec_b, SAMPLE_ROWS, d).transpose(1, 0, 2, 3).reshape(dec_b, h * SAMPLE_ROWS, d)


def _unstack_heads(o, dec_b):
    return o.reshape(dec_b, N_HEADS, SAMPLE_ROWS, HEAD_DIM).transpose(0, 2, 1, 3).reshape(dec_b * SAMPLE_ROWS, W_ATT)


def _new_page(a_hm, dec_b, page):
    h, _, d = a_hm.shape
    a = a_hm.reshape(h, dec_b, SAMPLE_ROWS, d).transpose(1, 0, 3, 2)
    return jnp.pad(a, ((0, 0), (0, 0), (0, 0), (0, page - SAMPLE_ROWS)))


def kernel(x_prompt, x_sample, p_prompt, p_sample, cache_a_k, cache_a_v, cache_idx_k, cache_b_k, cache_b_v,
           page_table, w_in, w_branch, w_out, ln1_g, ln1_b, w_pq, peer_sub_keys, peer_u, peer_v, ln2_g, ln2_b,
           w_ple_gate, w_ple_proj):
    depth = w_in.shape[0]
    alpha = (2.0 * depth) ** 0.25
    n_batch, seq, _ = x_prompt.shape
    dec_b, dec_t, _ = x_sample.shape
    page = cache_a_k.shape[2]
    n_pages = page_table.shape[1]
    n_past = n_pages * page
    rows = SAMPLE_ROWS
    assert dec_t <= rows and page == LANES and cache_a_k.shape[3:] == (N_HEADS, HEAD_DIM)

    slot_minor = lambda c: jnp.transpose(c, (0, 1, 3, 4, 2))
    cak, cav, cbk, cbv = (slot_minor(c) for c in (cache_a_k, cache_a_v, cache_b_k, cache_b_v))
    cik = jnp.transpose(cache_idx_k, (0, 1, 3, 2))

    pos_p = jnp.arange(seq, dtype=I32)
    pos_row = n_past + jnp.minimum(jnp.arange(rows, dtype=I32), dec_t - 1)
    pos_s = jnp.tile(pos_row, dec_b)
    g_pages = _pick(n_pages, (16, 8, 4, 2, 1))

    xp = x_prompt
    xs = jnp.pad(x_sample, ((0, 0), (0, rows - dec_t), (0, 0)), mode="edge").reshape(dec_b * rows, D_MODEL)
    shapes = ((N_HEADS, HEAD_DIM), (N_HEADS, HEAD_DIM), (D_IDX,), (N_HEADS, HEAD_DIM), (N_HEADS, HEAD_DIM))
    outs_p = [[] for _ in range(5)]
    outs_s = [[] for _ in range(5)]
    for l in range(depth):
        lw = _layer_weights(l, w_in, w_branch, w_out, ln1_g, ln1_b, w_pq, peer_sub_keys, peer_u, peer_v, ln2_g,
                            ln2_b, w_ple_gate, w_ple_proj)
        tl = _tiles(seq)
        new_xp = []
        per_b = [[] for _ in range(5)]
        for b in range(n_batch):
            x2d = xp[b]
            (qa, ka, ka16, va, va16, qis, ki, ki16, wi, qb, kb, kb16, vb, vb16, sga, sgb) = _project(
                x2d, lw["w_perm"], pos_p, tl["proj"])
            oa_t = _dsa_prompt(jnp.swapaxes(qis, 1, 2), wi.T, ki16, jnp.swapaxes(qa, 1, 2), ka16,
                               jnp.swapaxes(va16, 1, 2), tq=tl["tq"], tk=tl["tk"])
            ob = _sb_prompt(qb, kb16, vb16, tq=tl["sb"], tk=tl["sb"], near_blocks=3)
            new_xp.append(_token_tail(x2d, p_prompt[l, b], oa_t.T, ob, sga, sgb, lw, alpha))
            for dst, a in zip(per_b, (ka, va, ki, kb, vb)):
                dst.append(a)
        xp = jnp.stack(new_xp)
        for dst, a, sh in zip(outs_p, per_b, shapes):
            dst.append(jnp.stack(a).reshape((n_batch, seq) + sh))

        (qa, ka, ka16, va, va16, qis, ki, ki16, wi, qb, kb, kb16, vb, vb16, sga, sgb) = _project(
            xs, lw["w_perm"], pos_s, _tiles(dec_b * rows)["proj"])
        w_rep = jnp.broadcast_to(
            wi.reshape(dec_b, rows, H_IDX).transpose(0, 2, 1).reshape(dec_b, STACK, 1), (dec_b, STACK, LANES))
        ki_new = jnp.pad(ki16.reshape(dec_b, rows, D_IDX).transpose(0, 2, 1), ((0, 0), (0, 0), (0, page - rows)))
        oa = _dsa_sample(page_table, l, _stack_heads(qis, dec_b), w_rep, _stack_heads(qa, dec_b), ki_new,
                         _new_page(ka16, dec_b, page), _new_page(va16, dec_b, page), cik, cak, cav, n_new=dec_t,
                         g_pages=g_pages)
        ob = _sb_sample(page_table, l, _stack_heads(qb, dec_b), _new_page(kb16, dec_b, page),
                        _new_page(vb16, dec_b, page), cbk, cbv, n_new=dec_t, n_near=4)
        ps = jnp.pad(p_sample[l], ((0, 0), (0, rows - dec_t), (0, 0)), mode="edge").reshape(dec_b * rows, PLE_DIM)
        xs = _token_tail(xs, ps, _unstack_heads(oa, dec_b), _unstack_heads(ob, dec_b), sga, sgb, lw, alpha)
        for dst, a, sh in zip(outs_s, (ka, va, ki, kb, vb), shapes):
            dst.append(a.reshape((dec_b, rows) + sh)[:, :dec_t])

    y_sample = xs.reshape(dec_b, rows, D_MODEL)[:, :dec_t]
    return (xp, y_sample) + tuple(jnp.stack(o) for o in outs_p) + tuple(jnp.stack(o) for o in outs_s)
```

```python
import functools
import math

import numpy as np
import jax
import jax.numpy as jnp
from jax import lax
from jax.experimental import pallas as pl
from jax.experimental.pallas import tpu as pltpu

F32 = jnp.float32
BF16 = jnp.bfloat16
I32 = jnp.int32

D_MODEL = 1024
HEAD_DIM = 64
N_HEADS = 8
W_ATT = N_HEADS * HEAD_DIM
ROPE_THETA = 500000.0
ROT_64 = 16
ROT_32 = 8
H_IDX = 8
D_IDX = 32
TOPK_MAX = 256
PEER_HEADS = 8
PEER_NKEYS = 128
PEER_DQ = 128
PEER_TOPK = 16
PLE_DIM = 256
LN_EPS = 1e-5
IN_SIZES = (W_ATT, W_ATT, W_ATT, H_IDX * D_IDX, D_IDX, H_IDX, W_ATT, W_ATT, W_ATT, D_MODEL, D_MODEL)

LANES = 128
SUBLANES = 8
VMEM_LIMIT_BYTES = 56 * 1024 * 1024
INT_MIN = -(2 ** 31)
NO_TIE_CUT = 2 ** 30
NEG_BIG = -1e30
M_INIT = -1e20
SB_DEAD = -104.0
LOG2_E = 1.4426950408889634

C_QA, C_KA, C_VA, C_QI, C_KW = 0, 512, 1024, 1536, 1792
C_QB, C_KB, C_VB, C_GA, C_GB, N_COLS = 1920, 2432, 2944, 3456, 4480, 5504

SAMPLE_ROWS = SUBLANES
STACK = N_HEADS * SAMPLE_ROWS


def _cparams(sem):
    return pltpu.CompilerParams(dimension_semantics=sem, vmem_limit_bytes=VMEM_LIMIT_BYTES)


def _resident(shape):
    zeros = (0,) * len(shape)
    return pl.BlockSpec(shape, lambda *_: zeros, pipeline_mode=pl.Buffered(1))


def _rope_tables(pos):
    posf = pos.astype(F32)

    def cos_sin(r):
        half = r // 2
        inv = ROPE_THETA ** (-(jnp.arange(half, dtype=F32) * 2.0 / r))
        ang = posf[:, None] * inv[None, :]
        return jnp.cos(ang), jnp.sin(ang)

    def lanes(cos, sin, head_dim, half, n_rot_lanes):
        lane = np.arange(LANES)
        m = lane % head_dim
        first = (m < half) & (lane < n_rot_lanes)
        second = (m >= half) & (m < 2 * half) & (lane < n_rot_lanes)
        idx = np.where(first, m, np.where(second, m - half, 0))
        c = jnp.where(first | second, cos[:, idx], 1.0)
        sa = jnp.where(first, -sin[:, idx], 0.0)
        sb = jnp.where(second, sin[:, idx], 0.0)
        return [c, sa, sb]

    c64, s64 = cos_sin(ROT_64)
    c32, s32 = cos_sin(ROT_32)
    tabs = (lanes(c64, s64, HEAD_DIM, ROT_64 // 2, LANES) + lanes(c32, s32, D_IDX, ROT_32 // 2, LANES)
            + lanes(c32, s32, D_IDX, ROT_32 // 2, D_IDX))
    return jnp.stack(tabs).astype(F32)


def _proj_kernel(x_ref, w_ref, tab_ref, qa_ref, ka_ref, ka16_ref, va_ref, va16_ref, qi_ref, ki_ref, ki16_ref,
                 wi_ref, qb_ref, kb_ref, kb16_ref, vb_ref, vb16_ref, sga_ref, sgb_ref):
    xb = x_ref[...].astype(BF16)

    def mm(c0, n):
        return jnp.dot(xb, w_ref[:, c0:c0 + n], preferred_element_type=F32)

    def rope(z, k, sh):
        out = []
        for c in range(z.shape[1] // LANES):
            zc = z[:, c * LANES:(c + 1) * LANES]
            out.append(zc * tab_ref[3 * k] + pltpu.roll(zc, LANES - sh, 1) * tab_ref[3 * k + 1]
                       + pltpu.roll(zc, sh, 1) * tab_ref[3 * k + 2])
        return out[0] if len(out) == 1 else jnp.concatenate(out, axis=1)

    def heads_major(ref, z, width):
        for h in range(z.shape[1] // width):
            ref[h] = z[:, h * width:(h + 1) * width].astype(BF16)

    heads_major(qa_ref, rope(mm(C_QA, W_ATT), 0, ROT_64 // 2), HEAD_DIM)
    z = rope(mm(C_KA, W_ATT), 0, ROT_64 // 2)
    ka_ref[...] = z
    heads_major(ka16_ref, z, HEAD_DIM)
    z = mm(C_VA, W_ATT)
    va_ref[...] = z
    heads_major(va16_ref, z, HEAD_DIM)
    heads_major(qi_ref, rope(mm(C_QI, H_IDX * D_IDX), 1, ROT_32 // 2), D_IDX)
    r = rope(mm(C_KW, LANES), 2, ROT_32 // 2)
    ki_ref[...] = r[:, :D_IDX]
    ki16_ref[...] = r[:, :D_IDX].astype(BF16)
    wi_ref[...] = r[:, D_IDX:D_IDX + H_IDX]
    heads_major(qb_ref, mm(C_QB, W_ATT), HEAD_DIM)
    z = mm(C_KB, W_ATT)
    kb_ref[...] = z
    heads_major(kb16_ref, z, HEAD_DIM)
    z = mm(C_VB, W_ATT)
    vb_ref[...] = z
    heads_major(vb16_ref, z, HEAD_DIM)
    sga_ref[...] = jax.nn.sigmoid(mm(C_GA, D_MODEL)).astype(BF16)
    sgb_ref[...] = jax.nn.sigmoid(mm(C_GB, D_MODEL)).astype(BF16)


def _permute_w_in(w_in):
    qa, ka, va, qi, ki, wi, qb, kb, vb, ga, gb = jnp.split(w_in, np.cumsum(IN_SIZES)[:-1].tolist(), axis=1)
    scale = 1.0 / math.sqrt(HEAD_DIM)
    pad = jnp.zeros((D_MODEL, LANES - D_IDX - H_IDX), w_in.dtype)
    w = jnp.concatenate([qa * (scale * LOG2_E), ka, va, qi, ki, wi, pad, qb * scale, kb, vb, ga, gb], axis=1)
    assert w.shape[1] == N_COLS
    return w.astype(BF16)


def _project(x2d, w_perm, pos, tm):
    t = x2d.shape[0]
    tabs = _rope_tables(pos)
    rm = lambda n, dt: (jax.ShapeDtypeStruct((t, n), dt), pl.BlockSpec((tm, n), lambda i: (i, 0)))
    hm = lambda d: (jax.ShapeDtypeStruct((N_HEADS, t, d), BF16), pl.BlockSpec((N_HEADS, tm, d), lambda i: (0, i, 0)))
    outs = [hm(HEAD_DIM), rm(W_ATT, F32), hm(HEAD_DIM), rm(W_ATT, F32), hm(HEAD_DIM), hm(D_IDX), rm(D_IDX, F32),
            rm(D_IDX, BF16), rm(H_IDX, F32), hm(HEAD_DIM), rm(W_ATT, F32), hm(HEAD_DIM), rm(W_ATT, F32),
            hm(HEAD_DIM), rm(D_MODEL, BF16), rm(D_MODEL, BF16)]
    return pl.pallas_call(
        _proj_kernel,
        grid=(t // tm,),
        in_specs=[pl.BlockSpec((tm, D_MODEL), lambda i: (i, 0)), _resident((D_MODEL, N_COLS)),
                  pl.BlockSpec((9, tm, LANES), lambda i: (0, i, 0))],
        out_specs=[s for _, s in outs],
        out_shape=[s for s, _ in outs],
        compiler_params=_cparams(("parallel",)),
        name="proj_rope",
    )(x2d, w_perm, tabs)


def _sortable_key(score):
    b = lax.bitcast_convert_type(score, I32)
    return jnp.where(b < 0, b ^ jnp.int32(0x7FFFFFFF), b)


def _sortable_key_inv(key):
    return jnp.where(key < 0, key ^ jnp.int32(0x7FFFFFFF), key)


def _bracket_start(count_ge, k, k_min, k_max, n_finite, shape):
    few = n_finite <= k
    neg_inf_key = _sortable_key(jnp.full(shape, -jnp.inf, F32))
    lo = jnp.where(few, neg_inf_key, k_min)
    hi = jnp.where(few, neg_inf_key + 1, k_max + 1)
    c_lo = jnp.where(few, jnp.float32(k), n_finite)
    c_hi = jnp.zeros(shape, F32)
    for zk in (0, 1):
        c = count_ge(jnp.full(shape, zk, I32))
        up = jnp.logical_and(c >= k, lo < zk)
        dn = jnp.logical_and(c < k, hi > zk)
        lo, c_lo = jnp.where(up, zk, lo), jnp.where(up, c, c_lo)
        hi, c_hi = jnp.where(dn, zk, hi), jnp.where(dn, c, c_hi)
    return lo, hi, c_lo, c_hi


def _bracket_threshold(count_ge, k, lo, hi, c_lo, c_hi):
    kf = jnp.float32(k)
    log_k = math.log(k - 0.5)

    def unfinished(lo, hi, c_lo):
        return jnp.logical_and(c_lo != kf, hi > lo + 1)

    def cond(st):
        t, lo, hi, c_lo, _ = st
        return jnp.logical_and(t < 100, jnp.max(jnp.where(unfinished(lo, hi, c_lo), 1.0, 0.0)) > 0.0)

    def body(st):
        t, lo, hi, c_lo, c_hi = st
        lo_f = lax.bitcast_convert_type(_sortable_key_inv(lo), F32)
        hi_f = lax.bitcast_convert_type(_sortable_key_inv(hi), F32)
        la, lb = jnp.log(c_lo), jnp.log(jnp.maximum(c_hi, 0.25))
        frac = (la - log_k) / (la - lb)
        mid_f = lo_f + (hi_f - lo_f) * frac
        cand_i = _sortable_key(jnp.where(mid_f == mid_f, mid_f, lo_f))
        cand_b = (lo >> 1) + (hi >> 1) + (lo & hi & 1)
        cand = jnp.where(t % 3 == 2, cand_b, cand_i)
        cand = jnp.minimum(jnp.maximum(cand, lo + 1), hi - 1)
        cand = jnp.where(unfinished(lo, hi, c_lo), cand, lo)
        c = count_ge(cand)
        up = c >= kf
        return (t + 1, jnp.where(up, cand, lo), jnp.where(up, hi, cand), jnp.where(up, c, c_lo),
                jnp.where(up, c_hi, c))

    _, lo, _, c_lo, c_hi = lax.while_loop(cond, body, (jnp.int32(0), lo, hi, c_lo, c_hi))
    return lo, c_lo, c_hi


def _tie_cut_search(count_tied_le, need, active, last_idx, n_tied):
    hi = jnp.broadcast_to(last_idx, need.shape).astype(I32)
    lo = jnp.where(active, jnp.int32(-1), hi - 1)

    def unfinished(lo, hi, h_hi):
        return jnp.logical_and(h_hi != need, hi > lo + 1)

    def cond(st):
        t, lo, hi, _, h_hi = st
        return jnp.logical_and(t < 100, jnp.max(jnp.where(unfinished(lo, hi, h_hi), 1.0, 0.0)) > 0.0)

    def body(st):
        t, lo, hi, h_lo, h_hi = st
        span = (hi - lo).astype(F32)
        step = jnp.ceil(span * (need - h_lo - 0.5) / jnp.maximum(h_hi - h_lo, 1.0)).astype(I32)
        cand = jnp.where(t % 3 == 2, lo + ((hi - lo) >> 1), lo + step)
        cand = jnp.minimum(jnp.maximum(cand, lo + 1), hi - 1)
        cand = jnp.where(unfinished(lo, hi, h_hi), cand, hi)
        c = count_tied_le(cand)
        dn = c >= need
        return (t + 1, jnp.where(dn, lo, cand), jnp.where(dn, cand, hi), jnp.where(dn, h_lo, c),
                jnp.where(dn, c, h_hi))

    _, _, hi, _, _ = lax.while_loop(cond, body, (jnp.int32(0), lo, hi, jnp.zeros_like(need), n_tied))
    return hi


def _reduce_keys(x, op, width=64):
    n, w = x.shape
    if n % width == 0 and n > width:
        x = op(x.reshape(n // width, width, w), axis=0)
    return op(x, axis=0, keepdims=True)


def _selected(key, vt, jt, idx, on, off):
    return jnp.where(key > vt, on, jnp.where(key == vt, jnp.where(idx <= jt, on, off), off))


def _tri_pairs(t, tq, tk, reverse, j_window=None):
    qi, kj, first, last = [], [], [], []
    for i in range(t // tq):
        jd = ((i + 1) * tq - 1) // tk
        js = list(range(jd + 1))
        if j_window is not None:
            js = [j for j in js if j_window[0] <= jd - j < j_window[1]]
        if reverse:
            js = js[::-1]
        for n, j in enumerate(js):
            qi.append(i), kj.append(j), first.append(int(n == 0)), last.append(int(n == len(js) - 1))
    return tuple(np.asarray(a, np.int32) for a in (qi, kj, first, last))


def _dsa_prompt_kernel(qi_s, kj_s, first_s, last_s, qis_ref, wi_ref, ki_ref, qa_ref, ka_ref, va_ref, o_ref,
                       sc_ref, vthr_ref, jthr_ref, m_ref, l_ref, acc_ref, *, tq, tk, n_top):
    p = pl.program_id(0)
    i = qi_s[p]
    j = kj_s[p]
    q_idx = i * tq + lax.broadcasted_iota(I32, (tk, tq), 1)
    k_off = lax.broadcasted_iota(I32, (tk, tq), 0)

    @pl.when(first_s[p] == 1)
    def _():
        m_ref[...] = jnp.full(m_ref.shape, M_INIT, F32)
        l_ref[...] = jnp.zeros(l_ref.shape, F32)
        acc_ref[...] = jnp.zeros(acc_ref.shape, F32)
        n_chunks = ((i + 1) * tq + tk - 1) // tk
        w = wi_ref[...]

        cw = min(tk, 64)

        def score_chunk(c, carry):
            hi_s, lo_s = carry
            k_blk = ki_ref[pl.ds(pl.multiple_of(c * tk, tk), tk), :]
            score = jnp.zeros((tk, tq), F32)
            for h in range(H_IDX):
                s = jnp.dot(k_blk, qis_ref[h], preferred_element_type=F32)
                score = score + w[h:h + 1, :] * jnp.maximum(s, 0.0)
            score = jnp.where(c * tk + k_off <= q_idx, score, -jnp.inf)
            sc_ref[c] = _sortable_key(score)
            for r in range(tk // cw):
                blk = score[r * cw:(r + 1) * cw, :]
                hi_s = jnp.maximum(hi_s, blk)
                lo_s = jnp.minimum(lo_s, jnp.where(blk > -jnp.inf, blk, jnp.inf))
            return hi_s, lo_s

        hi_s, lo_s = lax.fori_loop(0, n_chunks, score_chunk,
                                   (jnp.full((cw, tq), -jnp.inf, F32), jnp.full((cw, tq), jnp.inf, F32)))
        k_max = _sortable_key(jnp.max(hi_s, axis=0, keepdims=True))
        k_min = _sortable_key(jnp.min(lo_s, axis=0, keepdims=True))

        def key_count(ones_fn):
            def body(c, acc):
                for r in range(tk // cw):
                    acc = acc + ones_fn(sc_ref[c, r * cw:(r + 1) * cw, :], c * tk + r * cw)
                return acc
            acc = lax.fori_loop(0, n_chunks, body, jnp.zeros((cw, tq), F32))
            return jnp.sum(acc, axis=0, keepdims=True)

        def count_ge(cand):
            cand_b = jnp.broadcast_to(cand, (cw, tq))
            return key_count(lambda blk, k0: jnp.where(blk >= cand_b, 1.0, 0.0))

        n_causal = (i * tq + lax.broadcasted_iota(I32, (1, tq), 1) + 1).astype(F32)
        lo0, hi0, c_lo0, c_hi0 = _bracket_start(count_ge, n_top, k_min, k_max, n_causal, (1, tq))
        v, cnt, above = _bracket_threshold(count_ge, n_top, lo0, hi0, c_lo0, c_hi0)
        vthr_ref[...] = jnp.broadcast_to(v, (SUBLANES, tq))
        jthr_ref[...] = jnp.full((SUBLANES, tq), NO_TIE_CUT, I32)

        @pl.when(jnp.max(cnt) > n_top)
        def _():
            need = n_top - above
            tied = cnt > n_top
            v_b = jnp.broadcast_to(v, (cw, tq))
            off = lax.broadcasted_iota(I32, (cw, tq), 0)

            def count_tied_le(x):
                x_b = jnp.broadcast_to(x, (cw, tq))
                return key_count(
                    lambda blk, k0: jnp.where(blk == v_b, jnp.where(k0 + off <= x_b, 1.0, 0.0), 0.0))
            cut = _tie_cut_search(count_tied_le, need, tied, n_chunks * tk - 1, cnt - above)
            jthr_ref[...] = jnp.broadcast_to(jnp.where(tied, cut, NO_TIE_CUT), (SUBLANES, tq))

    k_idx = j * tk + k_off
    sel = jnp.where(k_idx <= q_idx,
                    _selected(sc_ref[j], vthr_ref[0:1, :], jthr_ref[0:1, :], k_idx, 0.0, NEG_BIG), NEG_BIG)
    m_all = m_ref[...]
    l_all = l_ref[...]
    m_out, l_out = [], []

    def logits(h):
        return jnp.dot(ka_ref[h], qa_ref[h], preferred_element_type=F32) + sel

    ahead = 2
    pending = [logits(h) for h in range(ahead)]
    for h in range(N_HEADS):
        if h % ahead == 0 and h + ahead < N_HEADS:
            pending += [logits(h + ahead + a) for a in range(ahead)]
        s = pending.pop(0)
        m_prev = m_all[h:h + 1, :]
        m_new = jnp.maximum(m_prev, _reduce_keys(s, jnp.max))
        pr = jnp.exp2(s - m_new)
        alpha = jnp.exp2(m_prev - m_new)
        l_out.append(alpha * l_all[h:h + 1, :] + _reduce_keys(pr, jnp.sum))
        acc_ref[h] = alpha * acc_ref[h] + jnp.dot(va_ref[h], pr.astype(BF16), preferred_element_type=F32)
        m_out.append(m_new)
    m_ref[...] = jnp.concatenate(m_out, axis=0)
    l_ref[...] = jnp.concatenate(l_out, axis=0)

    @pl.when(last_s[p] == 1)
    def _():
        for h in range(N_HEADS):
            o_ref[h * HEAD_DIM:(h + 1) * HEAD_DIM, :] = (acc_ref[h] / l_ref[h:h + 1, :]).astype(o_ref.dtype)


def _dsa_prompt(qis_t, wi_t, ki16, qa_t, ka16, va_t, *, tq, tk):
    t = ki16.shape[0]
    n_top = min(TOPK_MAX, t // 4)
    pairs = _tri_pairs(t, tq, tk, reverse=False)
    kern = functools.partial(_dsa_prompt_kernel, tq=tq, tk=tk, n_top=n_top)
    qlane = lambda p, qi, kj, fi, la: (0, 0, qi[p])
    grid_spec = pltpu.PrefetchScalarGridSpec(
        num_scalar_prefetch=4,
        grid=(int(pairs[0].shape[0]),),
        in_specs=[
            pl.BlockSpec((H_IDX, D_IDX, tq), qlane),
            pl.BlockSpec((H_IDX, tq), lambda p, qi, kj, fi, la: (0, qi[p])),
            pl.BlockSpec((t, D_IDX), lambda p, qi, kj, fi, la: (0, 0)),
            pl.BlockSpec((N_HEADS, HEAD_DIM, tq), qlane),
            pl.BlockSpec((N_HEADS, tk, HEAD_DIM), lambda p, qi, kj, fi, la: (0, kj[p], 0)),
            pl.BlockSpec((N_HEADS, HEAD_DIM, tk), lambda p, qi, kj, fi, la: (0, 0, kj[p])),
        ],
        out_specs=pl.BlockSpec((W_ATT, tq), lambda p, qi, kj, fi, la: (0, qi[p])),
        scratch_shapes=[
            pltpu.VMEM((t // tk, tk, tq), I32),
            pltpu.VMEM((SUBLANES, tq), I32),
            pltpu.VMEM((SUBLANES, tq), I32),
            pltpu.VMEM((N_HEADS, tq), F32),
            pltpu.VMEM((N_HEADS, tq), F32),
            pltpu.VMEM((N_HEADS, HEAD_DIM, tq), F32),
        ],
    )
    return pl.pallas_call(
        kern, grid_spec=grid_spec, out_shape=jax.ShapeDtypeStruct((W_ATT, t), BF16),
        compiler_params=_cparams(("arbitrary",)), name="dsa_prompt",
    )(*pairs, qis_t, wi_t, ki16, qa_t, ka16, va_t)


def _log_sigmoid_neg(z):
    return -(jnp.maximum(z, 0.0) + jnp.log1p(jnp.exp(-jnp.abs(z))))


def _sum_of_later(lg, tri):
    hi = lg.astype(BF16)
    lo = (lg - hi.astype(F32)).astype(BF16)
    return jnp.dot(hi, tri, preferred_element_type=F32) + jnp.dot(lo, tri, preferred_element_type=F32)


def _later_matrix(n):
    r = np.arange(n)
    return jnp.asarray((r[:, None] > r[None, :]).astype(np.float32)).astype(BF16)


def _sb_prompt_kernel(qi_s, kj_s, first_s, last_s, q_ref, k_ref, v_ref, tri_ref, *rest, tq, tk, resume):
    if resume:
        acc_in_ref, carry_in_ref, acc_out_ref, carry_out_ref, carry_ref, acc_ref, done_ref = rest
    else:
        acc_out_ref, carry_out_ref, carry_ref, acc_ref, done_ref = rest
    p = pl.program_id(0)
    i = qi_s[p]
    j = kj_s[p]

    @pl.when(first_s[p] == 1)
    def _():
        if resume:
            worst = None
            for h in range(N_HEADS):
                acc_ref[h] = acc_in_ref[:, h * HEAD_DIM:(h + 1) * HEAD_DIM]
                c = carry_in_ref[:, h:h + 1]
                carry_ref[h] = c
                worst = c if worst is None else jnp.maximum(worst, c)
            done_ref[0] = (jnp.max(worst) < SB_DEAD).astype(I32)
        else:
            carry_ref[...] = jnp.zeros(carry_ref.shape, F32)
            acc_ref[...] = jnp.zeros(acc_ref.shape, F32)
            done_ref[0] = 0

    @pl.when(done_ref[0] == 0)
    def _():
        row = i * tq + lax.broadcasted_iota(I32, (tq, tk), 0)
        col = j * tk + lax.broadcasted_iota(I32, (tq, tk), 1)
        mask = col < row
        tri = tri_ref[...]
        zs = [lax.dot_general(q_ref[h], k_ref[h], (((1,), (1,)), ((), ())), preferred_element_type=F32)
              for h in range(N_HEADS)]
        lsms = [_log_sigmoid_neg(z) for z in zs]
        lgs = [jnp.where(mask, lsm, 0.0) for lsm in lsms]
        laters = [_sum_of_later(lg, tri) for lg in lgs]
        worst = None
        for h in range(N_HEADS):
            lg, later = lgs[h], laters[h]
            carry = carry_ref[h]
            a = jnp.where(mask, jnp.exp(lsms[h] + zs[h] + later + carry), 0.0)
            acc_ref[h] = acc_ref[h] + jnp.dot(a.astype(BF16), v_ref[h], preferred_element_type=F32)
            carry = carry + later[:, 0:1] + lg[:, 0:1]
            carry_ref[h] = carry
            worst = carry if worst is None else jnp.maximum(worst, carry)
        done_ref[0] = (jnp.max(worst) < SB_DEAD).astype(I32)

    @pl.when(last_s[p] == 1)
    def _():
        for h in range(N_HEADS):
            acc_out_ref[:, h * HEAD_DIM:(h + 1) * HEAD_DIM] = acc_ref[h]
            carry_out_ref[:, h:h + 1] = carry_ref[h]


def _sb_prompt_call(pairs, qb, kb16, vb16, tri, state, *, tq, tk):
    t = qb.shape[1]
    resume = state is not None
    qmap = lambda p, qi, kj, fi, la: (0, qi[p], 0)
    kmap = lambda p, qi, kj, fi, la: (0, kj[p], 0)
    rmap = lambda p, qi, kj, fi, la: (qi[p], 0)
    in_specs = [pl.BlockSpec((N_HEADS, tq, HEAD_DIM), qmap), pl.BlockSpec((N_HEADS, tk, HEAD_DIM), kmap),
                pl.BlockSpec((N_HEADS, tk, HEAD_DIM), kmap), pl.BlockSpec((tk, tk), lambda p, *_: (0, 0))]
    args = [qb, kb16, vb16, tri]
    if resume:
        in_specs += [pl.BlockSpec((tq, W_ATT), rmap), pl.BlockSpec((tq, N_HEADS), rmap)]
        args += list(state)
    grid_spec = pltpu.PrefetchScalarGridSpec(
        num_scalar_prefetch=4,
        grid=(int(pairs[0].shape[0]),),
        in_specs=in_specs,
        out_specs=[pl.BlockSpec((tq, W_ATT), rmap), pl.BlockSpec((tq, N_HEADS), rmap)],
        scratch_shapes=[pltpu.VMEM((N_HEADS, tq, 1), F32), pltpu.VMEM((N_HEADS, tq, HEAD_DIM), F32),
                        pltpu.SMEM((1,), I32)],
    )
    return pl.pallas_call(
        functools.partial(_sb_prompt_kernel, tq=tq, tk=tk, resume=resume),
        grid_spec=grid_spec,
        out_shape=[jax.ShapeDtypeStruct((t, W_ATT), F32), jax.ShapeDtypeStruct((t, N_HEADS), F32)],
        input_output_aliases=({4 + 4: 0, 4 + 5: 1} if resume else {}),
        compiler_params=_cparams(("arbitrary",)), name="sb_prompt_far" if resume else "sb_prompt_near",
    )(*pairs, *args)


def _sb_prompt(qb, kb16, vb16, *, tq, tk, near_blocks):
    t = qb.shape[1]
    tri = _later_matrix(tk)
    near = _tri_pairs(t, tq, tk, True, (0, near_blocks))
    far = _tri_pairs(t, tq, tk, True, (near_blocks, t))
    acc, carry = _sb_prompt_call(near, qb, kb16, vb16, tri, None, tq=tq, tk=tk)
    if far[0].shape[0] == 0:
        return acc
    first_row = int(far[0][0]) * tq
    alive = jnp.max(carry[first_row:]) >= SB_DEAD
    return lax.cond(
        alive,
        lambda a, c: _sb_prompt_call(far, qb, kb16, vb16, tri, (a, c), tq=tq, tk=tk)[0],
        lambda a, c: a, acc, carry)


def _dsa_sample_kernel(pt_s, qis_ref, w_ref, qa_ref, kin_ref, kan_ref, van_ref, *rest, g_pages, n_pages, n_new,
                       n_top):
    kip = rest[:g_pages]
    kap = rest[g_pages:2 * g_pages]
    vap = rest[2 * g_pages:3 * g_pages]
    o_ref, sc_ref, vthr_ref, jthr_ref, m_ref, l_ref, acc_ref = rest[3 * g_pages:]
    s = pl.program_id(1)
    n_steps = n_pages // g_pages
    n_blocks = n_pages + 1
    r_new = jnp.minimum(lax.broadcasted_iota(I32, (SAMPLE_ROWS, LANES), 0), n_new - 1)
    lane = lax.broadcasted_iota(I32, (SAMPLE_ROWS, LANES), 1)

    def scores(k_t):
        n = k_t.shape[1] // LANES
        s1 = jnp.dot(qis_ref[0], k_t, preferred_element_type=F32)
        wt = w_ref[0] if n == 1 else jnp.concatenate([w_ref[0]] * n, axis=1)
        t = wt * jnp.maximum(s1, 0.0)
        score = t[0:SAMPLE_ROWS]
        for h in range(1, H_IDX):
            score = score + t[h * SAMPLE_ROWS:(h + 1) * SAMPLE_ROWS]
        return score

    @pl.when(s < n_steps)
    def _():
        key = _sortable_key(scores(jnp.concatenate([r[0, 0].astype(BF16) for r in kip], axis=1)))
        for g in range(g_pages):
            sc_ref[s * g_pages + g] = key[:, g * LANES:(g + 1) * LANES]

    @pl.when(s == n_steps)
    def _():
        m_ref[...] = jnp.full(m_ref.shape, M_INIT, F32)
        l_ref[...] = jnp.zeros(l_ref.shape, F32)
        acc_ref[...] = jnp.zeros(acc_ref.shape, F32)
        key = _sortable_key(scores(kin_ref[0]))
        neg_inf_key = _sortable_key(jnp.full((SAMPLE_ROWS, LANES), -jnp.inf, F32))
        sc_ref[n_pages] = jnp.where(lane < n_new, jnp.where(lane <= r_new, key, neg_inf_key), jnp.int32(INT_MIN))
        keys = sc_ref[...]
        idx = (lax.broadcasted_iota(I32, (n_blocks, SAMPLE_ROWS, LANES), 0) * LANES
               + lax.broadcasted_iota(I32, (n_blocks, SAMPLE_ROWS, LANES), 2))

        def count(hit):
            ones = jnp.where(hit, 1.0, 0.0)
            parts = [jnp.sum(ones[b0:b0 + 16], axis=0) for b0 in range(0, n_blocks, 16)]
            while len(parts) > 1:
                parts = [sum(parts[k:k + 2]) for k in range(0, len(parts), 2)]
            return jnp.sum(parts[0], axis=1, keepdims=True)

        def count_ge(cand):
            return count(keys >= cand[None])

        scores_back = lax.bitcast_convert_type(_sortable_key_inv(keys), F32)
        finite = keys > neg_inf_key[None]

        def over_keys(x, op):
            return op(op(x, axis=0), axis=1, keepdims=True)

        k_max = _sortable_key(over_keys(jnp.where(finite, scores_back, -jnp.inf), jnp.max))
        k_min = _sortable_key(over_keys(jnp.where(finite, scores_back, jnp.inf), jnp.min))
        shape = (SAMPLE_ROWS, 1)
        lo0, hi0, c_lo0, c_hi0 = _bracket_start(count_ge, n_top, k_min, k_max, count(finite), shape)
        v, cnt, above = _bracket_threshold(count_ge, n_top, lo0, hi0, c_lo0, c_hi0)
        vthr_ref[...] = jnp.broadcast_to(v, (SAMPLE_ROWS, LANES))
        jthr_ref[...] = jnp.full((SAMPLE_ROWS, LANES), NO_TIE_CUT, I32)

        @pl.when(jnp.max(cnt) > n_top)
        def _():
            tied = cnt > n_top

            def count_tied_le(x):
                return count(jnp.logical_and(keys == v[None], idx <= x[None]))
            cut = _tie_cut_search(count_tied_le, n_top - above, tied, n_blocks * LANES - 1, cnt - above)
            jthr_ref[...] = jnp.broadcast_to(jnp.where(tied, cut, NO_TIE_CUT), (SAMPLE_ROWS, LANES))

    def chosen(blk_idx):
        return _selected(sc_ref[blk_idx], vthr_ref[...], jthr_ref[...], blk_idx * LANES + lane, 0.0, NEG_BIG)

    def attend(k_t, v_t, sel):
        q = qa_ref[0]
        sc = jnp.concatenate(
            [jnp.dot(q[h * SAMPLE_ROWS:(h + 1) * SAMPLE_ROWS], k_t[h], preferred_element_type=F32) + sel
             for h in range(N_HEADS)], axis=0)
        m_prev = m_ref[...]
        m_new = jnp.maximum(m_prev, jnp.max(sc, axis=1, keepdims=True))
        pr = jnp.exp2(sc - m_new)
        alpha = jnp.exp2(m_prev - m_new)
        l_ref[...] = alpha * l_ref[...] + jnp.sum(pr, axis=1, keepdims=True)
        pv = jnp.concatenate(
            [lax.dot_general(pr[h * SAMPLE_ROWS:(h + 1) * SAMPLE_ROWS].astype(BF16), v_t[h],
                             (((1,), (1,)), ((), ())), preferred_element_type=F32) for h in range(N_HEADS)], axis=0)
        acc_ref[...] = alpha * acc_ref[...] + pv
        m_ref[...] = m_new

    def head_pages(refs, h):
        return jnp.concatenate([r[0, 0, h].astype(BF16) for r in refs], axis=1)

    @pl.when(jnp.logical_and(s > n_steps, s <= 2 * n_steps))
    def _():
        p0 = (s - n_steps - 1) * g_pages
        attend([head_pages(kap, h) for h in range(N_HEADS)], [head_pages(vap, h) for h in range(N_HEADS)],
               jnp.concatenate([chosen(p0 + g) for g in range(g_pages)], axis=1))

    @pl.when(s == 2 * n_steps + 1)
    def _():
        sel = jnp.where(jnp.logical_and(lane <= r_new, lane < n_new), chosen(n_pages), NEG_BIG)
        attend([kan_ref[0, h] for h in range(N_HEADS)], [van_ref[0, h] for h in range(N_HEADS)], sel)
        o_ref[0] = (acc_ref[...] / l_ref[...]).astype(o_ref.dtype)


def _dsa_sample(page_table, layer, qis, w_rep, qa, ki_new, ka_new, va_new, cache_ik, cache_ak, cache_av, *,
                n_new, g_pages):
    b, n_pages = page_table.shape
    page = cache_ak.shape[-1]
    assert page == LANES and n_pages % g_pages == 0
    n_steps = n_pages // g_pages
    n_top = min(TOPK_MAX, (n_pages * page + n_new) // 4)
    kern = functools.partial(_dsa_sample_kernel, g_pages=g_pages, n_pages=n_pages, n_new=n_new, n_top=n_top)
    seq3 = lambda i, s, pt: (i, 0, 0)
    seq4 = lambda i, s, pt: (i, 0, 0, 0)

    def score_page(g):
        return lambda i, s, pt: (layer, pt[i, jnp.minimum(s, n_steps - 1) * g_pages + g], 0, 0)

    def attend_page(g):
        return lambda i, s, pt: (layer, pt[i, jnp.clip(s - n_steps - 1, 0, n_steps - 1) * g_pages + g], 0, 0, 0)

    page_blk = (1, 1, N_HEADS, HEAD_DIM, page)
    grid_spec = pltpu.PrefetchScalarGridSpec(
        num_scalar_prefetch=1,
        grid=(b, 2 * n_steps + 2),
        in_specs=[pl.BlockSpec((1, STACK, D_IDX), seq3), pl.BlockSpec((1, STACK, LANES), seq3),
                  pl.BlockSpec((1, STACK, HEAD_DIM), seq3), pl.BlockSpec((1, D_IDX, page), seq3),
                  pl.BlockSpec((1, N_HEADS, HEAD_DIM, page), seq4), pl.BlockSpec((1, N_HEADS, HEAD_DIM, page), seq4)]
        + [pl.BlockSpec((1, 1, D_IDX, page), score_page(g)) for g in range(g_pages)]
        + [pl.BlockSpec(page_blk, attend_page(g)) for g in range(g_pages)] * 2,
        out_specs=pl.BlockSpec((1, STACK, HEAD_DIM), seq3),
        scratch_shapes=[
            pltpu.VMEM((n_pages + 1, SAMPLE_ROWS, LANES), I32),
            pltpu.VMEM((SAMPLE_ROWS, LANES), I32),
            pltpu.VMEM((SAMPLE_ROWS, LANES), I32),
            pltpu.VMEM((STACK, 1), F32),
            pltpu.VMEM((STACK, 1), F32),
            pltpu.VMEM((STACK, HEAD_DIM), F32),
        ],
    )
    return pl.pallas_call(
        kern, grid_spec=grid_spec, out_shape=jax.ShapeDtypeStruct((b, STACK, HEAD_DIM), BF16),
        compiler_params=_cparams(("arbitrary", "arbitrary")), name="dsa_sample",
    )(page_table, qis, w_rep, qa, ki_new, ka_new, va_new,
      *([cache_ik] * g_pages), *([cache_ak] * g_pages), *([cache_av] * g_pages))


def _sb_stack_block(q, k_t, v_t, valid, tri, carry, acc):
    z = jnp.concatenate([jnp.dot(q[h * SAMPLE_ROWS:(h + 1) * SAMPLE_ROWS], k_t[h], preferred_element_type=F32)
                         for h in range(N_HEADS)], axis=0)
    lsm = _log_sigmoid_neg(z)
    lg = lsm if valid is None else jnp.where(valid, lsm, 0.0)
    later = _sum_of_later(lg, tri)
    a = jnp.exp(lsm + z + later + carry)
    if valid is not None:
        a = jnp.where(valid, a, 0.0)
    pv = jnp.concatenate(
        [lax.dot_general(a[h * SAMPLE_ROWS:(h + 1) * SAMPLE_ROWS].astype(BF16), v_t[h],
                         (((1,), (1,)), ((), ())), preferred_element_type=F32) for h in range(N_HEADS)], axis=0)
    return carry + jnp.sum(lg, axis=1, keepdims=True), acc + pv


def _page_heads(ref):
    return [ref[0, 0, h].astype(BF16) for h in range(N_HEADS)]


def _sb_sample_near_kernel(pt_s, q_ref, kn_ref, vn_ref, tri_ref, *rest, n_near, n_new):
    kp = rest[:n_near]
    vp = rest[n_near:2 * n_near]
    acc_ref, carry_ref = rest[2 * n_near:]
    q = q_ref[0]
    tri = tri_ref[...]
    row = lax.broadcasted_iota(I32, (STACK, LANES), 0)
    lane = lax.broadcasted_iota(I32, (STACK, LANES), 1)
    valid = lane < jnp.minimum(row % SAMPLE_ROWS, n_new - 1)
    carry, acc = _sb_stack_block(q, [kn_ref[0, h] for h in range(N_HEADS)], [vn_ref[0, h] for h in range(N_HEADS)],
                                 valid, tri, jnp.zeros((STACK, 1), F32), jnp.zeros((STACK, HEAD_DIM), F32))
    for g in range(n_near):
        carry, acc = _sb_stack_block(q, _page_heads(kp[g]), _page_heads(vp[g]), None, tri, carry, acc)
    acc_ref[0] = acc
    carry_ref[0] = jnp.broadcast_to(carry, (STACK, LANES))


def _sb_sample_far_kernel(pt_s, q_ref, tri_ref, kp_ref, vp_ref, acc_in_ref, carry_in_ref, acc_ref, carry_ref,
                          done_ref):
    s = pl.program_id(1)

    @pl.when(s == 0)
    def _():
        acc_ref[...] = acc_in_ref[...]
        carry_ref[...] = carry_in_ref[...]
        done_ref[0] = (jnp.max(carry_in_ref[...]) < SB_DEAD).astype(I32)

    @pl.when(done_ref[0] == 0)
    def _():
        carry, acc = _sb_stack_block(q_ref[0], _page_heads(kp_ref), _page_heads(vp_ref), None, tri_ref[...],
                                     carry_ref[0][:, 0:1], acc_ref[0])
        acc_ref[0] = acc
        carry_ref[0] = jnp.broadcast_to(carry, (STACK, LANES))
        done_ref[0] = (jnp.max(carry) < SB_DEAD).astype(I32)


def _sb_sample(page_table, layer, qb, kb_new, vb_new, cache_bk, cache_bv, *, n_new, n_near):
    b, n_pages = page_table.shape
    page = cache_bk.shape[-1]
    n_near = min(n_near, n_pages)
    tri = _later_matrix(page)
    seq3 = lambda i, *_: (i, 0, 0)
    seq4 = lambda i, *_: (i, 0, 0, 0)
    const2 = lambda *_: (0, 0)
    page_blk = (1, 1, N_HEADS, HEAD_DIM, page)
    new_blk = (1, N_HEADS, HEAD_DIM, page)
    state_shapes = [jax.ShapeDtypeStruct((b, STACK, HEAD_DIM), F32), jax.ShapeDtypeStruct((b, STACK, LANES), F32)]
    state_specs = [pl.BlockSpec((1, STACK, HEAD_DIM), seq3), pl.BlockSpec((1, STACK, LANES), seq3)]

    def near_page(g):
        return lambda i, pt: (layer, pt[i, n_pages - 1 - g], 0, 0, 0)

    near_spec = pltpu.PrefetchScalarGridSpec(
        num_scalar_prefetch=1, grid=(b,),
        in_specs=[pl.BlockSpec((1, STACK, HEAD_DIM), seq3), pl.BlockSpec(new_blk, seq4), pl.BlockSpec(new_blk, seq4),
                  pl.BlockSpec((page, page), const2)]
        + [pl.BlockSpec(page_blk, near_page(g)) for g in range(n_near)] * 2,
        out_specs=state_specs)
    acc, carry = pl.pallas_call(
        functools.partial(_sb_sample_near_kernel, n_near=n_near, n_new=n_new),
        grid_spec=near_spec, out_shape=state_shapes,
        compiler_params=_cparams(("arbitrary",)), name="sb_sample_near",
    )(page_table, qb, kb_new, vb_new, tri, *([cache_bk] * n_near), *([cache_bv] * n_near))
    n_far = n_pages - n_near
    if n_far == 0:
        return acc

    far_page = lambda i, s, pt: (layer, pt[i, n_far - 1 - s], 0, 0, 0)
    far_spec = pltpu.PrefetchScalarGridSpec(
        num_scalar_prefetch=1, grid=(b, n_far),
        in_specs=[pl.BlockSpec((1, STACK, HEAD_DIM), seq3), pl.BlockSpec((page, page), const2),
                  pl.BlockSpec(page_blk, far_page), pl.BlockSpec(page_blk, far_page)] + state_specs,
        out_specs=state_specs, scratch_shapes=[pltpu.SMEM((1,), I32)])

    def far(a, c):
        return pl.pallas_call(
            _sb_sample_far_kernel, grid_spec=far_spec, out_shape=state_shapes,
            compiler_params=_cparams(("arbitrary", "arbitrary")), name="sb_sample_far",
        )(page_table, qb, tri, cache_bk, cache_bv, a, c)[0]

    return lax.cond(jnp.max(carry) >= SB_DEAD, far, lambda a, c: a, acc, carry)


def _layer_norm(y, g, b):
    mu = jnp.mean(y, axis=1, keepdims=True)
    d = y - mu
    var = jnp.mean(d * d, axis=1, keepdims=True)
    return d * lax.rsqrt(var + LN_EPS) * g + b


def _merge_kernel(oa_ref, ob_ref, sga_ref, sgb_ref, x_ref, wba_ref, wbb_ref, wo_ref, g_ref, b_ref, h_ref, h16_ref,
                  *, alpha):
    oa = jnp.dot(oa_ref[...].astype(BF16), wba_ref[...], preferred_element_type=F32)
    ob = jnp.dot(ob_ref[...].astype(BF16), wbb_ref[...], preferred_element_type=F32)
    mix = sga_ref[...].astype(F32) * oa + sgb_ref[...].astype(F32) * ob
    y = alpha * x_ref[...] + jnp.dot(mix.astype(BF16), wo_ref[...], preferred_element_type=F32)
    h = _layer_norm(y, g_ref[...], b_ref[...])
    h_ref[...] = h
    h16_ref[...] = h.astype(BF16)


def _merge(oa, ob, sga, sgb, x2d, wba, wbb, wo, g, b, *, alpha, tm):
    t = x2d.shape[0]
    rspec = lambda n: pl.BlockSpec((tm, n), lambda i: (i, 0))
    return pl.pallas_call(
        functools.partial(_merge_kernel, alpha=alpha),
        grid=(t // tm,),
        in_specs=[rspec(W_ATT), rspec(W_ATT), rspec(D_MODEL), rspec(D_MODEL), rspec(D_MODEL),
                  _resident((W_ATT, D_MODEL)), _resident((W_ATT, D_MODEL)), _resident((D_MODEL, D_MODEL)),
                  _resident((1, D_MODEL)), _resident((1, D_MODEL))],
        out_specs=[rspec(D_MODEL), rspec(D_MODEL)],
        out_shape=[jax.ShapeDtypeStruct((t, D_MODEL), F32), jax.ShapeDtypeStruct((t, D_MODEL), BF16)],
        compiler_params=_cparams(("parallel",)), name="merge_ln1",
    )(oa, ob, sga, sgb, x2d, wba, wbb, wo, g, b)


def _top_rows(x, n):
    out = []
    rank = jnp.full(x.shape, float(n), F32)
    for r in range(n):
        m = jnp.max(x, axis=0, keepdims=True)
        out.append(m)
        hit = x == m
        rank = jnp.where(hit, float(r), rank)
        x = jnp.where(hit, -jnp.inf, x)
    return out, rank


def _peer_keys_kernel(ht_ref, wq_ref, wk_ref, cnt_ref, e1_ref, rank2_ref, e2_ref, st_ref):
    qt = jnp.dot(wq_ref[...], ht_ref[...], preferred_element_type=F32)
    st_ref[...] = jnp.dot(wk_ref[...], qt.astype(BF16), preferred_element_type=F32)

    def head(h, carry):
        base = pl.multiple_of(h * 2 * PEER_NKEYS, 2 * PEER_NKEYS)
        s1 = st_ref[pl.ds(base, PEER_NKEYS), :]
        s2 = st_ref[pl.ds(base + PEER_NKEYS, PEER_NKEYS), :]
        assert PEER_TOPK == 16
        t1, _ = _top_rows(s1, PEER_TOPK)
        t2_rows, rank2 = _top_rows(s2, PEER_TOPK)
        t2 = jnp.concatenate(t2_rows, axis=0)
        cand = jnp.concatenate(
            [t1[0] + t2, t1[1] + t2[0:8], t1[2] + t2[0:8], t1[3] + t2[0:8],
             t1[4] + t2[0:4], t1[5] + t2[0:4], t1[6] + t2[0:4], t1[7] + t2[0:4],
             jnp.concatenate(t1[8:], axis=0) + t2[0:1]], axis=0)
        cmax = t1[0] + t2[0:1]
        cur = cand
        tot = jnp.zeros_like(cmax)
        thr = cmax
        for _ in range(PEER_TOPK):
            m = jnp.max(cur, axis=0, keepdims=True)
            hit = cur == m
            tot_new = tot + jnp.sum(jnp.where(hit, 1.0, 0.0), axis=0, keepdims=True)
            thr = jnp.where(jnp.logical_and(tot < PEER_TOPK, tot_new >= PEER_TOPK), m, thr)
            tot = tot_new
            cur = jnp.where(hit, -jnp.inf, cur)
        z = jnp.sum(jnp.where(cand >= thr, jnp.exp(cand - cmax), 0.0), axis=0, keepdims=True)
        cnt = jnp.zeros_like(s1)
        for b in range(PEER_TOPK):
            cnt = cnt + jnp.where(s1 + t2_rows[b] >= thr, 1.0, 0.0)
        cnt_ref[h] = cnt
        e1_ref[h] = jnp.exp(s1 - t1[0]) / z
        rank2_ref[h] = rank2.astype(BF16)
        e2_ref[h] = jnp.exp(s2 - t2[0:1]).astype(BF16)
        return carry

    lax.fori_loop(0, PEER_HEADS, head, 0)


def _peer_keys(ht16, wq_t, wk_t, *, tn):
    t = ht16.shape[1]
    shape = lambda dt: jax.ShapeDtypeStruct((PEER_HEADS, PEER_NKEYS, t), dt)
    bspec = pl.BlockSpec((PEER_HEADS, PEER_NKEYS, tn), lambda i: (0, 0, i))
    return pl.pallas_call(
        _peer_keys_kernel,
        grid=(t // tn,),
        in_specs=[pl.BlockSpec((D_MODEL, tn), lambda i: (0, i)), _resident(wq_t.shape), _resident(wk_t.shape)],
        out_specs=[bspec, bspec, bspec, bspec],
        out_shape=[shape(F32), shape(F32), shape(BF16), shape(BF16)],
        scratch_shapes=[pltpu.VMEM((PEER_HEADS * 2 * PEER_NKEYS, tn), F32)],
        compiler_params=_cparams(("parallel",)), name="peer_keys",
    )(ht16, wq_t, wk_t)


def _gelu_tanh(x):
    return 0.5 * x * (1.0 + jnp.tanh(math.sqrt(2.0 / math.pi) * (x + 0.044715 * (x * x * x))))


def _peer_mix_kernel(ht_ref, u_ref, vt_ref, cnt_ref, e1_ref, rank2_ref, e2_ref, ft_ref, g_ref, *, i_per_chunk):
    c = pl.program_id(1)
    tn = ht_ref.shape[1]
    pack = 2 * SUBLANES

    @pl.when(c == 0)
    def _():
        ft_ref[...] = jnp.zeros(ft_ref.shape, F32)

    def row_tile(ref, h, i):
        one = jnp.broadcast_to(ref[h, pl.ds(i, 1), :], (pack, tn)).astype(BF16)
        return jnp.concatenate([one] * (PEER_NKEYS // pack), axis=0)

    n_piece = 4
    ipp = i_per_chunk // n_piece
    rows = ipp * PEER_NKEYS
    ht = ht_ref[...]
    pre = [jnp.dot(u_ref[q * rows:(q + 1) * rows, :], ht, preferred_element_type=F32) for q in range(n_piece)]
    for q in range(n_piece):
        for ii in range(ipp):
            i = c * i_per_chunk + q * ipp + ii
            w = None
            for h in range(PEER_HEADS):
                gate = jnp.where(rank2_ref[h] < row_tile(cnt_ref, h, i), e2_ref[h] * row_tile(e1_ref, h, i),
                                 jnp.zeros((), BF16))
                w = gate if w is None else w + gate
            act = _gelu_tanh(pre[q][ii * PEER_NKEYS:(ii + 1) * PEER_NKEYS, :]).astype(BF16)
            g_ref[q * rows + ii * PEER_NKEYS:q * rows + (ii + 1) * PEER_NKEYS, :] = w * act
        ft_ref[...] += jnp.dot(vt_ref[:, q * rows:(q + 1) * rows], g_ref[q * rows:(q + 1) * rows, :],
                               preferred_element_type=F32)


def _peer_mix(ht16, u16, vt16, cnt, e1, rank2, e2, *, tn, i_per_chunk):
    t = ht16.shape[1]
    ce = i_per_chunk * PEER_NKEYS
    n_exp = u16.shape[0]
    bspec = pl.BlockSpec((PEER_HEADS, PEER_NKEYS, tn), lambda i, c: (0, 0, i))
    return pl.pallas_call(
        functools.partial(_peer_mix_kernel, i_per_chunk=i_per_chunk),
        grid=(t // tn, n_exp // ce),
        in_specs=[pl.BlockSpec((D_MODEL, tn), lambda i, c: (0, i)),
                  pl.BlockSpec((ce, D_MODEL), lambda i, c: (c, 0)),
                  pl.BlockSpec((D_MODEL, ce), lambda i, c: (0, c)),
                  bspec, bspec, bspec, bspec],
        out_specs=pl.BlockSpec((D_MODEL, tn), lambda i, c: (0, i)),
        out_shape=jax.ShapeDtypeStruct((D_MODEL, t), F32),
        scratch_shapes=[pltpu.VMEM((ce, tn), BF16)],
        compiler_params=_cparams(("parallel", "arbitrary")), name="peer_mix",
    )(ht16, u16, vt16, cnt, e1, rank2, e2)


def _final_kernel(h_ref, h16_ref, f_ref, p_ref, wg_ref, wp_ref, g_ref, b_ref, o_ref, *, alpha):
    gate = jax.nn.sigmoid(jnp.dot(h16_ref[...], wg_ref[...], preferred_element_type=F32))
    e = gate * jnp.dot(p_ref[...].astype(BF16), wp_ref[...], preferred_element_type=F32)
    o_ref[...] = _layer_norm(alpha * h_ref[...] + f_ref[...] + e, g_ref[...], b_ref[...])


def _final(h, h16, f, p2d, wg, wp, g, b, *, alpha, tm):
    t = h.shape[0]
    rspec = lambda n: pl.BlockSpec((tm, n), lambda i: (i, 0))
    return pl.pallas_call(
        functools.partial(_final_kernel, alpha=alpha),
        grid=(t // tm,),
        in_specs=[rspec(D_MODEL), rspec(D_MODEL), rspec(D_MODEL), rspec(PLE_DIM),
                  _resident((D_MODEL, D_MODEL)), _resident((PLE_DIM, D_MODEL)),
                  _resident((1, D_MODEL)), _resident((1, D_MODEL))],
        out_specs=rspec(D_MODEL),
        out_shape=jax.ShapeDtypeStruct((t, D_MODEL), F32),
        compiler_params=_cparams(("parallel",)), name="ple_ln2",
    )(h, h16, f, p2d, wg, wp, g, b)


def _pick(n, prefs):
    for c in prefs:
        if n % c == 0:
            return c
    return n


def _tiles(t):
    return dict(proj=_pick(t, (512, 256, 128)), tq=_pick(t, (256, 128)), tk=_pick(t, (1024, 512, 256, 128)),
                sb=_pick(t, (256, 128)), rows=_pick(t, (256, 128)), peer=_pick(t, (512, 256, 128)))


def _layer_weights(l, w_in, w_branch, w_out, ln1_g, ln1_b, w_pq, peer_sub_keys, peer_u, peer_v, ln2_g, ln2_b,
                   w_ple_gate, w_ple_proj):
    sk = peer_sub_keys[l]
    half = PEER_DQ // 2
    blk = jnp.zeros((2 * PEER_NKEYS, PEER_DQ), F32)
    blk = blk.at[:PEER_NKEYS, :half].set(sk[0]).at[PEER_NKEYS:, half:].set(sk[1])
    return dict(
        w_perm=_permute_w_in(w_in[l]),
        wba=w_branch[l, :W_ATT].astype(BF16), wbb=w_branch[l, W_ATT:].astype(BF16), wo=w_out[l].astype(BF16),
        g1=ln1_g[l][None], b1=ln1_b[l][None], g2=ln2_g[l][None], b2=ln2_b[l][None],
        wq_t=w_pq[l].T.astype(BF16),
        wk_t=jnp.kron(jnp.eye(PEER_HEADS, dtype=F32), blk).astype(BF16),
        u16=peer_u[l].astype(BF16), vt16=peer_v[l].T.astype(BF16),
        wg=w_ple_gate[l].astype(BF16), wp=w_ple_proj[l].astype(BF16),
    )


def _token_tail(x2d, p2d, oa, ob, sga, sgb, lw, alpha):
    tl = _tiles(x2d.shape[0])
    h, h16 = _merge(oa, ob, sga, sgb, x2d, lw["wba"], lw["wbb"], lw["wo"], lw["g1"], lw["b1"], alpha=alpha,
                    tm=tl["rows"])
    ht16 = h16.T
    cnt, e1, rank2, e2 = _peer_keys(ht16, lw["wq_t"], lw["wk_t"], tn=tl["peer"])
    ft = _peer_mix(ht16, lw["u16"], lw["vt16"], cnt, e1, rank2, e2, tn=tl["peer"], i_per_chunk=8)
    return _final(h, h16, ft.T, p2d, lw["wg"], lw["wp"], lw["g2"], lw["b2"], alpha=alpha, tm=tl["rows"])


def _stack_heads(a_hm, dec_b):
    h, _, d = a_hm.shape
    return a_hm.reshape(h, dec_b, SAMPLE_ROWS, d).transpose(1, 0, 2, 3).reshape(dec_b, h * SAMPLE_ROWS, d)


def _unstack_heads(o, dec_b):
    return o.reshape(dec_b, N_HEADS, SAMPLE_ROWS, HEAD_DIM).transpose(0, 2, 1, 3).reshape(dec_b * SAMPLE_ROWS, W_ATT)


def _new_page(a_hm, dec_b, page):
    h, _, d = a_hm.shape
    a = a_hm.reshape(h, dec_b, SAMPLE_ROWS, d).transpose(1, 0, 3, 2)
    return jnp.pad(a, ((0, 0), (0, 0), (0, 0), (0, page - SAMPLE_ROWS)))


def kernel(x_prompt, x_sample, p_prompt, p_sample, cache_a_k, cache_a_v, cache_idx_k, cache_b_k, cache_b_v,
           page_table, w_in, w_branch, w_out, ln1_g, ln1_b, w_pq, peer_sub_keys, peer_u, peer_v, ln2_g, ln2_b,
           w_ple_gate, w_ple_proj):
    depth = w_in.shape[0]
    alpha = (2.0 * depth) ** 0.25
    n_batch, seq, _ = x_prompt.shape
    dec_b, dec_t, _ = x_sample.shape
    page = cache_a_k.shape[2]
    n_pages = page_table.shape[1]
    n_past = n_pages * page
    rows = SAMPLE_ROWS
    assert dec_t <= rows and page == LANES and cache_a_k.shape[3:] == (N_HEADS, HEAD_DIM)

    slot_minor = lambda c: jnp.transpose(c, (0, 1, 3, 4, 2))
    cak, cav, cbk, cbv = (slot_minor(c) for c in (cache_a_k, cache_a_v, cache_b_k, cache_b_v))
    cik = jnp.transpose(cache_idx_k, (0, 1, 3, 2))

    pos_p = jnp.arange(seq, dtype=I32)
    pos_row = n_past + jnp.minimum(jnp.arange(rows, dtype=I32), dec_t - 1)
    pos_s = jnp.tile(pos_row, dec_b)
    g_pages = _pick(n_pages, (16, 8, 4, 2, 1))

    xp = x_prompt
    xs = jnp.pad(x_sample, ((0, 0), (0, rows - dec_t), (0, 0)), mode="edge").reshape(dec_b * rows, D_MODEL)
    shapes = ((N_HEADS, HEAD_DIM), (N_HEADS, HEAD_DIM), (D_IDX,), (N_HEADS, HEAD_DIM), (N_HEADS, HEAD_DIM))
    outs_p = [[] for _ in range(5)]
    outs_s = [[] for _ in range(5)]
    for l in range(depth):
        lw = _layer_weights(l, w_in, w_branch, w_out, ln1_g, ln1_b, w_pq, peer_sub_keys, peer_u, peer_v, ln2_g,
                            ln2_b, w_ple_gate, w_ple_proj)
        tl = _tiles(seq)
        new_xp = []
        per_b = [[] for _ in range(5)]
        for b in range(n_batch):
            x2d = xp[b]
            (qa, ka, ka16, va, va16, qis, ki, ki16, wi, qb, kb, kb16, vb, vb16, sga, sgb) = _project(
                x2d, lw["w_perm"], pos_p, tl["proj"])
            oa_t = _dsa_prompt(jnp.swapaxes(qis, 1, 2), wi.T, ki16, jnp.swapaxes(qa, 1, 2), ka16,
                               jnp.swapaxes(va16, 1, 2), tq=tl["tq"], tk=tl["tk"])
            ob = _sb_prompt(qb, kb16, vb16, tq=tl["sb"], tk=tl["sb"], near_blocks=2)
            new_xp.append(_token_tail(x2d, p_prompt[l, b], oa_t.T, ob, sga, sgb, lw, alpha))
            for dst, a in zip(per_b, (ka, va, ki, kb, vb)):
                dst.append(a)
        xp = jnp.stack(new_xp)
        for dst, a, sh in zip(outs_p, per_b, shapes):
            dst.append(jnp.stack(a).reshape((n_batch, seq) + sh))

        (qa, ka, ka16, va, va16, qis, ki, ki16, wi, qb, kb, kb16, vb, vb16, sga, sgb) = _project(
            xs, lw["w_perm"], pos_s, _tiles(dec_b * rows)["proj"])
        w_rep = jnp.broadcast_to(
            wi.reshape(dec_b, rows, H_IDX).transpose(0, 2, 1).reshape(dec_b, STACK, 1), (dec_b, STACK, LANES))
        ki_new = jnp.pad(ki16.reshape(dec_b, rows, D_IDX).transpose(0, 2, 1), ((0, 0), (0, 0), (0, page - rows)))
        oa = _dsa_sample(page_table, l, _stack_heads(qis, dec_b), w_rep, _stack_heads(qa, dec_b), ki_new,
                         _new_page(ka16, dec_b, page), _new_page(va16, dec_b, page), cik, cak, cav, n_new=dec_t,
                         g_pages=g_pages)
        ob = _sb_sample(page_table, l, _stack_heads(qb, dec_b), _new_page(kb16, dec_b, page),
                        _new_page(vb16, dec_b, page), cbk, cbv, n_new=dec_t, n_near=4)
        ps = jnp.pad(p_sample[l], ((0, 0), (0, rows - dec_t), (0, 0)), mode="edge").reshape(dec_b * rows, PLE_DIM)
        xs = _token_tail(xs, ps, _unstack_heads(oa, dec_b), _unstack_heads(ob, dec_b), sga, sgb, lw, alpha)
        for dst, a, sh in zip(outs_s, (ka, va, ki, kb, vb), shapes):
            dst.append(a.reshape((dec_b, rows) + sh)[:, :dec_t])

    y_sample = xs.reshape(dec_b, rows, D_MODEL)[:, :dec_t]
    return (xp, y_sample) + tuple(jnp.stack(o) for o in outs_p) + tuple(jnp.stack(o) for o in outs_s)
```
